```python
import math
import jax
import jax.numpy as jnp
from jax import lax
import numpy as np

D_MODEL = 1024
BATCH = 8
SEQ = 4096
DEPTH = 1

GLA_HEADS = 4
GLA_DK = D_MODEL // (2 * GLA_HEADS)
GLA_DV = D_MODEL // GLA_HEADS
GLA_GATE_RANK = 16
GLA_TAU = 16.0
GLA_CHUNK = 64
DIFF_HEADS = 8
DIFF_DH = D_MODEL // (2 * DIFF_HEADS)
ROPE_THETA = 10000.0
Q_BLOCK = 128
N_GROUPS = 4
EXPERTS_PER_GROUP = 8
N_EXPERTS = N_GROUPS * EXPERTS_PER_GROUP
TOP_K_IN_GROUP = 2
EXPERT_FF = 512
MOE_BLOCK = 128
LN_EPS = 1e-5
RMS_EPS = 1e-6
IN_SPLITS = (GLA_HEADS * GLA_DK, GLA_HEADS * GLA_DK, GLA_HEADS * GLA_DV, GLA_HEADS * GLA_DV, GLA_GATE_RANK,
             DIFF_HEADS * 2 * DIFF_DH, DIFF_HEADS * 2 * DIFF_DH, DIFF_HEADS * 2 * DIFF_DH, 2 * D_MODEL)
IN_COLS = sum(IN_SPLITS)

kernel_name = "hybrid_gla_diffattn_hmoe_deepnorm_adaln"


def _split_points():
    pts, acc = [], 0
    for s in IN_SPLITS[:-1]:
        acc += s
        pts.append(acc)
    return pts


def _layer_norm(x):
    xf = x.astype(jnp.float32)
    mu = jnp.mean(xf, -1, keepdims=True)
    var = jnp.mean(jnp.square(xf - mu), -1, keepdims=True)
    return (xf - mu) * lax.rsqrt(var + LN_EPS)


def _rms_norm(x, w):
    xf = x.astype(jnp.float32)
    y = xf * lax.rsqrt(jnp.mean(jnp.square(xf), -1, keepdims=True) + RMS_EPS) * w.astype(jnp.float32)
    return y


def _rope(t, pos):
    half = t.shape[-1] // 2
    inv = ROPE_THETA ** (-jnp.arange(half, dtype=jnp.float32) / half)
    ang = pos.astype(jnp.float32)[:, :, None, None, None] * inv
    cos, sin = jnp.cos(ang), jnp.sin(ang)
    tf = t.astype(jnp.float32)
    t1, t2 = tf[..., :half], tf[..., half:]
    return jnp.concatenate([t1 * cos - t2 * sin, t2 * cos + t1 * sin], -1).astype(t.dtype)


def _gla_chunked(q, k, v, log_g):
    B, H, S, dk = q.shape
    dv = v.shape[-1]
    C = GLA_CHUNK
    N = S // C
    f32 = jnp.float32
    q = q.astype(f32).reshape(B, H, N, C, dk) * (dk ** -0.5)
    k = k.astype(f32).reshape(B, H, N, C, dk)
    v = v.astype(f32).reshape(B, H, N, C, dv)
    b = jnp.cumsum(log_g.astype(f32).reshape(B, H, N, C, dk), axis=3)
    b_last = b[:, :, :, -1:, :]
    q_t = q * jnp.exp(b)
    k_t = k * jnp.exp(-b)
    k_d = k * jnp.exp(b_last - b)
    causal = jnp.tril(jnp.ones((C, C), dtype=bool))
    attn = jnp.where(causal, jnp.einsum('bhnid,bhnjd->bhnij', q_t, k_t), 0.0)
    o_intra = jnp.einsum('bhnij,bhnjv->bhniv', attn, v)
    decay = jnp.exp(b_last[:, :, :, 0, :])

    def step(state, inp):
        qn, kn, vn, dn = inp
        o = jnp.einsum('bhid,bhdv->bhiv', qn, state)
        state = state * dn[..., None] + jnp.einsum('bhjd,bhjv->bhdv', kn, vn)
        return state, o

    xs = (jnp.moveaxis(q_t, 2, 0), jnp.moveaxis(k_d, 2, 0), jnp.moveaxis(v, 2, 0), jnp.moveaxis(decay, 2, 0))
    _, o_inter = lax.scan(step, jnp.zeros((B, H, dk, dv), f32), xs)
    o = o_intra + jnp.moveaxis(o_inter, 0, 2)
    return o.reshape(B, H, S, dv)


def _gla_branch(q_in, k_in, v_in, og_in, glr, w_g2, b_g2, norm_w):
    B, S, _ = q_in.shape
    log_g = jax.nn.log_sigmoid((glr @ w_g2 + b_g2).astype(jnp.float32)) / GLA_TAU

    def heads(t, d):
        return t.reshape(B, S, GLA_HEADS, d).transpose(0, 2, 1, 3)

    o = _gla_chunked(heads(q_in, GLA_DK), heads(k_in, GLA_DK), heads(v_in, GLA_DV), heads(log_g, GLA_DK))
    o = o.transpose(0, 2, 1, 3)
    o = _rms_norm(o, norm_w) * jax.nn.silu(og_in.astype(jnp.float32)).reshape(B, S, GLA_HEADS, GLA_DV)
    return o.reshape(B, S, GLA_HEADS * GLA_DV).astype(q_in.dtype)


def _diff_branch(q_in, k_in, v_in, pos, lq1, lk1, lq2, lk2, norm_w, lam_init):
    B, S, _ = q_in.shape
    H, dh = DIFF_HEADS, DIFF_DH
    nb = S // Q_BLOCK
    q = _rope(q_in.reshape(B, S, H, 2, dh), pos) * (dh ** -0.5)
    k = _rope(k_in.reshape(B, S, H, 2, dh), pos)
    q = jnp.moveaxis(q.transpose(0, 2, 3, 1, 4).reshape(B, H, 2, nb, Q_BLOCK, dh), 3, 0)
    k = k.transpose(0, 2, 3, 1, 4)
    v = v_in.reshape(B, S, H, 2 * dh).transpose(0, 2, 1, 3)
    f32 = jnp.float32
    lam = (jnp.exp(jnp.sum(lq1.astype(f32) * lk1.astype(f32)))
           - jnp.exp(jnp.sum(lq2.astype(f32) * lk2.astype(f32))) + lam_init)
    key_pos = jnp.arange(S)

    def block(args):
        qb, bi = args
        s = jnp.einsum('bhcqd,bhckd->bhcqk', qb, k, preferred_element_type=f32)
        q_pos = bi * Q_BLOCK + jnp.arange(Q_BLOCK)
        s = jnp.where(key_pos[None, :] <= q_pos[:, None], s, -jnp.inf)
        p = jax.nn.softmax(s, axis=-1)
        a = p[:, :, 0] - lam * p[:, :, 1]
        return jnp.einsum('bhqk,bhkd->bhqd', a.astype(v.dtype), v)

    o = lax.map(block, (q, jnp.arange(nb)))
    o = jnp.moveaxis(o, 0, 2).reshape(B, H, S, 2 * dh).transpose(0, 2, 1, 3)
    o = _rms_norm(o, norm_w) * (1.0 - lam_init)
    return o.reshape(B, S, H * 2 * dh).astype(q_in.dtype)


def _hier_moe(u, w_rg, b_rg, w_re, b_re, w_gate, w_up, w_down):
    B, S, D = u.shape
    T = B * S
    xf = u.reshape(T, D)
    f32 = jnp.float32
    p_group = jax.nn.softmax((xf @ w_rg + b_rg).astype(f32), axis=-1)
    w_grp, g_idx = lax.top_k(p_group, 1)
    e_logits = (xf @ w_re + b_re).astype(f32).reshape(T, N_GROUPS, EXPERTS_PER_GROUP)
    e_sel = jnp.take_along_axis(e_logits, g_idx[:, :, None], axis=1)[:, 0]
    v2, i2 = lax.top_k(e_sel, TOP_K_IN_GROUP)
    comb = w_grp * jax.nn.softmax(v2, axis=-1)
    eid = g_idx * EXPERTS_PER_GROUP + i2
    tk = T * TOP_K_IN_GROUP
    flat_e = eid.reshape(tk).astype(jnp.int32)
    flat_t = jnp.repeat(jnp.arange(T, dtype=jnp.int32), TOP_K_IN_GROUP)
    flat_w = comb.reshape(tk).astype(u.dtype)
    order = jnp.argsort(flat_e, stable=True)
    sorted_e = flat_e[order]
    counts = jnp.bincount(flat_e, length=N_EXPERTS)
    start = jnp.cumsum(counts) - counts
    pcounts = ((counts + MOE_BLOCK - 1) // MOE_BLOCK) * MOE_BLOCK
    pend = jnp.cumsum(pcounts)
    pstart = pend - pcounts
    dest = pstart[sorted_e] + (jnp.arange(tk) - start[sorted_e])
    P = ((tk + N_EXPERTS * (MOE_BLOCK - 1) + MOE_BLOCK - 1) // MOE_BLOCK) * MOE_BLOCK
    nblk = P // MOE_BLOCK
    row_token = jnp.zeros((P,), jnp.int32).at[dest].set(flat_t[order])
    row_w = jnp.zeros((P,), u.dtype).at[dest].set(flat_w[order])
    block_expert = jnp.clip(jnp.searchsorted(pend, jnp.arange(nblk) * MOE_BLOCK, side='right'),
                            0, N_EXPERTS - 1).astype(jnp.int32)
    xs = xf[row_token].reshape(nblk, MOE_BLOCK, D)

    def expert_block(args):
        xb, e = args
        h = jax.nn.silu(xb @ w_gate[e]) * (xb @ w_up[e])
        return h @ w_down[e]

    rows = lax.map(expert_block, (xs, block_expert)).reshape(P, D)
    y = jax.ops.segment_sum(rows * row_w[:, None], row_token, num_segments=T)
    return y.reshape(B, S, D)


def setup_inputs(seed: int = 0) -> dict:
    key = jax.random.key(seed)
    ks = jax.random.split(key, 26)
    f32 = jnp.float32
    beta = (8.0 * DEPTH) ** -0.25
    Ldim = DEPTH

    def nrm(k, shape, s):
        return jax.random.normal(k, shape, f32) * s

    return {
        "x": nrm(ks[0], (BATCH, SEQ, D_MODEL), 1.0),
        "c": nrm(ks[1], (BATCH, D_MODEL), 1.0),
        "positions": (jnp.arange(SEQ, dtype=jnp.int32)[None, :]
                      + jax.random.randint(ks[2], (BATCH, 1), 0, SEQ, dtype=jnp.int32)),
        "w_ada": nrm(ks[3], (Ldim, D_MODEL, 6 * D_MODEL), 0.5 * D_MODEL ** -0.5),
        "b_ada": nrm(ks[4], (Ldim, 6 * D_MODEL), 0.01),
        "w_in": nrm(ks[5], (Ldim, D_MODEL, IN_COLS), D_MODEL ** -0.5),
        "w_gla_gate2": nrm(ks[6], (Ldim, GLA_GATE_RANK, GLA_HEADS * GLA_DK), GLA_GATE_RANK ** -0.5),
        "b_gla_gate2": nrm(ks[7], (Ldim, GLA_HEADS * GLA_DK), 0.01),
        "gla_norm_w": 1.0 + nrm(ks[8], (Ldim, GLA_DV), 0.02),
        "diff_lambda_q1": nrm(ks[9], (Ldim, DIFF_DH), 0.1),
        "diff_lambda_k1": nrm(ks[10], (Ldim, DIFF_DH), 0.1),
        "diff_lambda_q2": nrm(ks[11], (Ldim, DIFF_DH), 0.1),
        "diff_lambda_k2": nrm(ks[12], (Ldim, DIFF_DH), 0.1),
        "diff_norm_w": 1.0 + nrm(ks[13], (Ldim, 2 * DIFF_DH), 0.02),
        "w_out": nrm(ks[14], (Ldim, D_MODEL, D_MODEL), beta * D_MODEL ** -0.5),
        "ln1_w": 1.0 + nrm(ks[15], (Ldim, D_MODEL), 0.02),
        "ln1_b": nrm(ks[16], (Ldim, D_MODEL), 0.02),
        "w_router_group": nrm(ks[17], (Ldim, D_MODEL, N_GROUPS), D_MODEL ** -0.5),
        "b_router_group": nrm(ks[18], (Ldim, N_GROUPS), 0.01),
        "w_router_expert": nrm(ks[19], (Ldim, D_MODEL, N_EXPERTS), D_MODEL ** -0.5),
        "b_router_expert": nrm(ks[20], (Ldim, N_EXPERTS), 0.01),
        "w_exp_gate": nrm(ks[21], (Ldim, N_EXPERTS, D_MODEL, EXPERT_FF), D_MODEL ** -0.5),
        "w_exp_up": nrm(ks[22], (Ldim, N_EXPERTS, D_MODEL, EXPERT_FF), D_MODEL ** -0.5),
        "w_exp_down": nrm(ks[23], (Ldim, N_EXPERTS, EXPERT_FF, D_MODEL), beta * EXPERT_FF ** -0.5),
        "ln2_w": 1.0 + nrm(ks[24], (Ldim, D_MODEL), 0.02),
        "ln2_b": nrm(ks[25], (Ldim, D_MODEL), 0.02),
    }


def reference(x, c, positions, w_ada, b_ada, w_in, w_gla_gate2, b_gla_gate2, gla_norm_w,
              diff_lambda_q1, diff_lambda_k1, diff_lambda_q2, diff_lambda_k2, diff_norm_w, w_out,
              ln1_w, ln1_b, w_router_group, b_router_group, w_router_expert, b_router_expert,
              w_exp_gate, w_exp_up, w_exp_down, ln2_w, ln2_b):
    alpha = (2.0 * DEPTH) ** 0.25
    dt = x.dtype
    for l in range(DEPTH):
        lam_init = 0.8 - 0.6 * math.exp(-0.3 * l)
        ada = jax.nn.silu(c) @ w_ada[l] + b_ada[l]
        shift1, scale1, gate1, shift2, scale2, gate2 = jnp.split(ada[:, None, :], 6, axis=-1)
        u = (_layer_norm(x) * (1.0 + scale1) + shift1).astype(dt)
        proj = u @ w_in[l]
        gq, gk, gv, go, gr, dq, dk, dv, mg = jnp.split(proj, _split_points(), axis=-1)
        o_gla = _gla_branch(gq, gk, gv, go, gr, w_gla_gate2[l], b_gla_gate2[l], gla_norm_w[l])
        o_diff = _diff_branch(dq, dk, dv, positions, diff_lambda_q1[l], diff_lambda_k1[l],
                              diff_lambda_q2[l], diff_lambda_k2[l], diff_norm_w[l], lam_init)
        g_a, g_b = jnp.split(jax.nn.sigmoid(mg), 2, axis=-1)
        y = (g_a * o_gla + g_b * o_diff) @ w_out[l]
        x = (_layer_norm(alpha * x + gate1 * y) * ln1_w[l] + ln1_b[l]).astype(dt)
        u2 = (_layer_norm(x) * (1.0 + scale2) + shift2).astype(dt)
        m = _hier_moe(u2, w_router_group[l], b_router_group[l], w_router_expert[l], b_router_expert[l],
                      w_exp_gate[l], w_exp_up[l], w_exp_down[l])
        x = (_layer_norm(alpha * x + gate2 * m) * ln2_w[l] + ln2_b[l]).astype(dt)
    return x
```

```python
import functools
import math

import jax
import jax.numpy as jnp
from jax import lax
from jax.experimental import pallas as pl
from jax.experimental.pallas import tpu as pltpu

F32 = jnp.float32
BF16 = jnp.bfloat16
HIGHEST = lax.Precision.HIGHEST

DEPTH = 1
GLA_HEADS = 4
GLA_GATE_RANK = 16
GLA_TAU = 16.0
GLA_CHUNK = 64
DIFF_HEADS = 8
ROPE_THETA = 10000.0
N_GROUPS = 4
EXPERTS_PER_GROUP = 8
N_EXPERTS = N_GROUPS * EXPERTS_PER_GROUP
LN_EPS = 1e-5
RMS_EPS = 1e-6
LANES = 128
VMEM_LIMIT = 56 * 1024 * 1024

NT_DIMS = (((1,), (1,)), ((), ()))
TN_DIMS = (((0,), (0,)), ((), ()))


def _layer_norm(x):
    mu = jnp.mean(x, axis=-1, keepdims=True)
    xc = x - mu
    var = jnp.mean(xc * xc, axis=-1, keepdims=True)
    return xc * lax.rsqrt(var + LN_EPS)


def _silu(x):
    return x * jax.nn.sigmoid(x)


def _ada_kernel(c_ref, w_ref, b_ref, o_ref):
    s = _silu(c_ref[...])
    o_ref[...] = jnp.dot(s, w_ref[...], preferred_element_type=F32, precision=HIGHEST) + b_ref[...]


def _ada(c, w, b):
    bsz, d = c.shape
    n = w.shape[1]
    tn = d
    return pl.pallas_call(
        _ada_kernel,
        grid=(n // tn,),
        in_specs=[pl.BlockSpec((bsz, d), lambda j: (0, 0)),
                  pl.BlockSpec((d, tn), lambda j: (0, j)),
                  pl.BlockSpec((1, tn), lambda j: (0, j))],
        out_specs=pl.BlockSpec((bsz, tn), lambda j: (0, j)),
        out_shape=jax.ShapeDtypeStruct((bsz, n), F32),
        name="ada",
    )(c, w, b)


def _rope_kernel(pos_ref, inv_ref, cs_ref):
    ang = pos_ref[...].astype(F32) * inv_ref[...]
    lane = lax.broadcasted_iota(jnp.int32, ang.shape, 1)
    first_half = (lane % 64) < 32
    sin = jnp.sin(ang)
    cs_ref[:, :LANES] = jnp.cos(ang)
    cs_ref[:, LANES:] = jnp.where(first_half, -sin, sin)


def _rope_tables(pos_col, inv_row, tm):
    t = pos_col.shape[0]
    return pl.pallas_call(
        _rope_kernel,
        grid=(t // tm,),
        in_specs=[pl.BlockSpec((tm, 1), lambda i: (i, 0)),
                  pl.BlockSpec((1, LANES), lambda i: (0, 0))],
        out_specs=pl.BlockSpec((tm, 2 * LANES), lambda i: (i, 0)),
        out_shape=jax.ShapeDtypeStruct((t, 2 * LANES), F32),
        name="rope_tables",
    )(pos_col, inv_row)


def _inproj_kernel(x_ref, mod_ref, cs_ref, w_ref, wgr_ref, wg2_ref, bg2_ref, o_ref, lg_ref, u_ref,
                   *, q_tile, k_tile, q_scale):
    j = pl.program_id(1)

    @pl.when(j == 0)
    def _():
        shift = mod_ref[0, 0:1, :]
        scale = mod_ref[0, 1:2, :]
        u = (_layer_norm(x_ref[...]) * (1.0 + scale) + shift).astype(BF16)
        u_ref[...] = u
        gr = jnp.dot(u, wgr_ref[...], preferred_element_type=F32)
        z = jnp.dot(gr, wg2_ref[...], preferred_element_type=F32, precision=HIGHEST) + bg2_ref[...]
        log_sig = jnp.minimum(z, 0.0) - jnp.log(1.0 + jnp.exp(-jnp.abs(z)))
        lg_ref[...] = log_sig * (1.0 / GLA_TAU)

    acc = jnp.dot(u_ref[...], w_ref[...], preferred_element_type=F32)
    is_rope = jnp.logical_or(j == q_tile, j == k_tile)

    @pl.when(is_rope)
    def _():
        tn = acc.shape[1]
        sc = jnp.where(j == q_tile, q_scale, 1.0).astype(F32)
        cos = cs_ref[:, :LANES] * sc
        sin = cs_ref[:, LANES:] * sc
        lane = lax.broadcasted_iota(jnp.int32, cos.shape, 1)
        first_half = (lane % 64) < 32
        for h in range(tn // LANES):
            t = acc[:, h * LANES:(h + 1) * LANES]
            partner = jnp.where(first_half, pltpu.roll(t, LANES - 32, 1), pltpu.roll(t, 32, 1))
            o_ref[:, h * LANES:(h + 1) * LANES] = (t * cos + partner * sin).astype(o_ref.dtype)

    @pl.when(jnp.logical_not(is_rope))
    def _():
        o_ref[...] = acc.astype(o_ref.dtype)


def _inproj(x2, mod, cs, w_main, w_gr, w_g2, b_g2, *, seq, tm, tn, q_tile, k_tile, q_scale):
    t, d = x2.shape
    n = w_main.shape[1]
    ng = w_g2.shape[1]
    kern = functools.partial(_inproj_kernel, q_tile=q_tile, k_tile=k_tile, q_scale=q_scale)
    return pl.pallas_call(
        kern,
        grid=(t // tm, n // tn),
        in_specs=[pl.BlockSpec((tm, d), lambda i, j: (i, 0)),
                  pl.BlockSpec((1,) + mod.shape[1:], lambda i, j: ((i * tm) // seq, 0, 0)),
                  pl.BlockSpec((tm, 2 * LANES), lambda i, j: (i, 0)),
                  pl.BlockSpec((d, tn), lambda i, j: (0, j)),
                  pl.BlockSpec((d, LANES), lambda i, j: (0, 0)),
                  pl.BlockSpec((LANES, ng), lambda i, j: (0, 0)),
                  pl.BlockSpec((1, ng), lambda i, j: (0, 0))],
        out_specs=[pl.BlockSpec((tm, tn), lambda i, j: (i, j)),
                   pl.BlockSpec((tm, ng), lambda i, j: (i, 0))],
        out_shape=[jax.ShapeDtypeStruct((t, n), BF16),
                   jax.ShapeDtypeStruct((t, ng), F32)],
        scratch_shapes=[pltpu.VMEM((tm, d), BF16)],
        compiler_params=pltpu.CompilerParams(
            dimension_semantics=("arbitrary", "arbitrary"), vmem_limit_bytes=VMEM_LIMIT),
        name="inproj",
    )(x2, mod, cs, w_main, w_gr, w_g2, b_g2)


def _gla_kernel(q_ref, k_ref, v_ref, g_ref, lg_ref, nw_ref, o_ref, st_ref, *, chunk, q_scale):
    @pl.when(pl.program_id(2) == 0)
    def _():
        st_ref[...] = jnp.zeros_like(st_ref)

    rows = q_ref.shape[0]
    r = lax.broadcasted_iota(jnp.int32, (chunk, chunk), 0)
    c = lax.broadcasted_iota(jnp.int32, (chunk, chunk), 1)
    causal = c <= r
    tri = causal.astype(F32)
    nw = nw_ref[...]
    for ci in range(rows // chunk):
        sl = slice(ci * chunk, (ci + 1) * chunk)
        b = jnp.dot(tri, lg_ref[sl, :], preferred_element_type=F32, precision=HIGHEST)
        b_last = b[chunk - 1:chunk, :]
        q = q_ref[sl, :].astype(F32) * q_scale
        k = k_ref[sl, :].astype(F32)
        v = v_ref[sl, :]
        q_t = (q * jnp.exp(b)).astype(BF16)
        k_t = (k * jnp.exp(-b)).astype(BF16)
        k_d = (k * jnp.exp(b_last - b)).astype(BF16)
        attn = lax.dot_general(q_t, k_t, NT_DIMS, preferred_element_type=F32)
        attn = jnp.where(causal, attn, 0.0).astype(BF16)
        st = st_ref[...]
        o = (jnp.dot(attn, v, preferred_element_type=F32)
             + lax.dot_general(q_t, st.astype(BF16), NT_DIMS, preferred_element_type=F32))
        st_ref[...] = st * jnp.exp(b_last) + lax.dot_general(v, k_d, TN_DIMS, preferred_element_type=F32)
        ms = jnp.mean(o * o, axis=-1, keepdims=True)
        y = o * lax.rsqrt(ms + RMS_EPS) * nw * _silu(g_ref[sl, :].astype(F32))
        o_ref[sl, :] = y.astype(o_ref.dtype)


def _gla(proj, log_g, norm_w, *, bsz, seq, rows, dk, dv, q_col, k_col, v_col, g_col):
    t = proj.shape[0]
    nl = seq // rows
    kern = functools.partial(_gla_kernel, chunk=GLA_CHUNK, q_scale=dk ** -0.5)
    row = lambda b, h, l: b * nl + l
    return pl.pallas_call(
        kern,
        grid=(bsz, GLA_HEADS, nl),
        in_specs=[pl.BlockSpec((rows, dk), lambda b, h, l: (row(b, h, l), q_col // dk + h)),
                  pl.BlockSpec((rows, dk), lambda b, h, l: (row(b, h, l), k_col // dk + h)),
                  pl.BlockSpec((rows, dv), lambda b, h, l: (row(b, h, l), v_col // dv + h)),
                  pl.BlockSpec((rows, dv), lambda b, h, l: (row(b, h, l), g_col // dv + h)),
                  pl.BlockSpec((rows, dk), lambda b, h, l: (row(b, h, l), h)),
                  pl.BlockSpec((1, dv), lambda b, h, l: (0, 0))],
        out_specs=pl.BlockSpec((rows, dv), lambda b, h, l: (row(b, h, l), h)),
        out_shape=jax.ShapeDtypeStruct((t, GLA_HEADS * dv), BF16),
        scratch_shapes=[pltpu.VMEM((dv, dk), F32)],
        compiler_params=pltpu.CompilerParams(
            dimension_semantics=("arbitrary", "arbitrary", "arbitrary"), vmem_limit_bytes=VMEM_LIMIT),
        name="gla",
    )(proj, proj, proj, proj, log_g, norm_w)


def _diff_kernel(lam_ref, q_ref, k_ref, v_ref, nw_ref, o_ref, qs_ref, m_ref, l_ref, acc_ref,
                 *, bq, bk, dh, lam_init):
    q0 = pl.program_id(2) * bq
    q = q_ref[...]
    lane = lax.broadcasted_iota(jnp.int32, q.shape, 1)
    zero = jnp.zeros_like(q)
    qs_ref[:bq, :] = jnp.where(lane < dh, q, zero)
    qs_ref[bq:, :] = jnp.where(lane >= dh, q, zero)
    m_ref[...] = jnp.full_like(m_ref, -jnp.inf)
    l_ref[...] = jnp.zeros_like(l_ref)
    acc_ref[...] = jnp.zeros_like(acc_ref)

    def step(kj, masked):
        k0 = pl.multiple_of(kj * bk, bk)
        s = lax.dot_general(qs_ref[...], k_ref[pl.ds(k0, bk), :], NT_DIMS, preferred_element_type=F32)
        if masked:
            row = lax.broadcasted_iota(jnp.int32, s.shape, 0)
            row = q0 + jnp.where(row >= bq, row - bq, row)
            col = k0 + lax.broadcasted_iota(jnp.int32, s.shape, 1)
            s = jnp.where(col <= row, s, -jnp.inf)
        m_prev = m_ref[...]
        m_new = jnp.maximum(m_prev, jnp.max(s, axis=-1, keepdims=True))
        alpha = jnp.exp(m_prev - m_new)
        p = jnp.exp(s - m_new)
        l_ref[...] = alpha * l_ref[...] + jnp.sum(p, axis=-1, keepdims=True)
        acc_ref[...] = alpha * acc_ref[...] + jnp.dot(
            p.astype(BF16), v_ref[pl.ds(k0, bk), :], preferred_element_type=F32)
        m_ref[...] = m_new

    n_full = (q0 + 1) // bk
    n_need = (q0 + bq - 1) // bk + 1

    def full_body(kj, carry):
        step(kj, False)
        return carry

    def edge_body(kj, carry):
        step(kj, True)
        return carry

    lax.fori_loop(0, n_full, full_body, 0)
    lax.fori_loop(n_full, n_need, edge_body, 0)

    lv = lam_ref[...]
    s1 = jnp.sum(lv[0:1, :] * lv[1:2, :], axis=-1, keepdims=True)
    s2 = jnp.sum(lv[2:3, :] * lv[3:4, :], axis=-1, keepdims=True)
    lam = jnp.exp(s1) - jnp.exp(s2) + lam_init
    o_all = acc_ref[...] / l_ref[...]
    o = o_all[:bq, :] - lam * o_all[bq:, :]
    ms = jnp.mean(o * o, axis=-1, keepdims=True)
    y = o * lax.rsqrt(ms + RMS_EPS) * nw_ref[...] * (1.0 - lam_init)
    o_ref[...] = y.astype(o_ref.dtype)


def _diff(lam_vecs, proj, norm_w, *, bsz, seq, bq, bk, dh, q_col, k_col, v_col, lam_init):
    t = proj.shape[0]
    w = 2 * dh
    nq = seq // bq
    kern = functools.partial(_diff_kernel, bq=bq, bk=bk, dh=dh, lam_init=lam_init)
    return pl.pallas_call(
        kern,
        grid=(bsz, DIFF_HEADS, nq),
        in_specs=[pl.BlockSpec(lam_vecs.shape, lambda b, h, i: (0, 0)),
                  pl.BlockSpec((bq, w), lambda b, h, i: (b * nq + i, q_col // w + h)),
                  pl.BlockSpec((seq, w), lambda b, h, i: (b, k_col // w + h)),
                  pl.BlockSpec((seq, w), lambda b, h, i: (b, v_col // w + h)),
                  pl.BlockSpec((1, w), lambda b, h, i: (0, 0))],
        out_specs=pl.BlockSpec((bq, w), lambda b, h, i: (b * nq + i, h)),
        out_shape=jax.ShapeDtypeStruct((t, DIFF_HEADS * w), BF16),
        scratch_shapes=[pltpu.VMEM((2 * bq, w), BF16),
                        pltpu.VMEM((2 * bq, 1), F32),
                        pltpu.VMEM((2 * bq, 1), F32),
                        pltpu.VMEM((2 * bq, w), F32)],
        compiler_params=pltpu.CompilerParams(
            dimension_semantics=("arbitrary", "arbitrary", "arbitrary"), vmem_limit_bytes=VMEM_LIMIT),
        name="diff_attn",
    )(lam_vecs, proj, proj, proj, norm_w)


def _outproj_kernel(og_ref, od_ref, ga_ref, gb_ref, x_ref, mod_ref, wo_ref, lnw_ref, lnb_ref, wr_ref, br_ref,
                    x1_ref, u2_ref, rt_ref, *, alpha):
    g_a = jax.nn.sigmoid(ga_ref[...].astype(F32))
    g_b = jax.nn.sigmoid(gb_ref[...].astype(F32))
    merged = (g_a * og_ref[...].astype(F32) + g_b * od_ref[...].astype(F32)).astype(BF16)
    y = jnp.dot(merged, wo_ref[...], preferred_element_type=F32)
    gate1 = mod_ref[0, 2:3, :]
    shift2 = mod_ref[0, 3:4, :]
    scale2 = mod_ref[0, 4:5, :]
    x1 = _layer_norm(alpha * x_ref[...] + gate1 * y) * lnw_ref[...] + lnb_ref[...]
    x1_ref[...] = x1
    u2 = _layer_norm(x1) * (1.0 + scale2) + shift2
    u2_ref[...] = u2

    logits = jnp.dot(u2, wr_ref[...], preferred_element_type=F32, precision=HIGHEST) + br_ref[...]
    lane = lax.broadcasted_iota(jnp.int32, logits.shape, 1)
    neg = jnp.float32(-jnp.inf)
    big = jnp.int32(LANES)
    lg = jnp.where(lane < N_GROUPS, logits, neg)
    g_max = jnp.max(lg, axis=-1, keepdims=True)
    w_grp = 1.0 / jnp.sum(jnp.exp(lg - g_max), axis=-1, keepdims=True)
    g_idx = jnp.min(jnp.where(lg == g_max, lane, big), axis=-1, keepdims=True)
    lo = N_GROUPS + EXPERTS_PER_GROUP * g_idx
    le = jnp.where(jnp.logical_and(lane >= lo, lane < lo + EXPERTS_PER_GROUP), logits, neg)
    v1 = jnp.max(le, axis=-1, keepdims=True)
    i1 = jnp.min(jnp.where(le == v1, lane, big), axis=-1, keepdims=True)
    le2 = jnp.where(lane == i1, neg, le)
    v2 = jnp.max(le2, axis=-1, keepdims=True)
    i2 = jnp.min(jnp.where(le2 == v2, lane, big), axis=-1, keepdims=True)
    e2 = jnp.exp(v2 - v1)
    den = 1.0 + e2
    c1 = w_grp / den
    c2 = w_grp * e2 / den
    rt = jnp.where(lane == 0, (i1 - N_GROUPS).astype(F32),
                   jnp.where(lane == 1, (i2 - N_GROUPS).astype(F32),
                             jnp.where(lane == 2, c1, jnp.where(lane == 3, c2, 0.0))))
    rt_ref[...] = rt


def _outproj(o_gla, o_diff, proj, x2, mod, w_out, ln_w, ln_b, w_r, b_r, *, seq, tm, ga_col, alpha):
    t, d = x2.shape
    kern = functools.partial(_outproj_kernel, alpha=alpha)
    row_spec = pl.BlockSpec((tm, d), lambda i: (i, 0))
    vec_spec = pl.BlockSpec((1, d), lambda i: (0, 0))
    return pl.pallas_call(
        kern,
        grid=(t // tm,),
        in_specs=[row_spec, row_spec,
                  pl.BlockSpec((tm, d), lambda i: (i, ga_col // d)),
                  pl.BlockSpec((tm, d), lambda i: (i, ga_col // d + 1)),
                  row_spec,
                  pl.BlockSpec((1,) + mod.shape[1:], lambda i: ((i * tm) // seq, 0, 0)),
                  pl.BlockSpec((d, d), lambda i: (0, 0)),
                  vec_spec, vec_spec,
                  pl.BlockSpec((d, LANES), lambda i: (0, 0)),
                  pl.BlockSpec((1, LANES), lambda i: (0, 0))],
        out_specs=[row_spec, row_spec, pl.BlockSpec((tm, LANES), lambda i: (i, 0))],
        out_shape=[jax.ShapeDtypeStruct((t, d), F32),
                   jax.ShapeDtypeStruct((t, d), F32),
                   jax.ShapeDtypeStruct((t, LANES), F32)],
        compiler_params=pltpu.CompilerParams(
            dimension_semantics=("arbitrary",), vmem_limit_bytes=VMEM_LIMIT),
        name="outproj",
    )(o_gla, o_diff, proj, proj, x2, mod, w_out, ln_w, ln_b, w_r, b_r)


def _row_copy(src_hbm, dst_vmem, sem, src_row, dst_row):
    return pltpu.make_async_copy(src_hbm.at[pl.ds(src_row, 1), :], dst_vmem.at[pl.ds(dst_row, 1), :], sem)


def _expert_kernel(be_ref, tok_ref, tokn_ref, u_hbm, wg_ref, wu_ref, wd_ref, o_ref,
                   xbuf, wgb, wub, wdb, sem, *, blk):
    i = pl.program_id(0)
    n = pl.num_programs(0)
    slot = i % 2

    def start_gather(idx_ref, s):
        def body(r, carry):
            _row_copy(u_hbm, xbuf.at[s], sem.at[s], idx_ref[0, 0, r], r).start()
            return carry
        lax.fori_loop(0, blk, body, 0)

    def wait_gather(s):
        def body(r, carry):
            _row_copy(u_hbm, xbuf.at[s], sem.at[s], 0, r).wait()
            return carry
        lax.fori_loop(0, blk, body, 0)

    @pl.when(i == 0)
    def _():
        start_gather(tok_ref, 0)

    @pl.when(i + 1 < n)
    def _():
        start_gather(tokn_ref, 1 - slot)

    changed = jnp.logical_or(i == 0, be_ref[i] != be_ref[jnp.maximum(i - 1, 0)])

    @pl.when(changed)
    def _():
        wgb[...] = wg_ref[0].astype(BF16)
        wub[...] = wu_ref[0].astype(BF16)
        wdb[...] = wd_ref[0].astype(BF16)

    wait_gather(slot)
    xb = xbuf[slot].astype(BF16)
    hg = jnp.dot(xb, wgb[...], preferred_element_type=F32)
    hu = jnp.dot(xb, wub[...], preferred_element_type=F32)
    h = (_silu(hg) * hu).astype(BF16)
    o_ref[...] = jnp.dot(h, wdb[...], preferred_element_type=F32)


def _experts(block_expert, row_token3, u2, w_gate, w_up, w_down, *, blk):
    nblk = row_token3.shape[0]
    t, d = u2.shape
    e, _, ff = w_gate.shape
    kern = functools.partial(_expert_kernel, blk=blk)
    grid_spec = pltpu.PrefetchScalarGridSpec(
        num_scalar_prefetch=1,
        grid=(nblk,),
        in_specs=[pl.BlockSpec((1, 1, blk), lambda i, be: (i, 0, 0), memory_space=pltpu.SMEM),
                  pl.BlockSpec((1, 1, blk), lambda i, be: (jnp.minimum(i + 1, nblk - 1), 0, 0),
                               memory_space=pltpu.SMEM),
                  pl.BlockSpec(memory_space=pl.ANY),
                  pl.BlockSpec((1, d, ff), lambda i, be: (be[i], 0, 0)),
                  pl.BlockSpec((1, d, ff), lambda i, be: (be[i], 0, 0)),
                  pl.BlockSpec((1, ff, d), lambda i, be: (be[i], 0, 0))],
        out_specs=pl.BlockSpec((blk, d), lambda i, be: (i, 0)),
        scratch_shapes=[pltpu.VMEM((2, blk, d), F32),
                        pltpu.VMEM((d, ff), BF16),
                        pltpu.VMEM((d, ff), BF16),
                        pltpu.VMEM((ff, d), BF16),
                        pltpu.SemaphoreType.DMA((2,))],
    )
    return pl.pallas_call(
        kern,
        grid_spec=grid_spec,
        out_shape=jax.ShapeDtypeStruct((nblk * blk, d), F32),
        compiler_params=pltpu.CompilerParams(
            dimension_semantics=("arbitrary",), vmem_limit_bytes=VMEM_LIMIT),
        name="experts",
    )(block_expert, row_token3, row_token3, u2, w_gate, w_up, w_down)


def _combine_kernel(d0_ref, d0n_ref, d1_ref, d1n_ref, rows_hbm, x1_ref, rt_ref, mod_ref, lnw_ref, lnb_ref,
                    o_ref, buf, sem, *, tm, alpha):
    i = pl.program_id(0)
    n = pl.num_programs(0)
    slot = i % 2

    def start_gather(i0_ref, i1_ref, s):
        def body(r, carry):
            _row_copy(rows_hbm, buf.at[s, 0], sem.at[s], i0_ref[0, 0, r], r).start()
            _row_copy(rows_hbm, buf.at[s, 1], sem.at[s], i1_ref[0, 0, r], r).start()
            return carry
        lax.fori_loop(0, tm, body, 0)

    def wait_gather(s):
        def body(r, carry):
            _row_copy(rows_hbm, buf.at[s, 0], sem.at[s], 0, r).wait()
            _row_copy(rows_hbm, buf.at[s, 1], sem.at[s], 0, r).wait()
            return carry
        lax.fori_loop(0, tm, body, 0)

    @pl.when(i == 0)
    def _():
        start_gather(d0_ref, d1_ref, 0)

    @pl.when(i + 1 < n)
    def _():
        start_gather(d0n_ref, d1n_ref, 1 - slot)

    wait_gather(slot)
    rt = rt_ref[...]
    y = rt[:, 2:3] * buf[slot, 0] + rt[:, 3:4] * buf[slot, 1]
    gate2 = mod_ref[0, 5:6, :]
    z = alpha * x1_ref[...] + gate2 * y
    o_ref[...] = _layer_norm(z) * lnw_ref[...] + lnb_ref[...]


def _combine(dest0, dest1, rows, x1, route, mod, ln_w, ln_b, *, seq, tm, alpha):
    t, d = x1.shape
    nt = t // tm
    kern = functools.partial(_combine_kernel, tm=tm, alpha=alpha)
    cur = lambda i: (i, 0, 0)
    nxt = lambda i: (jnp.minimum(i + 1, nt - 1), 0, 0)
    idx_spec = lambda m: pl.BlockSpec((1, 1, tm), m, memory_space=pltpu.SMEM)
    row_spec = pl.BlockSpec((tm, d), lambda i: (i, 0))
    vec_spec = pl.BlockSpec((1, d), lambda i: (0, 0))
    return pl.pallas_call(
        kern,
        grid=(nt,),
        in_specs=[idx_spec(cur), idx_spec(nxt), idx_spec(cur), idx_spec(nxt),
                  pl.BlockSpec(memory_space=pl.ANY),
                  row_spec,
                  pl.BlockSpec((tm, LANES), lambda i: (i, 0)),
                  pl.BlockSpec((1,) + mod.shape[1:], lambda i: ((i * tm) // seq, 0, 0)),
                  vec_spec, vec_spec],
        out_specs=row_spec,
        out_shape=jax.ShapeDtypeStruct((t, d), F32),
        scratch_shapes=[pltpu.VMEM((2, 2, tm, d), F32),
                        pltpu.SemaphoreType.DMA((2,))],
        compiler_params=pltpu.CompilerParams(
            dimension_semantics=("arbitrary",), vmem_limit_bytes=VMEM_LIMIT),
        name="combine",
    )(dest0, dest0, dest1, dest1, rows, x1, route, mod, ln_w, ln_b)


def _dispatch_plan(route, blk):
    t = route.shape[0]
    eid = route[:, :2].astype(jnp.int32)
    flat_e = eid.reshape(-1)
    tk = flat_e.shape[0]
    onehot = (flat_e[:, None] == jnp.arange(N_EXPERTS, dtype=jnp.int32)[None, :]).astype(jnp.int32)
    csum = jnp.cumsum(onehot, axis=0)
    pos = jnp.sum(csum * onehot, axis=1) - 1
    counts = csum[-1]
    pcounts = ((counts + blk - 1) // blk) * blk
    pend = jnp.cumsum(pcounts)
    pstart = pend - pcounts
    dest = pstart[flat_e] + pos
    p_rows = ((tk + N_EXPERTS * (blk - 1) + blk - 1) // blk) * blk
    nblk = p_rows // blk
    flat_t = jnp.repeat(jnp.arange(t, dtype=jnp.int32), 2)
    row_token = jnp.zeros((p_rows,), jnp.int32).at[dest].set(flat_t)
    block_expert = jnp.clip(jnp.searchsorted(pend, jnp.arange(nblk, dtype=jnp.int32) * blk, side="right"),
                            0, N_EXPERTS - 1).astype(jnp.int32)
    dest2 = dest.reshape(t, 2)
    return row_token, block_expert, dest2[:, 0], dest2[:, 1]


def _layer(x, c, positions, w_ada, b_ada, w_in, w_g2, b_g2, gla_nw, lq1, lk1, lq2, lk2, diff_nw, w_out,
           ln1_w, ln1_b, w_rg, b_rg, w_re, b_re, w_eg, w_eu, w_ed, ln2_w, ln2_b, *, lam_init,
           tm_in, tn_in, gla_rows, bq, bk, tm_out, moe_blk, tm_cmb):
    bsz, seq, d = x.shape
    t = bsz * seq
    alpha = (2.0 * DEPTH) ** 0.25
    gla_dk = d // (2 * GLA_HEADS)
    gla_dv = d // GLA_HEADS
    dh = d // (2 * DIFF_HEADS)
    hk = GLA_HEADS * gla_dk
    hv = GLA_HEADS * gla_dv
    dq = DIFF_HEADS * 2 * dh
    gr_col = 2 * hk + 2 * hv
    q_col, k_col = 0, hk
    v_col, g_col = 2 * hk, 2 * hk + hv
    dq_col = gr_col
    dk_col = dq_col + dq
    dv_col = dk_col + dq
    mg_col = dv_col + dq
    w_main = jnp.concatenate([w_in[:, :gr_col], w_in[:, gr_col + GLA_GATE_RANK:]], axis=1).astype(BF16)
    w_gr = jnp.pad(w_in[:, gr_col:gr_col + GLA_GATE_RANK], ((0, 0), (0, LANES - GLA_GATE_RANK))).astype(BF16)
    w_g2p = jnp.pad(w_g2, ((0, LANES - GLA_GATE_RANK), (0, 0)))
    x2 = x.reshape(t, d)

    ada = _ada(c, w_ada, b_ada.reshape(1, -1))
    mod = ada.reshape(bsz, 6, d)

    half = dh // 2
    inv = ROPE_THETA ** (-jnp.arange(half, dtype=F32) / half)
    inv_row = jnp.tile(inv, LANES // half).reshape(1, LANES)
    cs = _rope_tables(positions.reshape(t, 1), inv_row, tm_in)

    proj, log_g = _inproj(x2, mod, cs, w_main, w_gr, w_g2p, b_g2.reshape(1, -1), seq=seq, tm=tm_in, tn=tn_in,
                          q_tile=dq_col // tn_in, k_tile=dk_col // tn_in, q_scale=dh ** -0.5)

    o_gla = _gla(proj, log_g, gla_nw.reshape(1, -1), bsz=bsz, seq=seq, rows=gla_rows, dk=gla_dk, dv=gla_dv,
                 q_col=q_col, k_col=k_col, v_col=v_col, g_col=g_col)

    lam_vecs = jnp.pad(jnp.stack([lq1, lk1, lq2, lk2]), ((0, 4), (0, LANES - dh)))
    o_diff = _diff(lam_vecs, proj, diff_nw.reshape(1, -1), bsz=bsz, seq=seq, bq=bq, bk=bk, dh=dh,
                   q_col=dq_col, k_col=dk_col, v_col=dv_col, lam_init=lam_init)

    w_r = jnp.pad(jnp.concatenate([w_rg, w_re], axis=1), ((0, 0), (0, LANES - N_GROUPS - N_EXPERTS)))
    b_r = jnp.pad(jnp.concatenate([b_rg, b_re]), (0, LANES - N_GROUPS - N_EXPERTS)).reshape(1, LANES)
    x1, u2, route = _outproj(o_gla, o_diff, proj, x2, mod, w_out.astype(BF16), ln1_w.reshape(1, -1),
                             ln1_b.reshape(1, -1), w_r, b_r, seq=seq, tm=tm_out, ga_col=mg_col, alpha=alpha)

    row_token, block_expert, dest0, dest1 = _dispatch_plan(route, moe_blk)
    rows = _experts(block_expert, row_token.reshape(-1, 1, moe_blk), u2, w_eg, w_eu, w_ed, blk=moe_blk)
    out = _combine(dest0.reshape(-1, 1, tm_cmb), dest1.reshape(-1, 1, tm_cmb), rows, x1, route, mod,
                   ln2_w.reshape(1, -1), ln2_b.reshape(1, -1), seq=seq, tm=tm_cmb, alpha=alpha)
    return out.reshape(bsz, seq, d)


def kernel(x, c, positions, w_ada, b_ada, w_in, w_gla_gate2, b_gla_gate2, gla_norm_w, diff_lambda_q1,
           diff_lambda_k1, diff_lambda_q2, diff_lambda_k2, diff_norm_w, w_out, ln1_w, ln1_b, w_router_group,
           b_router_group, w_router_expert, b_router_expert, w_exp_gate, w_exp_up, w_exp_down, ln2_w, ln2_b):
    assert w_ada.shape[0] == DEPTH
    for l in range(DEPTH):
        lam_init = 0.8 - 0.6 * math.exp(-0.3 * l)
        x = _layer(x, c, positions, w_ada[l], b_ada[l], w_in[l], w_gla_gate2[l], b_gla_gate2[l], gla_norm_w[l],
                   diff_lambda_q1[l], diff_lambda_k1[l], diff_lambda_q2[l], diff_lambda_k2[l], diff_norm_w[l],
                   w_out[l], ln1_w[l], ln1_b[l], w_router_group[l], b_router_group[l], w_router_expert[l],
                   b_router_expert[l], w_exp_gate[l], w_exp_up[l], w_exp_down[l], ln2_w[l], ln2_b[l],
                   lam_init=lam_init, tm_in=1024, tn_in=1024, gla_rows=512, bq=256, bk=512, tm_out=512,
                   moe_blk=128, tm_cmb=256)
    return x
```

```python
import functools
import math

import jax
import jax.numpy as jnp
from jax import lax
from jax.experimental import pallas as pl
from jax.experimental.pallas import tpu as pltpu

F32 = jnp.float32
BF16 = jnp.bfloat16
HIGHEST = lax.Precision.HIGHEST

DEPTH = 1
GLA_HEADS = 4
GLA_GATE_RANK = 16
GLA_TAU = 16.0
GLA_CHUNK = 64
DIFF_HEADS = 8
ROPE_THETA = 10000.0
N_GROUPS = 4
EXPERTS_PER_GROUP = 8
N_EXPERTS = N_GROUPS * EXPERTS_PER_GROUP
LN_EPS = 1e-5
RMS_EPS = 1e-6
LOG2_E = math.log2(math.e)
LANES = 128
VMEM_LIMIT = 56 * 1024 * 1024

NT_DIMS = (((1,), (1,)), ((), ()))
TN_DIMS = (((0,), (0,)), ((), ()))


def _layer_norm(x):
    mu = jnp.mean(x, axis=-1, keepdims=True)
    xc = x - mu
    var = jnp.mean(xc * xc, axis=-1, keepdims=True)
    return xc * lax.rsqrt(var + LN_EPS)


def _silu(x):
    return x * jax.nn.sigmoid(x)


def _ada_kernel(c_ref, w_ref, b_ref, o_ref):
    s = _silu(c_ref[...])
    o_ref[...] = jnp.dot(s, w_ref[...], preferred_element_type=F32, precision=HIGHEST) + b_ref[...]


def _ada(c, w, b):
    bsz, d = c.shape
    n = w.shape[1]
    tn = d
    return pl.pallas_call(
        _ada_kernel,
        grid=(n // tn,),
        in_specs=[pl.BlockSpec((bsz, d), lambda j: (0, 0)),
                  pl.BlockSpec((d, tn), lambda j: (0, j)),
                  pl.BlockSpec((1, tn), lambda j: (0, j))],
        out_specs=pl.BlockSpec((bsz, tn), lambda j: (0, j)),
        out_shape=jax.ShapeDtypeStruct((bsz, n), F32),
        name="ada",
    )(c, w, b)


def _rope_kernel(pos_ref, inv_ref, cs_ref):
    ang = pos_ref[...].astype(F32) * inv_ref[...]
    lane = lax.broadcasted_iota(jnp.int32, ang.shape, 1)
    first_half = (lane % 64) < 32
    sin = jnp.sin(ang)
    cs_ref[:, :LANES] = jnp.cos(ang)
    cs_ref[:, LANES:] = jnp.where(first_half, -sin, sin)


def _rope_tables(pos_col, inv_row, tm):
    t = pos_col.shape[0]
    return pl.pallas_call(
        _rope_kernel,
        grid=(t // tm,),
        in_specs=[pl.BlockSpec((tm, 1), lambda i: (i, 0)),
                  pl.BlockSpec((1, LANES), lambda i: (0, 0))],
        out_specs=pl.BlockSpec((tm, 2 * LANES), lambda i: (i, 0)),
        out_shape=jax.ShapeDtypeStruct((t, 2 * LANES), F32),
        name="rope_tables",
    )(pos_col, inv_row)


def _inproj_kernel(x_ref, mod_ref, cs_ref, w_ref, wgr_ref, wg2_ref, bg2_ref, o_ref, lg_ref, u_ref,
                   *, q_tile, k_tile, q_scale):
    j = pl.program_id(1)

    @pl.when(j == 0)
    def _():
        shift = mod_ref[0, 0:1, :]
        scale = mod_ref[0, 1:2, :]
        u = (_layer_norm(x_ref[...]) * (1.0 + scale) + shift).astype(BF16)
        u_ref[...] = u
        gr = jnp.dot(u, wgr_ref[...], preferred_element_type=F32)
        z = jnp.dot(gr, wg2_ref[...], preferred_element_type=F32, precision=HIGHEST) + bg2_ref[...]
        log_sig = jnp.minimum(z, 0.0) - jnp.log(1.0 + jnp.exp(-jnp.abs(z)))
        lg_ref[...] = log_sig * (1.0 / GLA_TAU)

    acc = jnp.dot(u_ref[...], w_ref[...], preferred_element_type=F32)
    is_rope = jnp.logical_or(j == q_tile, j == k_tile)

    @pl.when(is_rope)
    def _():
        tn = acc.shape[1]
        sc = jnp.where(j == q_tile, q_scale, 1.0).astype(F32)
        cos = cs_ref[:, :LANES] * sc
        sin = cs_ref[:, LANES:] * sc
        lane = lax.broadcasted_iota(jnp.int32, cos.shape, 1)
        first_half = (lane % 64) < 32
        for h in range(tn // LANES):
            t = acc[:, h * LANES:(h + 1) * LANES]
            partner = jnp.where(first_half, pltpu.roll(t, LANES - 32, 1), pltpu.roll(t, 32, 1))
            o_ref[:, h * LANES:(h + 1) * LANES] = (t * cos + partner * sin).astype(o_ref.dtype)

    @pl.when(jnp.logical_not(is_rope))
    def _():
        o_ref[...] = acc.astype(o_ref.dtype)


def _inproj(x2, mod, cs, w_main, w_gr, w_g2, b_g2, *, seq, tm, tn, q_tile, k_tile, q_scale):
    t, d = x2.shape
    n = w_main.shape[1]
    ng = w_g2.shape[1]
    kern = functools.partial(_inproj_kernel, q_tile=q_tile, k_tile=k_tile, q_scale=q_scale)
    return pl.pallas_call(
        kern,
        grid=(t // tm, n // tn),
        in_specs=[pl.BlockSpec((tm, d), lambda i, j: (i, 0)),
                  pl.BlockSpec((1,) + mod.shape[1:], lambda i, j: ((i * tm) // seq, 0, 0)),
                  pl.BlockSpec((tm, 2 * LANES), lambda i, j: (i, 0)),
                  pl.BlockSpec((d, tn), lambda i, j: (0, j)),
                  pl.BlockSpec((d, LANES), lambda i, j: (0, 0)),
                  pl.BlockSpec((LANES, ng), lambda i, j: (0, 0)),
                  pl.BlockSpec((1, ng), lambda i, j: (0, 0))],
        out_specs=[pl.BlockSpec((tm, tn), lambda i, j: (i, j)),
                   pl.BlockSpec((tm, ng), lambda i, j: (i, 0))],
        out_shape=[jax.ShapeDtypeStruct((t, n), BF16),
                   jax.ShapeDtypeStruct((t, ng), F32)],
        scratch_shapes=[pltpu.VMEM((tm, d), BF16)],
        compiler_params=pltpu.CompilerParams(
            dimension_semantics=("arbitrary", "arbitrary"), vmem_limit_bytes=VMEM_LIMIT),
        name="inproj",
    )(x2, mod, cs, w_main, w_gr, w_g2, b_g2)


def _gla_kernel(q_ref, k_ref, v_ref, g_ref, lg_ref, nw_ref, o_ref, st_ref, *, chunk, q_scale):
    @pl.when(pl.program_id(2) == 0)
    def _():
        st_ref[...] = jnp.zeros_like(st_ref)

    rows = q_ref.shape[0]
    r = lax.broadcasted_iota(jnp.int32, (chunk, chunk), 0)
    c = lax.broadcasted_iota(jnp.int32, (chunk, chunk), 1)
    causal = c <= r
    tri = causal.astype(F32)
    nw = nw_ref[...]
    for ci in range(rows // chunk):
        sl = slice(ci * chunk, (ci + 1) * chunk)
        b = jnp.dot(tri, lg_ref[sl, :], preferred_element_type=F32, precision=HIGHEST)
        b_last = b[chunk - 1:chunk, :]
        q = q_ref[sl, :].astype(F32) * q_scale
        k = k_ref[sl, :].astype(F32)
        v = v_ref[sl, :]
        q_t = (q * jnp.exp(b)).astype(BF16)
        k_t = (k * jnp.exp(-b)).astype(BF16)
        k_d = (k * jnp.exp(b_last - b)).astype(BF16)
        attn = lax.dot_general(q_t, k_t, NT_DIMS, preferred_element_type=F32)
        attn = jnp.where(causal, attn, 0.0).astype(BF16)
        st = st_ref[...]
        o = (jnp.dot(attn, v, preferred_element_type=F32)
             + lax.dot_general(q_t, st.astype(BF16), NT_DIMS, preferred_element_type=F32))
        st_ref[...] = st * jnp.exp(b_last) + lax.dot_general(v, k_d, TN_DIMS, preferred_element_type=F32)
        ms = jnp.mean(o * o, axis=-1, keepdims=True)
        y = o * lax.rsqrt(ms + RMS_EPS) * nw * _silu(g_ref[sl, :].astype(F32))
        o_ref[sl, :] = y.astype(o_ref.dtype)


def _gla(proj, log_g, norm_w, *, bsz, seq, rows, dk, dv, q_col, k_col, v_col, g_col):
    t = proj.shape[0]
    nl = seq // rows
    kern = functools.partial(_gla_kernel, chunk=GLA_CHUNK, q_scale=dk ** -0.5)
    row = lambda b, h, l: b * nl + l
    return pl.pallas_call(
        kern,
        grid=(bsz, GLA_HEADS, nl),
        in_specs=[pl.BlockSpec((rows, dk), lambda b, h, l: (row(b, h, l), q_col // dk + h)),
                  pl.BlockSpec((rows, dk), lambda b, h, l: (row(b, h, l), k_col // dk + h)),
                  pl.BlockSpec((rows, dv), lambda b, h, l: (row(b, h, l), v_col // dv + h)),
                  pl.BlockSpec((rows, dv), lambda b, h, l: (row(b, h, l), g_col // dv + h)),
                  pl.BlockSpec((rows, dk), lambda b, h, l: (row(b, h, l), h)),
                  pl.BlockSpec((1, dv), lambda b, h, l: (0, 0))],
        out_specs=pl.BlockSpec((rows, dv), lambda b, h, l: (row(b, h, l), h)),
        out_shape=jax.ShapeDtypeStruct((t, GLA_HEADS * dv), BF16),
        scratch_shapes=[pltpu.VMEM((dv, dk), F32)],
        compiler_params=pltpu.CompilerParams(
            dimension_semantics=("arbitrary", "arbitrary", "arbitrary"), vmem_limit_bytes=VMEM_LIMIT),
        name="gla",
    )(proj, proj, proj, proj, log_g, norm_w)


def _diff_kernel(lam_ref, q_ref, k_ref, v_ref, nw_ref, o_ref, qs_ref, vt_ref, sa_ref, sb_ref, m_ref, l_ref, acc_ref,
                 *, bq, bk, dh, lam_init):
    qi = pl.program_id(2)
    q0 = qi * bq
    seq = k_ref.shape[0]

    @pl.when(qi == 0)
    def _():
        for c in range(seq // bk):
            vt_ref[:, c * bk:(c + 1) * bk] = v_ref[c * bk:(c + 1) * bk, :].astype(F32).T.astype(BF16)

    q = q_ref[...]
    lane = lax.broadcasted_iota(jnp.int32, q.shape, 1)
    zero = jnp.zeros_like(q)
    qs_ref[:bq, :] = jnp.where(lane < dh, q, zero)
    qs_ref[bq:, :] = jnp.where(lane >= dh, q, zero)
    m_ref[...] = jnp.full_like(m_ref, -jnp.inf)
    l_ref[...] = jnp.zeros_like(l_ref)
    acc_ref[...] = jnp.zeros_like(acc_ref)

    def scores(kj, dst):
        k0 = pl.multiple_of(kj * bk, bk)
        dst[...] = lax.dot_general(k_ref[pl.ds(k0, bk), :], qs_ref[...], NT_DIMS, preferred_element_type=F32)

    def update(src, kj, masked):
        k0 = pl.multiple_of(kj * bk, bk)
        s = src[...]
        if masked:
            key = k0 + lax.broadcasted_iota(jnp.int32, s.shape, 0)
            col = lax.broadcasted_iota(jnp.int32, s.shape, 1)
            qpos = q0 + jnp.where(col >= bq, col - bq, col)
            s = jnp.where(key <= qpos, s, -jnp.inf)
        m_prev = m_ref[...]
        m_new = jnp.maximum(m_prev, jnp.max(s, axis=0, keepdims=True))
        alpha = jnp.exp2(m_prev - m_new)
        p = jnp.exp2(s - m_new)
        l_ref[...] = alpha * l_ref[...] + jnp.sum(p, axis=0, keepdims=True)
        acc_ref[...] = alpha * acc_ref[...] + jnp.dot(
            vt_ref[:, pl.ds(k0, bk)], p.astype(BF16), preferred_element_type=F32)
        m_ref[...] = m_new

    n_full = (q0 + 1) // bk
    scores(0, sa_ref)

    def pair_body(jj, carry):
        j = 2 * jj
        scores(j + 1, sb_ref)
        update(sa_ref, j, False)
        scores(j + 2, sa_ref)
        update(sb_ref, j + 1, False)
        return carry

    lax.fori_loop(0, n_full // 2, pair_body, 0)
    odd = (n_full % 2) == 1

    @pl.when(odd)
    def _():
        scores(n_full, sb_ref)
        update(sa_ref, n_full - 1, False)
        update(sb_ref, n_full, True)

    @pl.when(jnp.logical_not(odd))
    def _():
        update(sa_ref, n_full, True)

    lv = lam_ref[...]
    s1 = jnp.sum(lv[0:1, :] * lv[1:2, :], axis=-1, keepdims=True)
    s2 = jnp.sum(lv[2:3, :] * lv[3:4, :], axis=-1, keepdims=True)
    lam = jnp.exp(s1) - jnp.exp(s2) + lam_init
    o_all = acc_ref[...] / l_ref[...]
    o_t = o_all[:, :bq] - lam * o_all[:, bq:]
    ms = jnp.mean(o_t * o_t, axis=0, keepdims=True)
    y = (o_t * lax.rsqrt(ms + RMS_EPS)).T * nw_ref[...] * (1.0 - lam_init)
    o_ref[...] = y.astype(o_ref.dtype)


def _diff(lam_vecs, proj, norm_w, *, bsz, seq, bq, bk, dh, q_col, k_col, v_col, lam_init):
    assert bk % bq == 0 and seq % bk == 0
    t = proj.shape[0]
    w = 2 * dh
    nq = seq // bq
    kern = functools.partial(_diff_kernel, bq=bq, bk=bk, dh=dh, lam_init=lam_init)
    return pl.pallas_call(
        kern,
        grid=(bsz, DIFF_HEADS, nq),
        in_specs=[pl.BlockSpec(lam_vecs.shape, lambda b, h, i: (0, 0)),
                  pl.BlockSpec((bq, w), lambda b, h, i: (b * nq + i, q_col // w + h)),
                  pl.BlockSpec((seq, w), lambda b, h, i: (b, k_col // w + h)),
                  pl.BlockSpec((seq, w), lambda b, h, i: (b, v_col // w + h)),
                  pl.BlockSpec((1, w), lambda b, h, i: (0, 0))],
        out_specs=pl.BlockSpec((bq, w), lambda b, h, i: (b * nq + i, h)),
        out_shape=jax.ShapeDtypeStruct((t, DIFF_HEADS * w), BF16),
        scratch_shapes=[pltpu.VMEM((2 * bq, w), BF16),
                        pltpu.VMEM((w, seq), BF16),
                        pltpu.VMEM((bk, 2 * bq), F32),
                        pltpu.VMEM((bk, 2 * bq), F32),
                        pltpu.VMEM((1, 2 * bq), F32),
                        pltpu.VMEM((1, 2 * bq), F32),
                        pltpu.VMEM((w, 2 * bq), F32)],
        compiler_params=pltpu.CompilerParams(
            dimension_semantics=("arbitrary", "arbitrary", "arbitrary"), vmem_limit_bytes=VMEM_LIMIT),
        name="diff_attn",
    )(lam_vecs, proj, proj, proj, norm_w)


def _outproj_kernel(og_ref, od_ref, ga_ref, gb_ref, x_ref, mod_ref, wo_ref, lnw_ref, lnb_ref, wr_ref, br_ref,
                    x1_ref, u2_ref, rt_ref, *, alpha):
    g_a = jax.nn.sigmoid(ga_ref[...].astype(F32))
    g_b = jax.nn.sigmoid(gb_ref[...].astype(F32))
    merged = (g_a * og_ref[...].astype(F32) + g_b * od_ref[...].astype(F32)).astype(BF16)
    y = jnp.dot(merged, wo_ref[...], preferred_element_type=F32)
    gate1 = mod_ref[0, 2:3, :]
    shift2 = mod_ref[0, 3:4, :]
    scale2 = mod_ref[0, 4:5, :]
    x1 = _layer_norm(alpha * x_ref[...] + gate1 * y) * lnw_ref[...] + lnb_ref[...]
    x1_ref[...] = x1
    u2 = _layer_norm(x1) * (1.0 + scale2) + shift2
    u2_ref[...] = u2

    logits = jnp.dot(u2, wr_ref[...], preferred_element_type=F32, precision=HIGHEST) + br_ref[...]
    lane = lax.broadcasted_iota(jnp.int32, logits.shape, 1)
    neg = jnp.float32(-jnp.inf)
    big = jnp.int32(LANES)
    lg = jnp.where(lane < N_GROUPS, logits, neg)
    g_max = jnp.max(lg, axis=-1, keepdims=True)
    w_grp = 1.0 / jnp.sum(jnp.exp(lg - g_max), axis=-1, keepdims=True)
    g_idx = jnp.min(jnp.where(lg == g_max, lane, big), axis=-1, keepdims=True)
    lo = N_GROUPS + EXPERTS_PER_GROUP * g_idx
    le = jnp.where(jnp.logical_and(lane >= lo, lane < lo + EXPERTS_PER_GROUP), logits, neg)
    v1 = jnp.max(le, axis=-1, keepdims=True)
    i1 = jnp.min(jnp.where(le == v1, lane, big), axis=-1, keepdims=True)
    le2 = jnp.where(lane == i1, neg, le)
    v2 = jnp.max(le2, axis=-1, keepdims=True)
    i2 = jnp.min(jnp.where(le2 == v2, lane, big), axis=-1, keepdims=True)
    e2 = jnp.exp(v2 - v1)
    den = 1.0 + e2
    c1 = w_grp / den
    c2 = w_grp * e2 / den
    rt = jnp.where(lane == 0, (i1 - N_GROUPS).astype(F32),
                   jnp.where(lane == 1, (i2 - N_GROUPS).astype(F32),
                             jnp.where(lane == 2, c1, jnp.where(lane == 3, c2, 0.0))))
    rt_ref[...] = rt


def _outproj(o_gla, o_diff, proj, x2, mod, w_out, ln_w, ln_b, w_r, b_r, *, seq, tm, ga_col, alpha):
    t, d = x2.shape
    kern = functools.partial(_outproj_kernel, alpha=alpha)
    row_spec = pl.BlockSpec((tm, d), lambda i: (i, 0))
    vec_spec = pl.BlockSpec((1, d), lambda i: (0, 0))
    return pl.pallas_call(
        kern,
        grid=(t // tm,),
        in_specs=[row_spec, row_spec,
                  pl.BlockSpec((tm, d), lambda i: (i, ga_col // d)),
                  pl.BlockSpec((tm, d), lambda i: (i, ga_col // d + 1)),
                  row_spec,
                  pl.BlockSpec((1,) + mod.shape[1:], lambda i: ((i * tm) // seq, 0, 0)),
                  pl.BlockSpec((d, d), lambda i: (0, 0)),
                  vec_spec, vec_spec,
                  pl.BlockSpec((d, LANES), lambda i: (0, 0)),
                  pl.BlockSpec((1, LANES), lambda i: (0, 0))],
        out_specs=[row_spec, row_spec, pl.BlockSpec((tm, LANES), lambda i: (i, 0))],
        out_shape=[jax.ShapeDtypeStruct((t, d), F32),
                   jax.ShapeDtypeStruct((t, d), F32),
                   jax.ShapeDtypeStruct((t, LANES), F32)],
        compiler_params=pltpu.CompilerParams(
            dimension_semantics=("arbitrary",), vmem_limit_bytes=VMEM_LIMIT),
        name="outproj",
    )(o_gla, o_diff, proj, proj, x2, mod, w_out, ln_w, ln_b, w_r, b_r)


def _row_copy(src_hbm, dst_vmem, sem, src_row, dst_row):
    return pltpu.make_async_copy(src_hbm.at[pl.ds(src_row, 1), :], dst_vmem.at[pl.ds(dst_row, 1), :], sem)


def _expert_kernel(be_ref, tok_ref, tokn_ref, u_hbm, wg_ref, wu_ref, wd_ref, o_ref,
                   xbuf, wgb, wub, wdb, sem, *, blk):
    i = pl.program_id(0)
    n = pl.num_programs(0)
    slot = i % 2

    def start_gather(idx_ref, s):
        def body(r, carry):
            _row_copy(u_hbm, xbuf.at[s], sem.at[s], idx_ref[0, 0, r], r).start()
            return carry
        lax.fori_loop(0, blk, body, 0)

    def wait_gather(s):
        def body(r, carry):
            _row_copy(u_hbm, xbuf.at[s], sem.at[s], 0, r).wait()
            return carry
        lax.fori_loop(0, blk, body, 0)

    @pl.when(i == 0)
    def _():
        start_gather(tok_ref, 0)

    @pl.when(i + 1 < n)
    def _():
        start_gather(tokn_ref, 1 - slot)

    changed = jnp.logical_or(i == 0, be_ref[i] != be_ref[jnp.maximum(i - 1, 0)])

    @pl.when(changed)
    def _():
        wgb[...] = wg_ref[0].astype(BF16)
        wub[...] = wu_ref[0].astype(BF16)
        wdb[...] = wd_ref[0].astype(BF16)

    wait_gather(slot)
    xb = xbuf[slot].astype(BF16)
    hg = jnp.dot(xb, wgb[...], preferred_element_type=F32)
    hu = jnp.dot(xb, wub[...], preferred_element_type=F32)
    h = (_silu(hg) * hu).astype(BF16)
    o_ref[...] = jnp.dot(h, wdb[...], preferred_element_type=F32)


def _experts(block_expert, row_token3, u2, w_gate, w_up, w_down, *, blk):
    nblk = row_token3.shape[0]
    t, d = u2.shape
    e, _, ff = w_gate.shape
    kern = functools.partial(_expert_kernel, blk=blk)
    grid_spec = pltpu.PrefetchScalarGridSpec(
        num_scalar_prefetch=1,
        grid=(nblk,),
        in_specs=[pl.BlockSpec((1, 1, blk), lambda i, be: (i, 0, 0), memory_space=pltpu.SMEM),
                  pl.BlockSpec((1, 1, blk), lambda i, be: (jnp.minimum(i + 1, nblk - 1), 0, 0),
                               memory_space=pltpu.SMEM),
                  pl.BlockSpec(memory_space=pl.ANY),
                  pl.BlockSpec((1, d, ff), lambda i, be: (be[i], 0, 0)),
                  pl.BlockSpec((1, d, ff), lambda i, be: (be[i], 0, 0)),
                  pl.BlockSpec((1, ff, d), lambda i, be: (be[i], 0, 0))],
        out_specs=pl.BlockSpec((blk, d), lambda i, be: (i, 0)),
        scratch_shapes=[pltpu.VMEM((2, blk, d), F32),
                        pltpu.VMEM((d, ff), BF16),
                        pltpu.VMEM((d, ff), BF16),
                        pltpu.VMEM((ff, d), BF16),
                        pltpu.SemaphoreType.DMA((2,))],
    )
    return pl.pallas_call(
        kern,
        grid_spec=grid_spec,
        out_shape=jax.ShapeDtypeStruct((nblk * blk, d), F32),
        compiler_params=pltpu.CompilerParams(
            dimension_semantics=("arbitrary",), vmem_limit_bytes=VMEM_LIMIT),
        name="experts",
    )(block_expert, row_token3, row_token3, u2, w_gate, w_up, w_down)


def _combine_kernel(d0_ref, d0n_ref, d1_ref, d1n_ref, rows_hbm, x1_ref, rt_ref, mod_ref, lnw_ref, lnb_ref,
                    o_ref, buf, sem, *, tm, alpha):
    i = pl.program_id(0)
    n = pl.num_programs(0)
    slot = i % 2

    def start_gather(i0_ref, i1_ref, s):
        def body(r, carry):
            _row_copy(rows_hbm, buf.at[s, 0], sem.at[s], i0_ref[0, 0, r], r).start()
            _row_copy(rows_hbm, buf.at[s, 1], sem.at[s], i1_ref[0, 0, r], r).start()
            return carry
        lax.fori_loop(0, tm, body, 0)

    def wait_gather(s):
        def body(r, carry):
            _row_copy(rows_hbm, buf.at[s, 0], sem.at[s], 0, r).wait()
            _row_copy(rows_hbm, buf.at[s, 1], sem.at[s], 0, r).wait()
            return carry
        lax.fori_loop(0, tm, body, 0)

    @pl.when(i == 0)
    def _():
        start_gather(d0_ref, d1_ref, 0)

    @pl.when(i + 1 < n)
    def _():
        start_gather(d0n_ref, d1n_ref, 1 - slot)

    wait_gather(slot)
    rt = rt_ref[...]
    y = rt[:, 2:3] * buf[slot, 0] + rt[:, 3:4] * buf[slot, 1]
    gate2 = mod_ref[0, 5:6, :]
    z = alpha * x1_ref[...] + gate2 * y
    o_ref[...] = _layer_norm(z) * lnw_ref[...] + lnb_ref[...]


def _combine(dest0, dest1, rows, x1, route, mod, ln_w, ln_b, *, seq, tm, alpha):
    t, d = x1.shape
    nt = t // tm
    kern = functools.partial(_combine_kernel, tm=tm, alpha=alpha)
    cur = lambda i: (i, 0, 0)
    nxt = lambda i: (jnp.minimum(i + 1, nt - 1), 0, 0)
    idx_spec = lambda m: pl.BlockSpec((1, 1, tm), m, memory_space=pltpu.SMEM)
    row_spec = pl.BlockSpec((tm, d), lambda i: (i, 0))
    vec_spec = pl.BlockSpec((1, d), lambda i: (0, 0))
    return pl.pallas_call(
        kern,
        grid=(nt,),
        in_specs=[idx_spec(cur), idx_spec(nxt), idx_spec(cur), idx_spec(nxt),
                  pl.BlockSpec(memory_space=pl.ANY),
                  row_spec,
                  pl.BlockSpec((tm, LANES), lambda i: (i, 0)),
                  pl.BlockSpec((1,) + mod.shape[1:], lambda i: ((i * tm) // seq, 0, 0)),
                  vec_spec, vec_spec],
        out_specs=row_spec,
        out_shape=jax.ShapeDtypeStruct((t, d), F32),
        scratch_shapes=[pltpu.VMEM((2, 2, tm, d), F32),
                        pltpu.SemaphoreType.DMA((2,))],
        compiler_params=pltpu.CompilerParams(
            dimension_semantics=("arbitrary",), vmem_limit_bytes=VMEM_LIMIT),
        name="combine",
    )(dest0, dest0, dest1, dest1, rows, x1, route, mod, ln_w, ln_b)


def _dispatch_plan(route, blk):
    t = route.shape[0]
    eid = route[:, :2].astype(jnp.int32)
    flat_e = eid.reshape(-1)
    tk = flat_e.shape[0]
    onehot = (flat_e[:, None] == jnp.arange(N_EXPERTS, dtype=jnp.int32)[None, :]).astype(jnp.int32)
    csum = jnp.cumsum(onehot, axis=0)
    pos = jnp.sum(csum * onehot, axis=1) - 1
    counts = csum[-1]
    pcounts = ((counts + blk - 1) // blk) * blk
    pend = jnp.cumsum(pcounts)
    pstart = pend - pcounts
    dest = pstart[flat_e] + pos
    p_rows = ((tk + N_EXPERTS * (blk - 1) + blk - 1) // blk) * blk
    nblk = p_rows // blk
    flat_t = jnp.repeat(jnp.arange(t, dtype=jnp.int32), 2)
    row_token = jnp.zeros((p_rows,), jnp.int32).at[dest].set(flat_t)
    block_expert = jnp.clip(jnp.searchsorted(pend, jnp.arange(nblk, dtype=jnp.int32) * blk, side="right"),
                            0, N_EXPERTS - 1).astype(jnp.int32)
    dest2 = dest.reshape(t, 2)
    return row_token, block_expert, dest2[:, 0], dest2[:, 1]


def _layer(x, c, positions, w_ada, b_ada, w_in, w_g2, b_g2, gla_nw, lq1, lk1, lq2, lk2, diff_nw, w_out,
           ln1_w, ln1_b, w_rg, b_rg, w_re, b_re, w_eg, w_eu, w_ed, ln2_w, ln2_b, *, lam_init,
           tm_in, tn_in, gla_rows, bq, bk, tm_out, moe_blk, tm_cmb):
    bsz, seq, d = x.shape
    t = bsz * seq
    alpha = (2.0 * DEPTH) ** 0.25
    gla_dk = d // (2 * GLA_HEADS)
    gla_dv = d // GLA_HEADS
    dh = d // (2 * DIFF_HEADS)
    hk = GLA_HEADS * gla_dk
    hv = GLA_HEADS * gla_dv
    dq = DIFF_HEADS * 2 * dh
    gr_col = 2 * hk + 2 * hv
    q_col, k_col = 0, hk
    v_col, g_col = 2 * hk, 2 * hk + hv
    dq_col = gr_col
    dk_col = dq_col + dq
    dv_col = dk_col + dq
    mg_col = dv_col + dq
    w_main = jnp.concatenate([w_in[:, :gr_col], w_in[:, gr_col + GLA_GATE_RANK:]], axis=1).astype(BF16)
    w_gr = jnp.pad(w_in[:, gr_col:gr_col + GLA_GATE_RANK], ((0, 0), (0, LANES - GLA_GATE_RANK))).astype(BF16)
    w_g2p = jnp.pad(w_g2, ((0, LANES - GLA_GATE_RANK), (0, 0)))
    x2 = x.reshape(t, d)

    ada = _ada(c, w_ada, b_ada.reshape(1, -1))
    mod = ada.reshape(bsz, 6, d)

    half = dh // 2
    inv = ROPE_THETA ** (-jnp.arange(half, dtype=F32) / half)
    inv_row = jnp.tile(inv, LANES // half).reshape(1, LANES)
    cs = _rope_tables(positions.reshape(t, 1), inv_row, tm_in)

    proj, log_g = _inproj(x2, mod, cs, w_main, w_gr, w_g2p, b_g2.reshape(1, -1), seq=seq, tm=tm_in, tn=tn_in,
                          q_tile=dq_col // tn_in, k_tile=dk_col // tn_in, q_scale=dh ** -0.5 * LOG2_E)

    o_gla = _gla(proj, log_g, gla_nw.reshape(1, -1), bsz=bsz, seq=seq, rows=gla_rows, dk=gla_dk, dv=gla_dv,
                 q_col=q_col, k_col=k_col, v_col=v_col, g_col=g_col)

    lam_vecs = jnp.pad(jnp.stack([lq1, lk1, lq2, lk2]), ((0, 4), (0, LANES - dh)))
    o_diff = _diff(lam_vecs, proj, diff_nw.reshape(1, -1), bsz=bsz, seq=seq, bq=bq, bk=bk, dh=dh,
                   q_col=dq_col, k_col=dk_col, v_col=dv_col, lam_init=lam_init)

    w_r = jnp.pad(jnp.concatenate([w_rg, w_re], axis=1), ((0, 0), (0, LANES - N_GROUPS - N_EXPERTS)))
    b_r = jnp.pad(jnp.concatenate([b_rg, b_re]), (0, LANES - N_GROUPS - N_EXPERTS)).reshape(1, LANES)
    x1, u2, route = _outproj(o_gla, o_diff, proj, x2, mod, w_out.astype(BF16), ln1_w.reshape(1, -1),
                             ln1_b.reshape(1, -1), w_r, b_r, seq=seq, tm=tm_out, ga_col=mg_col, alpha=alpha)

    row_token, block_expert, dest0, dest1 = _dispatch_plan(route, moe_blk)
    rows = _experts(block_expert, row_token.reshape(-1, 1, moe_blk), u2, w_eg, w_eu, w_ed, blk=moe_blk)
    out = _combine(dest0.reshape(-1, 1, tm_cmb), dest1.reshape(-1, 1, tm_cmb), rows, x1, route, mod,
                   ln2_w.reshape(1, -1), ln2_b.reshape(1, -1), seq=seq, tm=tm_cmb, alpha=alpha)
    return out.reshape(bsz, seq, d)


def kernel(x, c, positions, w_ada, b_ada, w_in, w_gla_gate2, b_gla_gate2, gla_norm_w, diff_lambda_q1,
           diff_lambda_k1, diff_lambda_q2, diff_lambda_k2, diff_norm_w, w_out, ln1_w, ln1_b, w_router_group,
           b_router_group, w_router_expert, b_router_expert, w_exp_gate, w_exp_up, w_exp_down, ln2_w, ln2_b):
    assert w_ada.shape[0] == DEPTH
    for l in range(DEPTH):
        lam_init = 0.8 - 0.6 * math.exp(-0.3 * l)
        x = _layer(x, c, positions, w_ada[l], b_ada[l], w_in[l], w_gla_gate2[l], b_gla_gate2[l], gla_norm_w[l],
                   diff_lambda_q1[l], diff_lambda_k1[l], diff_lambda_q2[l], diff_lambda_k2[l], diff_norm_w[l],
                   w_out[l], ln1_w[l], ln1_b[l], w_router_group[l], b_router_group[l], w_router_expert[l],
                   b_router_expert[l], w_exp_gate[l], w_exp_up[l], w_exp_down[l], ln2_w[l], ln2_b[l],
                   lam_init=lam_init, tm_in=1024, tn_in=1024, gla_rows=512, bq=256, bk=512, tm_out=512,
                   moe_blk=128, tm_cmb=256)
    return x
```

```python
import functools
import math

import jax
import jax.numpy as jnp
from jax import lax
from jax.experimental import pallas as pl
from jax.experimental.pallas import tpu as pltpu

F32 = jnp.float32
BF16 = jnp.bfloat16
HIGHEST = lax.Precision.HIGHEST

DEPTH = 1
GLA_HEADS = 4
GLA_GATE_RANK = 16
GLA_TAU = 16.0
GLA_CHUNK = 64
DIFF_HEADS = 8
ROPE_THETA = 10000.0
N_GROUPS = 4
EXPERTS_PER_GROUP = 8
N_EXPERTS = N_GROUPS * EXPERTS_PER_GROUP
LN_EPS = 1e-5
RMS_EPS = 1e-6
LOG2_E = math.log2(math.e)
LANES = 128
VMEM_LIMIT = 56 * 1024 * 1024

NT_DIMS = (((1,), (1,)), ((), ()))
TN_DIMS = (((0,), (0,)), ((), ()))


def _layer_norm(x):
    mu = jnp.mean(x, axis=-1, keepdims=True)
    xc = x - mu
    var = jnp.mean(xc * xc, axis=-1, keepdims=True)
    return xc * lax.rsqrt(var + LN_EPS)


def _silu(x):
    return x * jax.nn.sigmoid(x)


def _ada_kernel(c_ref, w_ref, b_ref, o_ref):
    s = _silu(c_ref[...])
    o_ref[...] = jnp.dot(s, w_ref[...], preferred_element_type=F32, precision=HIGHEST) + b_ref[...]


def _ada(c, w, b):
    bsz, d = c.shape
    n = w.shape[1]
    tn = d
    return pl.pallas_call(
        _ada_kernel,
        grid=(n // tn,),
        in_specs=[pl.BlockSpec((bsz, d), lambda j: (0, 0)),
                  pl.BlockSpec((d, tn), lambda j: (0, j)),
                  pl.BlockSpec((1, tn), lambda j: (0, j))],
        out_specs=pl.BlockSpec((bsz, tn), lambda j: (0, j)),
        out_shape=jax.ShapeDtypeStruct((bsz, n), F32),
        name="ada",
    )(c, w, b)


def _rope_kernel(pos_ref, inv_ref, cs_ref):
    ang = pos_ref[...].astype(F32) * inv_ref[...]
    lane = lax.broadcasted_iota(jnp.int32, ang.shape, 1)
    first_half = (lane % 64) < 32
    sin = jnp.sin(ang)
    cs_ref[:, :LANES] = jnp.cos(ang)
    cs_ref[:, LANES:] = jnp.where(first_half, -sin, sin)


def _rope_tables(pos_col, inv_row, tm):
    t = pos_col.shape[0]
    return pl.pallas_call(
        _rope_kernel,
        grid=(t // tm,),
        in_specs=[pl.BlockSpec((tm, 1), lambda i: (i, 0)),
                  pl.BlockSpec((1, LANES), lambda i: (0, 0))],
        out_specs=pl.BlockSpec((tm, 2 * LANES), lambda i: (i, 0)),
        out_shape=jax.ShapeDtypeStruct((t, 2 * LANES), F32),
        name="rope_tables",
    )(pos_col, inv_row)


def _inproj_kernel(x_ref, mod_ref, cs_ref, w_ref, wgr_ref, wg2_ref, bg2_ref, o_ref, lg_ref, u_ref,
                   *, q_tile, k_tile, q_scale):
    j = pl.program_id(1)

    @pl.when(j == 0)
    def _():
        shift = mod_ref[0, 0:1, :]
        scale = mod_ref[0, 1:2, :]
        u = (_layer_norm(x_ref[...]) * (1.0 + scale) + shift).astype(BF16)
        u_ref[...] = u
        gr = jnp.dot(u, wgr_ref[...], preferred_element_type=F32)
        z = jnp.dot(gr, wg2_ref[...], preferred_element_type=F32, precision=HIGHEST) + bg2_ref[...]
        log_sig = jnp.minimum(z, 0.0) - jnp.log(1.0 + jnp.exp(-jnp.abs(z)))
        lg_ref[...] = log_sig * (1.0 / GLA_TAU)

    acc = jnp.dot(u_ref[...], w_ref[...], preferred_element_type=F32)
    is_rope = jnp.logical_or(j == q_tile, j == k_tile)

    @pl.when(is_rope)
    def _():
        tn = acc.shape[1]
        sc = jnp.where(j == q_tile, q_scale, 1.0).astype(F32)
        cos = cs_ref[:, :LANES] * sc
        sin = cs_ref[:, LANES:] * sc
        lane = lax.broadcasted_iota(jnp.int32, cos.shape, 1)
        first_half = (lane % 64) < 32
        for h in range(tn // LANES):
            t = acc[:, h * LANES:(h + 1) * LANES]
            partner = jnp.where(first_half, pltpu.roll(t, LANES - 32, 1), pltpu.roll(t, 32, 1))
            o_ref[:, h * LANES:(h + 1) * LANES] = (t * cos + partner * sin).astype(o_ref.dtype)

    @pl.when(jnp.logical_not(is_rope))
    def _():
        o_ref[...] = acc.astype(o_ref.dtype)


def _inproj(x2, mod, cs, w_main, w_gr, w_g2, b_g2, *, seq, tm, tn, q_tile, k_tile, q_scale):
    t, d = x2.shape
    n = w_main.shape[1]
    ng = w_g2.shape[1]
    kern = functools.partial(_inproj_kernel, q_tile=q_tile, k_tile=k_tile, q_scale=q_scale)
    return pl.pallas_call(
        kern,
        grid=(t // tm, n // tn),
        in_specs=[pl.BlockSpec((tm, d), lambda i, j: (i, 0)),
                  pl.BlockSpec((1,) + mod.shape[1:], lambda i, j: ((i * tm) // seq, 0, 0)),
                  pl.BlockSpec((tm, 2 * LANES), lambda i, j: (i, 0)),
                  pl.BlockSpec((d, tn), lambda i, j: (0, j)),
                  pl.BlockSpec((d, LANES), lambda i, j: (0, 0)),
                  pl.BlockSpec((LANES, ng), lambda i, j: (0, 0)),
                  pl.BlockSpec((1, ng), lambda i, j: (0, 0))],
        out_specs=[pl.BlockSpec((tm, tn), lambda i, j: (i, j)),
                   pl.BlockSpec((tm, ng), lambda i, j: (i, 0))],
        out_shape=[jax.ShapeDtypeStruct((t, n), BF16),
                   jax.ShapeDtypeStruct((t, ng), F32)],
        scratch_shapes=[pltpu.VMEM((tm, d), BF16)],
        compiler_params=pltpu.CompilerParams(
            dimension_semantics=("arbitrary", "arbitrary"), vmem_limit_bytes=VMEM_LIMIT),
        name="inproj",
    )(x2, mod, cs, w_main, w_gr, w_g2, b_g2)


def _gla_kernel(q_ref, k_ref, v_ref, g_ref, lg_ref, nw_ref, o_ref, st_ref, *, chunk, q_scale):
    @pl.when(pl.program_id(2) == 0)
    def _():
        st_ref[...] = jnp.zeros_like(st_ref)

    rows = q_ref.shape[0]
    r = lax.broadcasted_iota(jnp.int32, (chunk, chunk), 0)
    c = lax.broadcasted_iota(jnp.int32, (chunk, chunk), 1)
    causal = c <= r
    tri = causal.astype(F32)
    nw = nw_ref[...]
    for ci in range(rows // chunk):
        sl = slice(ci * chunk, (ci + 1) * chunk)
        b = jnp.dot(tri, lg_ref[sl, :], preferred_element_type=F32, precision=HIGHEST)
        b_last = b[chunk - 1:chunk, :]
        q = q_ref[sl, :].astype(F32) * q_scale
        k = k_ref[sl, :].astype(F32)
        v = v_ref[sl, :]
        q_t = (q * jnp.exp(b)).astype(BF16)
        k_t = (k * jnp.exp(-b)).astype(BF16)
        k_d = (k * jnp.exp(b_last - b)).astype(BF16)
        attn = lax.dot_general(q_t, k_t, NT_DIMS, preferred_element_type=F32)
        attn = jnp.where(causal, attn, 0.0).astype(BF16)
        st = st_ref[...]
        o = (jnp.dot(attn, v, preferred_element_type=F32)
             + lax.dot_general(q_t, st.astype(BF16), NT_DIMS, preferred_element_type=F32))
        st_ref[...] = st * jnp.exp(b_last) + lax.dot_general(v, k_d, TN_DIMS, preferred_element_type=F32)
        ms = jnp.mean(o * o, axis=-1, keepdims=True)
        y = o * lax.rsqrt(ms + RMS_EPS) * nw * _silu(g_ref[sl, :].astype(F32))
        o_ref[sl, :] = y.astype(o_ref.dtype)


def _gla(proj, log_g, norm_w, *, bsz, seq, rows, dk, dv, q_col, k_col, v_col, g_col):
    t = proj.shape[0]
    nl = seq // rows
    kern = functools.partial(_gla_kernel, chunk=GLA_CHUNK, q_scale=dk ** -0.5)
    row = lambda b, h, l: b * nl + l
    return pl.pallas_call(
        kern,
        grid=(bsz, GLA_HEADS, nl),
        in_specs=[pl.BlockSpec((rows, dk), lambda b, h, l: (row(b, h, l), q_col // dk + h)),
                  pl.BlockSpec((rows, dk), lambda b, h, l: (row(b, h, l), k_col // dk + h)),
                  pl.BlockSpec((rows, dv), lambda b, h, l: (row(b, h, l), v_col // dv + h)),
                  pl.BlockSpec((rows, dv), lambda b, h, l: (row(b, h, l), g_col // dv + h)),
                  pl.BlockSpec((rows, dk), lambda b, h, l: (row(b, h, l), h)),
                  pl.BlockSpec((1, dv), lambda b, h, l: (0, 0))],
        out_specs=pl.BlockSpec((rows, dv), lambda b, h, l: (row(b, h, l), h)),
        out_shape=jax.ShapeDtypeStruct((t, GLA_HEADS * dv), BF16),
        scratch_shapes=[pltpu.VMEM((dv, dk), F32)],
        compiler_params=pltpu.CompilerParams(
            dimension_semantics=("arbitrary", "arbitrary", "arbitrary"), vmem_limit_bytes=VMEM_LIMIT),
        name="gla",
    )(proj, proj, proj, proj, log_g, norm_w)


def _diff_kernel(lam_ref, q_ref, k_ref, v_ref, nw_ref, o_ref, qs_ref, vt_ref, sa_ref, sb_ref, m_ref, l_ref, acc_ref,
                 *, bq, bk, dh, lam_init):
    qi = pl.program_id(2)
    q0 = qi * bq
    seq = k_ref.shape[0]

    @pl.when(qi == 0)
    def _():
        for c in range(seq // bk):
            vt_ref[:, c * bk:(c + 1) * bk] = v_ref[c * bk:(c + 1) * bk, :].astype(F32).T.astype(BF16)

    q = q_ref[...]
    lane = lax.broadcasted_iota(jnp.int32, q.shape, 1)
    zero = jnp.zeros_like(q)
    qs_ref[:bq, :] = jnp.where(lane < dh, q, zero)
    qs_ref[bq:, :] = jnp.where(lane >= dh, q, zero)
    m_ref[...] = jnp.full_like(m_ref, -jnp.inf)
    l_ref[...] = jnp.zeros_like(l_ref)
    acc_ref[...] = jnp.zeros_like(acc_ref)

    def scores(kj, dst):
        k0 = pl.multiple_of(kj * bk, bk)
        dst[...] = lax.dot_general(k_ref[pl.ds(k0, bk), :], qs_ref[...], NT_DIMS, preferred_element_type=F32)

    def update(src, kj, masked):
        k0 = pl.multiple_of(kj * bk, bk)
        s = src[...]
        if masked:
            key = k0 + lax.broadcasted_iota(jnp.int32, s.shape, 0)
            col = lax.broadcasted_iota(jnp.int32, s.shape, 1)
            qpos = q0 + jnp.where(col >= bq, col - bq, col)
            s = jnp.where(key <= qpos, s, -jnp.inf)
        m_prev = m_ref[...]
        m_new = jnp.maximum(m_prev, jnp.max(s, axis=0, keepdims=True))
        alpha = jnp.exp2(m_prev - m_new)
        p = jnp.exp2(s - m_new)
        l_ref[...] = alpha * l_ref[...] + jnp.sum(p, axis=0, keepdims=True)
        acc_ref[...] = alpha * acc_ref[...] + jnp.dot(
            vt_ref[:, pl.ds(k0, bk)], p.astype(BF16), preferred_element_type=F32)
        m_ref[...] = m_new

    n_full = (q0 + 1) // bk
    scores(0, sa_ref)

    def pair_body(jj, carry):
        j = 2 * jj
        scores(j + 1, sb_ref)
        update(sa_ref, j, False)
        scores(j + 2, sa_ref)
        update(sb_ref, j + 1, False)
        return carry

    lax.fori_loop(0, n_full // 2, pair_body, 0)
    odd = (n_full % 2) == 1

    @pl.when(odd)
    def _():
        scores(n_full, sb_ref)
        update(sa_ref, n_full - 1, False)
        update(sb_ref, n_full, True)

    @pl.when(jnp.logical_not(odd))
    def _():
        update(sa_ref, n_full, True)

    lv = lam_ref[...]
    s1 = jnp.sum(lv[0:1, :] * lv[1:2, :], axis=-1, keepdims=True)
    s2 = jnp.sum(lv[2:3, :] * lv[3:4, :], axis=-1, keepdims=True)
    lam = jnp.exp(s1) - jnp.exp(s2) + lam_init
    o_all = acc_ref[...] / l_ref[...]
    o_t = o_all[:, :bq] - lam * o_all[:, bq:]
    ms = jnp.mean(o_t * o_t, axis=0, keepdims=True)
    y = (o_t * lax.rsqrt(ms + RMS_EPS)).T * nw_ref[...] * (1.0 - lam_init)
    o_ref[...] = y.astype(o_ref.dtype)


def _diff(lam_vecs, proj, norm_w, *, bsz, seq, bq, bk, dh, q_col, k_col, v_col, lam_init):
    assert bk % bq == 0 and seq % bk == 0
    t = proj.shape[0]
    w = 2 * dh
    nq = seq // bq
    kern = functools.partial(_diff_kernel, bq=bq, bk=bk, dh=dh, lam_init=lam_init)
    return pl.pallas_call(
        kern,
        grid=(bsz, DIFF_HEADS, nq),
        in_specs=[pl.BlockSpec(lam_vecs.shape, lambda b, h, i: (0, 0)),
                  pl.BlockSpec((bq, w), lambda b, h, i: (b * nq + i, q_col // w + h)),
                  pl.BlockSpec((seq, w), lambda b, h, i: (b, k_col // w + h)),
                  pl.BlockSpec((seq, w), lambda b, h, i: (b, v_col // w + h)),
                  pl.BlockSpec((1, w), lambda b, h, i: (0, 0))],
        out_specs=pl.BlockSpec((bq, w), lambda b, h, i: (b * nq + i, h)),
        out_shape=jax.ShapeDtypeStruct((t, DIFF_HEADS * w), BF16),
        scratch_shapes=[pltpu.VMEM((2 * bq, w), BF16),
                        pltpu.VMEM((w, seq), BF16),
                        pltpu.VMEM((bk, 2 * bq), F32),
                        pltpu.VMEM((bk, 2 * bq), F32),
                        pltpu.VMEM((1, 2 * bq), F32),
                        pltpu.VMEM((1, 2 * bq), F32),
                        pltpu.VMEM((w, 2 * bq), F32)],
        compiler_params=pltpu.CompilerParams(
            dimension_semantics=("arbitrary", "arbitrary", "arbitrary"), vmem_limit_bytes=VMEM_LIMIT),
        name="diff_attn",
    )(lam_vecs, proj, proj, proj, norm_w)


ROW_SUB = 8


def _store_row_tiles(ref, val):
    rows = val.shape[0]
    for s in range(ROW_SUB):
        ref[pl.ds(s, rows, stride=ROW_SUB), :] = val[:, s * LANES:(s + 1) * LANES]


def _load_row_tiles(ref, rows):
    return jnp.concatenate([ref[pl.ds(s, rows, stride=ROW_SUB), :] for s in range(ROW_SUB)], axis=1)


def _tile_copy(src_hbm, dst_vmem, sem, src_row, dst_row):
    return pltpu.make_async_copy(src_hbm.at[pl.ds(src_row * ROW_SUB, ROW_SUB), :],
                                 dst_vmem.at[pl.ds(dst_row * ROW_SUB, ROW_SUB), :], sem)


def _wait_tiles(src_hbm, dst_vmem, sem):
    pltpu.make_async_copy(src_hbm.at[pl.ds(0, dst_vmem.shape[0]), :], dst_vmem, sem).wait()


def _outproj_kernel(og_ref, od_ref, ga_ref, gb_ref, x_ref, mod_ref, wo_ref, lnw_ref, lnb_ref, wr_ref, br_ref,
                    x1_ref, u2_ref, rt_ref, cnt_ref, tri_ref, run_ref, *, alpha):
    @pl.when(pl.program_id(0) == 0)
    def _():
        r = lax.broadcasted_iota(jnp.int32, tri_ref.shape, 0)
        c = lax.broadcasted_iota(jnp.int32, tri_ref.shape, 1)
        tri_ref[...] = (c < r).astype(BF16)
        run_ref[...] = jnp.zeros_like(run_ref)

    g_a = jax.nn.sigmoid(ga_ref[...].astype(F32))
    g_b = jax.nn.sigmoid(gb_ref[...].astype(F32))
    merged = (g_a * og_ref[...].astype(F32) + g_b * od_ref[...].astype(F32)).astype(BF16)
    y = jnp.dot(merged, wo_ref[...], preferred_element_type=F32)
    gate1 = mod_ref[0, 2:3, :]
    shift2 = mod_ref[0, 3:4, :]
    scale2 = mod_ref[0, 4:5, :]
    x1 = _layer_norm(alpha * x_ref[...] + gate1 * y) * lnw_ref[...] + lnb_ref[...]
    x1_ref[...] = x1
    u2 = _layer_norm(x1) * (1.0 + scale2) + shift2
    _store_row_tiles(u2_ref, u2)

    logits = jnp.dot(u2, wr_ref[...], preferred_element_type=F32, precision=HIGHEST) + br_ref[...]
    lane = lax.broadcasted_iota(jnp.int32, logits.shape, 1)
    neg = jnp.float32(-jnp.inf)
    big = jnp.int32(LANES)
    lg = jnp.where(lane < N_GROUPS, logits, neg)
    g_max = jnp.max(lg, axis=-1, keepdims=True)
    w_grp = 1.0 / jnp.sum(jnp.exp(lg - g_max), axis=-1, keepdims=True)
    g_idx = jnp.min(jnp.where(lg == g_max, lane, big), axis=-1, keepdims=True)
    lo = N_GROUPS + EXPERTS_PER_GROUP * g_idx
    le = jnp.where(jnp.logical_and(lane >= lo, lane < lo + EXPERTS_PER_GROUP), logits, neg)
    v1 = jnp.max(le, axis=-1, keepdims=True)
    i1 = jnp.min(jnp.where(le == v1, lane, big), axis=-1, keepdims=True)
    le2 = jnp.where(lane == i1, neg, le)
    v2 = jnp.max(le2, axis=-1, keepdims=True)
    i2 = jnp.min(jnp.where(le2 == v2, lane, big), axis=-1, keepdims=True)
    e2 = jnp.exp(v2 - v1)
    den = 1.0 + e2
    c1 = w_grp / den
    c2 = w_grp * e2 / den

    sel1 = lane == i1 - N_GROUPS
    sel2 = lane == i2 - N_GROUPS
    onehot = jnp.logical_or(sel1, sel2).astype(BF16)
    before = jnp.dot(tri_ref[...], onehot, preferred_element_type=F32) + run_ref[...]
    pos1 = jnp.sum(jnp.where(sel1, before, 0.0), axis=-1, keepdims=True)
    pos2 = jnp.sum(jnp.where(sel2, before, 0.0), axis=-1, keepdims=True)
    run_ref[...] += jnp.sum(onehot.astype(F32), axis=0, keepdims=True)
    cnt_ref[...] = jnp.broadcast_to(run_ref[...], cnt_ref.shape)

    cols = ((i1 - N_GROUPS).astype(F32), (i2 - N_GROUPS).astype(F32), c1, c2, pos1, pos2)
    rt = jnp.zeros_like(logits)
    for li, col in enumerate(cols):
        rt = jnp.where(lane == li, col, rt)
    rt_ref[...] = rt


def _outproj(o_gla, o_diff, proj, x2, mod, w_out, ln_w, ln_b, w_r, b_r, *, seq, tm, ga_col, alpha):
    t, d = x2.shape
    assert d == ROW_SUB * LANES
    kern = functools.partial(_outproj_kernel, alpha=alpha)
    row_spec = pl.BlockSpec((tm, d), lambda i: (i, 0))
    vec_spec = pl.BlockSpec((1, d), lambda i: (0, 0))
    return pl.pallas_call(
        kern,
        grid=(t // tm,),
        in_specs=[row_spec, row_spec,
                  pl.BlockSpec((tm, d), lambda i: (i, ga_col // d)),
                  pl.BlockSpec((tm, d), lambda i: (i, ga_col // d + 1)),
                  row_spec,
                  pl.BlockSpec((1,) + mod.shape[1:], lambda i: ((i * tm) // seq, 0, 0)),
                  pl.BlockSpec((d, d), lambda i: (0, 0)),
                  vec_spec, vec_spec,
                  pl.BlockSpec((d, LANES), lambda i: (0, 0)),
                  pl.BlockSpec((1, LANES), lambda i: (0, 0))],
        out_specs=[row_spec,
                   pl.BlockSpec((tm * ROW_SUB, LANES), lambda i: (i, 0)),
                   pl.BlockSpec((tm, LANES), lambda i: (i, 0)),
                   pl.BlockSpec((ROW_SUB, LANES), lambda i: (0, 0))],
        out_shape=[jax.ShapeDtypeStruct((t, d), F32),
                   jax.ShapeDtypeStruct((t * ROW_SUB, LANES), F32),
                   jax.ShapeDtypeStruct((t, LANES), F32),
                   jax.ShapeDtypeStruct((ROW_SUB, LANES), F32)],
        scratch_shapes=[pltpu.VMEM((tm, tm), BF16),
                        pltpu.VMEM((1, LANES), F32)],
        compiler_params=pltpu.CompilerParams(
            dimension_semantics=("arbitrary",), vmem_limit_bytes=VMEM_LIMIT),
        name="outproj",
    )(o_gla, o_diff, proj, proj, x2, mod, w_out, ln_w, ln_b, w_r, b_r)


def _expert_kernel(be_ref, tok_ref, tokn_ref, u_hbm, wg_ref, wu_ref, wd_ref, o_ref,
                   xbuf, wgb, wub, wdb, sem, *, blk):
    i = pl.program_id(0)
    n = pl.num_programs(0)
    slot = i % 2

    def start_gather(idx_ref, s):
        def body(r, carry):
            _tile_copy(u_hbm, xbuf.at[s], sem.at[s], idx_ref[0, 0, r], r).start()
            return carry
        lax.fori_loop(0, blk, body, 0, unroll=8)

    @pl.when(i == 0)
    def _():
        start_gather(tok_ref, 0)

    changed = jnp.logical_or(i == 0, be_ref[i] != be_ref[jnp.maximum(i - 1, 0)])

    @pl.when(changed)
    def _():
        wgb[...] = wg_ref[0].astype(BF16)
        wub[...] = wu_ref[0].astype(BF16)
        wdb[...] = wd_ref[0].astype(BF16)

    start_gather(tokn_ref, 1 - slot)
    _wait_tiles(u_hbm, xbuf.at[slot], sem.at[slot])
    xb = _load_row_tiles(xbuf.at[slot], blk).astype(BF16)
    hg = jnp.dot(xb, wgb[...], preferred_element_type=F32)
    hu = jnp.dot(xb, wub[...], preferred_element_type=F32)
    h = (_silu(hg) * hu).astype(BF16)
    _store_row_tiles(o_ref, jnp.dot(h, wdb[...], preferred_element_type=F32))

    @pl.when(i == n - 1)
    def _():
        _wait_tiles(u_hbm, xbuf.at[1 - slot], sem.at[1 - slot])


def _experts(block_expert, row_token3, u2_tiles, w_gate, w_up, w_down, *, blk):
    nblk = row_token3.shape[0]
    e, d, ff = w_gate.shape
    kern = functools.partial(_expert_kernel, blk=blk)
    grid_spec = pltpu.PrefetchScalarGridSpec(
        num_scalar_prefetch=1,
        grid=(nblk,),
        in_specs=[pl.BlockSpec((1, 1, blk), lambda i, be: (i, 0, 0), memory_space=pltpu.SMEM),
                  pl.BlockSpec((1, 1, blk), lambda i, be: (jnp.minimum(i + 1, nblk - 1), 0, 0),
                               memory_space=pltpu.SMEM),
                  pl.BlockSpec(memory_space=pl.ANY),
                  pl.BlockSpec((1, d, ff), lambda i, be: (be[i], 0, 0)),
                  pl.BlockSpec((1, d, ff), lambda i, be: (be[i], 0, 0)),
                  pl.BlockSpec((1, ff, d), lambda i, be: (be[i], 0, 0))],
        out_specs=pl.BlockSpec((blk * ROW_SUB, LANES), lambda i, be: (i, 0)),
        scratch_shapes=[pltpu.VMEM((2, blk * ROW_SUB, LANES), F32),
                        pltpu.VMEM((d, ff), BF16),
                        pltpu.VMEM((d, ff), BF16),
                        pltpu.VMEM((ff, d), BF16),
                        pltpu.SemaphoreType.DMA((2,))],
    )
    return pl.pallas_call(
        kern,
        grid_spec=grid_spec,
        out_shape=jax.ShapeDtypeStruct((nblk * blk * ROW_SUB, LANES), F32),
        compiler_params=pltpu.CompilerParams(
            dimension_semantics=("arbitrary",), vmem_limit_bytes=VMEM_LIMIT),
        name="experts",
    )(block_expert, row_token3, row_token3, u2_tiles, w_gate, w_up, w_down)


def _combine_kernel(d0_ref, d0n_ref, d1_ref, d1n_ref, rows_hbm, x1_ref, rt_ref, mod_ref, lnw_ref, lnb_ref,
                    o_ref, buf, sem, *, tm, alpha):
    i = pl.program_id(0)
    n = pl.num_programs(0)
    slot = i % 2

    def start_gather(i0_ref, i1_ref, s):
        def body(r, carry):
            _tile_copy(rows_hbm, buf.at[s, 0], sem.at[s], i0_ref[0, 0, r], r).start()
            _tile_copy(rows_hbm, buf.at[s, 1], sem.at[s], i1_ref[0, 0, r], r).start()
            return carry
        lax.fori_loop(0, tm, body, 0, unroll=4)

    def wait_gather(s):
        _wait_tiles(rows_hbm, buf.at[s, 0], sem.at[s])
        _wait_tiles(rows_hbm, buf.at[s, 1], sem.at[s])

    @pl.when(i == 0)
    def _():
        start_gather(d0_ref, d1_ref, 0)

    start_gather(d0n_ref, d1n_ref, 1 - slot)
    wait_gather(slot)
    rt = rt_ref[...]
    y = rt[:, 2:3] * _load_row_tiles(buf.at[slot, 0], tm) + rt[:, 3:4] * _load_row_tiles(buf.at[slot, 1], tm)
    gate2 = mod_ref[0, 5:6, :]
    z = alpha * x1_ref[...] + gate2 * y
    o_ref[...] = _layer_norm(z) * lnw_ref[...] + lnb_ref[...]

    @pl.when(i == n - 1)
    def _():
        wait_gather(1 - slot)


def _combine(dest0, dest1, rows_tiles, x1, route, mod, ln_w, ln_b, *, seq, tm, alpha):
    t, d = x1.shape
    nt = t // tm
    kern = functools.partial(_combine_kernel, tm=tm, alpha=alpha)
    cur = lambda i: (i, 0, 0)
    nxt = lambda i: (jnp.minimum(i + 1, nt - 1), 0, 0)
    idx_spec = lambda m: pl.BlockSpec((1, 1, tm), m, memory_space=pltpu.SMEM)
    row_spec = pl.BlockSpec((tm, d), lambda i: (i, 0))
    vec_spec = pl.BlockSpec((1, d), lambda i: (0, 0))
    return pl.pallas_call(
        kern,
        grid=(nt,),
        in_specs=[idx_spec(cur), idx_spec(nxt), idx_spec(cur), idx_spec(nxt),
                  pl.BlockSpec(memory_space=pl.ANY),
                  row_spec,
                  pl.BlockSpec((tm, LANES), lambda i: (i, 0)),
                  pl.BlockSpec((1,) + mod.shape[1:], lambda i: ((i * tm) // seq, 0, 0)),
                  vec_spec, vec_spec],
        out_specs=row_spec,
        out_shape=jax.ShapeDtypeStruct((t, d), F32),
        scratch_shapes=[pltpu.VMEM((2, 2, tm * ROW_SUB, LANES), F32),
                        pltpu.SemaphoreType.DMA((2,))],
        compiler_params=pltpu.CompilerParams(
            dimension_semantics=("arbitrary",), vmem_limit_bytes=VMEM_LIMIT),
        name="combine",
    )(dest0, dest0, dest1, dest1, rows_tiles, x1, route, mod, ln_w, ln_b)


def _dispatch_plan(route, counts, blk):
    t = route.shape[0]
    eid = route[:, :2].astype(jnp.int32)
    pos = route[:, 4:6].astype(jnp.int32)
    counts = counts[:N_EXPERTS].astype(jnp.int32)
    pcounts = ((counts + blk - 1) // blk) * blk
    pend = jnp.cumsum(pcounts)
    pstart = pend - pcounts
    dest = pstart[eid] + pos
    tk = 2 * t
    p_rows = ((tk + N_EXPERTS * (blk - 1) + blk - 1) // blk) * blk
    nblk = p_rows // blk
    tok = jnp.broadcast_to(jnp.arange(t, dtype=jnp.int32)[:, None], (t, 2))
    row_token = jnp.zeros((p_rows,), jnp.int32).at[dest.reshape(-1)].set(tok.reshape(-1))
    block_expert = jnp.clip(jnp.searchsorted(pend, jnp.arange(nblk, dtype=jnp.int32) * blk, side="right"),
                            0, N_EXPERTS - 1).astype(jnp.int32)
    return row_token, block_expert, dest[:, 0], dest[:, 1]


def _layer(x, c, positions, w_ada, b_ada, w_in, w_g2, b_g2, gla_nw, lq1, lk1, lq2, lk2, diff_nw, w_out,
           ln1_w, ln1_b, w_rg, b_rg, w_re, b_re, w_eg, w_eu, w_ed, ln2_w, ln2_b, *, lam_init,
           tm_in, tn_in, gla_rows, bq, bk, tm_out, moe_blk, tm_cmb):
    bsz, seq, d = x.shape
    t = bsz * seq
    alpha = (2.0 * DEPTH) ** 0.25
    gla_dk = d // (2 * GLA_HEADS)
    gla_dv = d // GLA_HEADS
    dh = d // (2 * DIFF_HEADS)
    hk = GLA_HEADS * gla_dk
    hv = GLA_HEADS * gla_dv
    dq = DIFF_HEADS * 2 * dh
    gr_col = 2 * hk + 2 * hv
    q_col, k_col = 0, hk
    v_col, g_col = 2 * hk, 2 * hk + hv
    dq_col = gr_col
    dk_col = dq_col + dq
    dv_col = dk_col + dq
    mg_col = dv_col + dq
    w_main = jnp.concatenate([w_in[:, :gr_col], w_in[:, gr_col + GLA_GATE_RANK:]], axis=1).astype(BF16)
    w_gr = jnp.pad(w_in[:, gr_col:gr_col + GLA_GATE_RANK], ((0, 0), (0, LANES - GLA_GATE_RANK))).astype(BF16)
    w_g2p = jnp.pad(w_g2, ((0, LANES - GLA_GATE_RANK), (0, 0)))
    x2 = x.reshape(t, d)

    ada = _ada(c, w_ada, b_ada.reshape(1, -1))
    mod = ada.reshape(bsz, 6, d)

    half = dh // 2
    inv = ROPE_THETA ** (-jnp.arange(half, dtype=F32) / half)
    inv_row = jnp.tile(inv, LANES // half).reshape(1, LANES)
    cs = _rope_tables(positions.reshape(t, 1), inv_row, tm_in)

    proj, log_g = _inproj(x2, mod, cs, w_main, w_gr, w_g2p, b_g2.reshape(1, -1), seq=seq, tm=tm_in, tn=tn_in,
                          q_tile=dq_col // tn_in, k_tile=dk_col // tn_in, q_scale=dh ** -0.5 * LOG2_E)

    o_gla = _gla(proj, log_g, gla_nw.reshape(1, -1), bsz=bsz, seq=seq, rows=gla_rows, dk=gla_dk, dv=gla_dv,
                 q_col=q_col, k_col=k_col, v_col=v_col, g_col=g_col)

    lam_vecs = jnp.pad(jnp.stack([lq1, lk1, lq2, lk2]), ((0, 4), (0, LANES - dh)))
    o_diff = _diff(lam_vecs, proj, diff_nw.reshape(1, -1), bsz=bsz, seq=seq, bq=bq, bk=bk, dh=dh,
                   q_col=dq_col, k_col=dk_col, v_col=dv_col, lam_init=lam_init)

    w_r = jnp.pad(jnp.concatenate([w_rg, w_re], axis=1), ((0, 0), (0, LANES - N_GROUPS - N_EXPERTS)))
    b_r = jnp.pad(jnp.concatenate([b_rg, b_re]), (0, LANES - N_GROUPS - N_EXPERTS)).reshape(1, LANES)
    x1, u2_tiles, route, counts = _outproj(o_gla, o_diff, proj, x2, mod, w_out.astype(BF16), ln1_w.reshape(1, -1),
                                           ln1_b.reshape(1, -1), w_r, b_r, seq=seq, tm=tm_out, ga_col=mg_col,
                                           alpha=alpha)

    row_token, block_expert, dest0, dest1 = _dispatch_plan(route, counts[0], moe_blk)
    rows = _experts(block_expert, row_token.reshape(-1, 1, moe_blk), u2_tiles, w_eg, w_eu, w_ed, blk=moe_blk)
    out = _combine(dest0.reshape(-1, 1, tm_cmb), dest1.reshape(-1, 1, tm_cmb), rows, x1, route, mod,
                   ln2_w.reshape(1, -1), ln2_b.reshape(1, -1), seq=seq, tm=tm_cmb, alpha=alpha)
    return out.reshape(bsz, seq, d)


def kernel(x, c, positions, w_ada, b_ada, w_in, w_gla_gate2, b_gla_gate2, gla_norm_w, diff_lambda_q1,
           diff_lambda_k1, diff_lambda_q2, diff_lambda_k2, diff_norm_w, w_out, ln1_w, ln1_b, w_router_group,
           b_router_group, w_router_expert, b_router_expert, w_exp_gate, w_exp_up, w_exp_down, ln2_w, ln2_b):
    assert w_ada.shape[0] == DEPTH
    for l in range(DEPTH):
        lam_init = 0.8 - 0.6 * math.exp(-0.3 * l)
        x = _layer(x, c, positions, w_ada[l], b_ada[l], w_in[l], w_gla_gate2[l], b_gla_gate2[l], gla_norm_w[l],
                   diff_lambda_q1[l], diff_lambda_k1[l], diff_lambda_q2[l], diff_lambda_k2[l], diff_norm_w[l],
                   w_out[l], ln1_w[l], ln1_b[l], w_router_group[l], b_router_group[l], w_router_expert[l],
                   b_router_expert[l], w_exp_gate[l], w_exp_up[l], w_exp_down[l], ln2_w[l], ln2_b[l],
                   lam_init=lam_init, tm_in=1024, tn_in=1024, gla_rows=512, bq=512, bk=512, tm_out=512,
                   moe_blk=128, tm_cmb=256)
    return x
```

```python
import functools
import math

import jax
import jax.numpy as jnp
from jax import lax
from jax.experimental import pallas as pl
from jax.experimental.pallas import tpu as pltpu

F32 = jnp.float32
BF16 = jnp.bfloat16
HIGHEST = lax.Precision.HIGHEST

DEPTH = 1
GLA_HEADS = 4
GLA_GATE_RANK = 16
GLA_TAU = 16.0
GLA_CHUNK = 64
DIFF_HEADS = 8
ROPE_THETA = 10000.0
N_GROUPS = 4
EXPERTS_PER_GROUP = 8
N_EXPERTS = N_GROUPS * EXPERTS_PER_GROUP
LN_EPS = 1e-5
RMS_EPS = 1e-6
LOG2_E = math.log2(math.e)
LANES = 128
VMEM_LIMIT = 56 * 1024 * 1024

NT_DIMS = (((1,), (1,)), ((), ()))
TN_DIMS = (((0,), (0,)), ((), ()))


def _layer_norm(x):
    mu = jnp.mean(x, axis=-1, keepdims=True)
    xc = x - mu
    var = jnp.mean(xc * xc, axis=-1, keepdims=True)
    return xc * lax.rsqrt(var + LN_EPS)


def _silu(x):
    return x * jax.nn.sigmoid(x)


def _ada_kernel(c_ref, w_ref, b_ref, o_ref):
    s = _silu(c_ref[...])
    o_ref[...] = jnp.dot(s, w_ref[...], preferred_element_type=F32, precision=HIGHEST) + b_ref[...]


def _ada(c, w, b):
    bsz, d = c.shape
    n = w.shape[1]
    tn = d
    return pl.pallas_call(
        _ada_kernel,
        grid=(n // tn,),
        in_specs=[pl.BlockSpec((bsz, d), lambda j: (0, 0)),
                  pl.BlockSpec((d, tn), lambda j: (0, j)),
                  pl.BlockSpec((1, tn), lambda j: (0, j))],
        out_specs=pl.BlockSpec((bsz, tn), lambda j: (0, j)),
        out_shape=jax.ShapeDtypeStruct((bsz, n), F32),
        name="ada",
    )(c, w, b)


def _rope_kernel(pos_ref, inv_ref, cs_ref):
    ang = pos_ref[...].astype(F32) * inv_ref[...]
    lane = lax.broadcasted_iota(jnp.int32, ang.shape, 1)
    first_half = (lane % 64) < 32
    sin = jnp.sin(ang)
    cs_ref[:, :LANES] = jnp.cos(ang)
    cs_ref[:, LANES:] = jnp.where(first_half, -sin, sin)


def _rope_tables(pos_col, inv_row, tm):
    t = pos_col.shape[0]
    return pl.pallas_call(
        _rope_kernel,
        grid=(t // tm,),
        in_specs=[pl.BlockSpec((tm, 1), lambda i: (i, 0)),
                  pl.BlockSpec((1, LANES), lambda i: (0, 0))],
        out_specs=pl.BlockSpec((tm, 2 * LANES), lambda i: (i, 0)),
        out_shape=jax.ShapeDtypeStruct((t, 2 * LANES), F32),
        name="rope_tables",
    )(pos_col, inv_row)


def _inproj_kernel(x_ref, mod_ref, cs_ref, w_ref, wgr_ref, wg2_ref, bg2_ref, o_ref, lg_ref, u_ref,
                   *, q_tile, k_tile, q_scale):
    j = pl.program_id(1)

    @pl.when(j == 0)
    def _():
        shift = mod_ref[0, 0:1, :]
        scale = mod_ref[0, 1:2, :]
        u = (_layer_norm(x_ref[...]) * (1.0 + scale) + shift).astype(BF16)
        u_ref[...] = u
        gr = jnp.dot(u, wgr_ref[...], preferred_element_type=F32)
        z = jnp.dot(gr, wg2_ref[...], preferred_element_type=F32, precision=HIGHEST) + bg2_ref[...]
        log_sig = jnp.minimum(z, 0.0) - jnp.log(1.0 + jnp.exp(-jnp.abs(z)))
        lg_ref[...] = log_sig * (1.0 / GLA_TAU)

    acc = jnp.dot(u_ref[...], w_ref[...], preferred_element_type=F32)
    is_rope = jnp.logical_or(j == q_tile, j == k_tile)

    @pl.when(is_rope)
    def _():
        tn = acc.shape[1]
        sc = jnp.where(j == q_tile, q_scale, 1.0).astype(F32)
        cos = cs_ref[:, :LANES] * sc
        sin = cs_ref[:, LANES:] * sc
        lane = lax.broadcasted_iota(jnp.int32, cos.shape, 1)
        first_half = (lane % 64) < 32
        for h in range(tn // LANES):
            t = acc[:, h * LANES:(h + 1) * LANES]
            partner = jnp.where(first_half, pltpu.roll(t, LANES - 32, 1), pltpu.roll(t, 32, 1))
            o_ref[:, h * LANES:(h + 1) * LANES] = (t * cos + partner * sin).astype(o_ref.dtype)

    @pl.when(jnp.logical_not(is_rope))
    def _():
        o_ref[...] = acc.astype(o_ref.dtype)


def _inproj(x2, mod, cs, w_main, w_gr, w_g2, b_g2, *, seq, tm, tn, q_tile, k_tile, q_scale):
    t, d = x2.shape
    n = w_main.shape[1]
    ng = w_g2.shape[1]
    kern = functools.partial(_inproj_kernel, q_tile=q_tile, k_tile=k_tile, q_scale=q_scale)
    return pl.pallas_call(
        kern,
        grid=(t // tm, n // tn),
        in_specs=[pl.BlockSpec((tm, d), lambda i, j: (i, 0)),
                  pl.BlockSpec((1,) + mod.shape[1:], lambda i, j: ((i * tm) // seq, 0, 0)),
                  pl.BlockSpec((tm, 2 * LANES), lambda i, j: (i, 0)),
                  pl.BlockSpec((d, tn), lambda i, j: (0, j)),
                  pl.BlockSpec((d, LANES), lambda i, j: (0, 0)),
                  pl.BlockSpec((LANES, ng), lambda i, j: (0, 0)),
                  pl.BlockSpec((1, ng), lambda i, j: (0, 0))],
        out_specs=[pl.BlockSpec((tm, tn), lambda i, j: (i, j)),
                   pl.BlockSpec((tm, ng), lambda i, j: (i, 0))],
        out_shape=[jax.ShapeDtypeStruct((t, n), BF16),
                   jax.ShapeDtypeStruct((t, ng), F32)],
        scratch_shapes=[pltpu.VMEM((tm, d), BF16)],
        compiler_params=pltpu.CompilerParams(
            dimension_semantics=("arbitrary", "arbitrary"), vmem_limit_bytes=VMEM_LIMIT),
        name="inproj",
    )(x2, mod, cs, w_main, w_gr, w_g2, b_g2)


def _gla_kernel(q_ref, k_ref, v_ref, g_ref, lg_ref, nw_ref, o_ref, st_ref, *, chunk, q_scale):
    @pl.when(pl.program_id(2) == 0)
    def _():
        st_ref[...] = jnp.zeros_like(st_ref)

    rows = q_ref.shape[0]
    n_chunks = rows // chunk

    b = lg_ref[...]
    pos = lax.broadcasted_iota(jnp.int32, b.shape, 0) % chunk
    step = 1
    while step < chunk:
        b = b + jnp.where(pos >= step, pltpu.roll(b, step, 0), 0.0)
        step *= 2

    k = k_ref[...].astype(F32)
    q_t = (q_ref[...].astype(F32) * q_scale * jnp.exp(b)).astype(BF16)
    k_t = (k * jnp.exp(-b)).astype(BF16)
    v = v_ref[...]

    r = lax.broadcasted_iota(jnp.int32, (rows, rows), 0)
    c = lax.broadcasted_iota(jnp.int32, (rows, rows), 1)
    keep = jnp.logical_and(c <= r, (r // chunk) == (c // chunk))
    attn = lax.dot_general(q_t, k_t, NT_DIMS, preferred_element_type=F32)
    o = jnp.dot(jnp.where(keep, attn, 0.0).astype(BF16), v, preferred_element_type=F32)

    st = st_ref[...]
    o_inter = []
    for ci in range(n_chunks):
        sl = slice(ci * chunk, (ci + 1) * chunk)
        b_c = b[sl, :]
        b_last = b_c[chunk - 1:chunk, :]
        k_d = (k[sl, :] * jnp.exp(b_last - b_c)).astype(BF16)
        kv = lax.dot_general(v[sl, :], k_d, TN_DIMS, preferred_element_type=F32)
        o_inter.append(lax.dot_general(q_t[sl, :], st.astype(BF16), NT_DIMS, preferred_element_type=F32))
        st = st * jnp.exp(b_last) + kv
    st_ref[...] = st
    o = o + jnp.concatenate(o_inter, axis=0)

    ms = jnp.mean(o * o, axis=-1, keepdims=True)
    y = o * lax.rsqrt(ms + RMS_EPS) * nw_ref[...] * _silu(g_ref[...].astype(F32))
    o_ref[...] = y.astype(o_ref.dtype)


def _gla(proj, log_g, norm_w, *, bsz, seq, rows, dk, dv, q_col, k_col, v_col, g_col):
    t = proj.shape[0]
    nl = seq // rows
    kern = functools.partial(_gla_kernel, chunk=GLA_CHUNK, q_scale=dk ** -0.5)
    row = lambda b, h, l: b * nl + l
    return pl.pallas_call(
        kern,
        grid=(bsz, GLA_HEADS, nl),
        in_specs=[pl.BlockSpec((rows, dk), lambda b, h, l: (row(b, h, l), q_col // dk + h)),
                  pl.BlockSpec((rows, dk), lambda b, h, l: (row(b, h, l), k_col // dk + h)),
                  pl.BlockSpec((rows, dv), lambda b, h, l: (row(b, h, l), v_col // dv + h)),
                  pl.BlockSpec((rows, dv), lambda b, h, l: (row(b, h, l), g_col // dv + h)),
                  pl.BlockSpec((rows, dk), lambda b, h, l: (row(b, h, l), h)),
                  pl.BlockSpec((1, dv), lambda b, h, l: (0, 0))],
        out_specs=pl.BlockSpec((rows, dv), lambda b, h, l: (row(b, h, l), h)),
        out_shape=jax.ShapeDtypeStruct((t, GLA_HEADS * dv), BF16),
        scratch_shapes=[pltpu.VMEM((dv, dk), F32)],
        compiler_params=pltpu.CompilerParams(
            dimension_semantics=("arbitrary", "arbitrary", "arbitrary"), vmem_limit_bytes=VMEM_LIMIT),
        name="gla",
    )(proj, proj, proj, proj, log_g, norm_w)


def _diff_kernel(lam_ref, q_ref, k_ref, v_ref, nw_ref, o_ref, qs_ref, vt_ref, sa_ref, sb_ref, m_ref, l_ref, acc_ref,
                 *, bq, bk, dh, lam_init):
    qi = pl.program_id(2)
    q0 = qi * bq
    seq = k_ref.shape[0]

    @pl.when(qi == 0)
    def _():
        for c in range(seq // bk):
            vt_ref[:, c * bk:(c + 1) * bk] = v_ref[c * bk:(c + 1) * bk, :].astype(F32).T.astype(BF16)

    q = q_ref[...]
    lane = lax.broadcasted_iota(jnp.int32, q.shape, 1)
    zero = jnp.zeros_like(q)
    qs_ref[:bq, :] = jnp.where(lane < dh, q, zero)
    qs_ref[bq:, :] = jnp.where(lane >= dh, q, zero)
    m_ref[...] = jnp.full_like(m_ref, -jnp.inf)
    l_ref[...] = jnp.zeros_like(l_ref)
    acc_ref[...] = jnp.zeros_like(acc_ref)

    def scores(kj, dst):
        k0 = pl.multiple_of(kj * bk, bk)
        dst[...] = lax.dot_general(k_ref[pl.ds(k0, bk), :], qs_ref[...], NT_DIMS, preferred_element_type=F32)

    def update(src, kj, masked):
        k0 = pl.multiple_of(kj * bk, bk)
        s = src[...]
        if masked:
            key = k0 + lax.broadcasted_iota(jnp.int32, s.shape, 0)
            col = lax.broadcasted_iota(jnp.int32, s.shape, 1)
            qpos = q0 + jnp.where(col >= bq, col - bq, col)
            s = jnp.where(key <= qpos, s, -jnp.inf)
        m_prev = m_ref[...]
        m_new = jnp.maximum(m_prev, jnp.max(s, axis=0, keepdims=True))
        alpha = jnp.exp2(m_prev - m_new)
        p = jnp.exp2(s - m_new)
        l_ref[...] = alpha * l_ref[...] + jnp.sum(p, axis=0, keepdims=True)
        acc_ref[...] = alpha * acc_ref[...] + jnp.dot(
            vt_ref[:, pl.ds(k0, bk)], p.astype(BF16), preferred_element_type=F32)
        m_ref[...] = m_new

    n_full = (q0 + 1) // bk
    scores(0, sa_ref)

    def pair_body(jj, carry):
        j = 2 * jj
        scores(j + 1, sb_ref)
        update(sa_ref, j, False)
        scores(j + 2, sa_ref)
        update(sb_ref, j + 1, False)
        return carry

    lax.fori_loop(0, n_full // 2, pair_body, 0)
    odd = (n_full % 2) == 1

    @pl.when(odd)
    def _():
        scores(n_full, sb_ref)
        update(sa_ref, n_full - 1, False)
        update(sb_ref, n_full, True)

    @pl.when(jnp.logical_not(odd))
    def _():
        update(sa_ref, n_full, True)

    lv = lam_ref[...]
    s1 = jnp.sum(lv[0:1, :] * lv[1:2, :], axis=-1, keepdims=True)
    s2 = jnp.sum(lv[2:3, :] * lv[3:4, :], axis=-1, keepdims=True)
    lam = jnp.exp(s1) - jnp.exp(s2) + lam_init
    o_all = acc_ref[...] / l_ref[...]
    o_t = o_all[:, :bq] - lam * o_all[:, bq:]
    ms = jnp.mean(o_t * o_t, axis=0, keepdims=True)
    y = (o_t * lax.rsqrt(ms + RMS_EPS)).T * nw_ref[...] * (1.0 - lam_init)
    o_ref[...] = y.astype(o_ref.dtype)


def _diff(lam_vecs, proj, norm_w, *, bsz, seq, bq, bk, dh, q_col, k_col, v_col, lam_init):
    assert bk % bq == 0 and seq % bk == 0
    t = proj.shape[0]
    w = 2 * dh
    nq = seq // bq
    kern = functools.partial(_diff_kernel, bq=bq, bk=bk, dh=dh, lam_init=lam_init)
    return pl.pallas_call(
        kern,
        grid=(bsz, DIFF_HEADS, nq),
        in_specs=[pl.BlockSpec(lam_vecs.shape, lambda b, h, i: (0, 0)),
                  pl.BlockSpec((bq, w), lambda b, h, i: (b * nq + i, q_col // w + h)),
                  pl.BlockSpec((seq, w), lambda b, h, i: (b, k_col // w + h)),
                  pl.BlockSpec((seq, w), lambda b, h, i: (b, v_col // w + h)),
                  pl.BlockSpec((1, w), lambda b, h, i: (0, 0))],
        out_specs=pl.BlockSpec((bq, w), lambda b, h, i: (b * nq + i, h)),
        out_shape=jax.ShapeDtypeStruct((t, DIFF_HEADS * w), BF16),
        scratch_shapes=[pltpu.VMEM((2 * bq, w), BF16),
                        pltpu.VMEM((w, seq), BF16),
                        pltpu.VMEM((bk, 2 * bq), F32),
                        pltpu.VMEM((bk, 2 * bq), F32),
                        pltpu.VMEM((1, 2 * bq), F32),
                        pltpu.VMEM((1, 2 * bq), F32),
                        pltpu.VMEM((w, 2 * bq), F32)],
        compiler_params=pltpu.CompilerParams(
            dimension_semantics=("arbitrary", "arbitrary", "arbitrary"), vmem_limit_bytes=VMEM_LIMIT),
        name="diff_attn",
    )(lam_vecs, proj, proj, proj, norm_w)


ROW_SUB = 8


def _store_row_tiles(ref, val):
    rows = val.shape[0]
    for s in range(ROW_SUB):
        ref[pl.ds(s, rows, stride=ROW_SUB), :] = val[:, s * LANES:(s + 1) * LANES]


def _load_row_tiles(ref, rows):
    return jnp.concatenate([ref[pl.ds(s, rows, stride=ROW_SUB), :] for s in range(ROW_SUB)], axis=1)


def _tile_copy(src_hbm, dst_vmem, sem, src_row, dst_row):
    return pltpu.make_async_copy(src_hbm.at[pl.ds(src_row * ROW_SUB, ROW_SUB), :],
                                 dst_vmem.at[pl.ds(dst_row * ROW_SUB, ROW_SUB), :], sem)


def _wait_tiles(src_hbm, dst_vmem, sem):
    pltpu.make_async_copy(src_hbm.at[pl.ds(0, dst_vmem.shape[0]), :], dst_vmem, sem).wait()


def _outproj_kernel(og_ref, od_ref, ga_ref, gb_ref, x_ref, mod_ref, wo_ref, lnw_ref, lnb_ref, wr_ref, br_ref,
                    x1_ref, u2_ref, rt_ref, cnt_ref, tri_ref, run_ref, *, alpha):
    @pl.when(pl.program_id(0) == 0)
    def _():
        r = lax.broadcasted_iota(jnp.int32, tri_ref.shape, 0)
        c = lax.broadcasted_iota(jnp.int32, tri_ref.shape, 1)
        tri_ref[...] = (c < r).astype(BF16)
        run_ref[...] = jnp.zeros_like(run_ref)

    g_a = jax.nn.sigmoid(ga_ref[...].astype(F32))
    g_b = jax.nn.sigmoid(gb_ref[...].astype(F32))
    merged = (g_a * og_ref[...].astype(F32) + g_b * od_ref[...].astype(F32)).astype(BF16)
    y = jnp.dot(merged, wo_ref[...], preferred_element_type=F32)
    gate1 = mod_ref[0, 2:3, :]
    shift2 = mod_ref[0, 3:4, :]
    scale2 = mod_ref[0, 4:5, :]
    x1 = _layer_norm(alpha * x_ref[...] + gate1 * y) * lnw_ref[...] + lnb_ref[...]
    x1_ref[...] = x1
    u2 = _layer_norm(x1) * (1.0 + scale2) + shift2
    _store_row_tiles(u2_ref, u2)

    logits = jnp.dot(u2, wr_ref[...], preferred_element_type=F32, precision=HIGHEST) + br_ref[...]
    lane = lax.broadcasted_iota(jnp.int32, logits.shape, 1)
    neg = jnp.float32(-jnp.inf)
    big = jnp.int32(LANES)
    lg = jnp.where(lane < N_GROUPS, logits, neg)
    g_max = jnp.max(lg, axis=-1, keepdims=True)
    w_grp = 1.0 / jnp.sum(jnp.exp(lg - g_max), axis=-1, keepdims=True)
    g_idx = jnp.min(jnp.where(lg == g_max, lane, big), axis=-1, keepdims=True)
    lo = N_GROUPS + EXPERTS_PER_GROUP * g_idx
    le = jnp.where(jnp.logical_and(lane >= lo, lane < lo + EXPERTS_PER_GROUP), logits, neg)
    v1 = jnp.max(le, axis=-1, keepdims=True)
    i1 = jnp.min(jnp.where(le == v1, lane, big), axis=-1, keepdims=True)
    le2 = jnp.where(lane == i1, neg, le)
    v2 = jnp.max(le2, axis=-1, keepdims=True)
    i2 = jnp.min(jnp.where(le2 == v2, lane, big), axis=-1, keepdims=True)
    e2 = jnp.exp(v2 - v1)
    den = 1.0 + e2
    c1 = w_grp / den
    c2 = w_grp * e2 / den

    sel1 = lane == i1 - N_GROUPS
    sel2 = lane == i2 - N_GROUPS
    onehot = jnp.logical_or(sel1, sel2).astype(BF16)
    before = jnp.dot(tri_ref[...], onehot, preferred_element_type=F32) + run_ref[...]
    pos1 = jnp.sum(jnp.where(sel1, before, 0.0), axis=-1, keepdims=True)
    pos2 = jnp.sum(jnp.where(sel2, before, 0.0), axis=-1, keepdims=True)
    run_ref[...] += jnp.sum(onehot.astype(F32), axis=0, keepdims=True)
    cnt_ref[...] = jnp.broadcast_to(run_ref[...], cnt_ref.shape)

    cols = ((i1 - N_GROUPS).astype(F32), (i2 - N_GROUPS).astype(F32), c1, c2, pos1, pos2)
    rt = jnp.zeros_like(logits)
    for li, col in enumerate(cols):
        rt = jnp.where(lane == li, col, rt)
    rt_ref[...] = rt


def _outproj(o_gla, o_diff, proj, x2, mod, w_out, ln_w, ln_b, w_r, b_r, *, seq, tm, ga_col, alpha):
    t, d = x2.shape
    assert d == ROW_SUB * LANES
    kern = functools.partial(_outproj_kernel, alpha=alpha)
    row_spec = pl.BlockSpec((tm, d), lambda i: (i, 0))
    vec_spec = pl.BlockSpec((1, d), lambda i: (0, 0))
    return pl.pallas_call(
        kern,
        grid=(t // tm,),
        in_specs=[row_spec, row_spec,
                  pl.BlockSpec((tm, d), lambda i: (i, ga_col // d)),
                  pl.BlockSpec((tm, d), lambda i: (i, ga_col // d + 1)),
                  row_spec,
                  pl.BlockSpec((1,) + mod.shape[1:], lambda i: ((i * tm) // seq, 0, 0)),
                  pl.BlockSpec((d, d), lambda i: (0, 0)),
                  vec_spec, vec_spec,
                  pl.BlockSpec((d, LANES), lambda i: (0, 0)),
                  pl.BlockSpec((1, LANES), lambda i: (0, 0))],
        out_specs=[row_spec,
                   pl.BlockSpec((tm * ROW_SUB, LANES), lambda i: (i, 0)),
                   pl.BlockSpec((tm, LANES), lambda i: (i, 0)),
                   pl.BlockSpec((ROW_SUB, LANES), lambda i: (0, 0))],
        out_shape=[jax.ShapeDtypeStruct((t, d), F32),
                   jax.ShapeDtypeStruct((t * ROW_SUB, LANES), F32),
                   jax.ShapeDtypeStruct((t, LANES), F32),
                   jax.ShapeDtypeStruct((ROW_SUB, LANES), F32)],
        scratch_shapes=[pltpu.VMEM((tm, tm), BF16),
                        pltpu.VMEM((1, LANES), F32)],
        compiler_params=pltpu.CompilerParams(
            dimension_semantics=("arbitrary",), vmem_limit_bytes=VMEM_LIMIT),
        name="outproj",
    )(o_gla, o_diff, proj, proj, x2, mod, w_out, ln_w, ln_b, w_r, b_r)


def _expert_kernel(be_ref, tok_ref, tokn_ref, u_hbm, wg_ref, wu_ref, wd_ref, o_ref,
                   xbuf, wgb, wub, wdb, sem, *, blk):
    i = pl.program_id(0)
    n = pl.num_programs(0)
    slot = i % 2

    def start_gather(idx_ref, s):
        def body(r, carry):
            _tile_copy(u_hbm, xbuf.at[s], sem.at[s], idx_ref[0, 0, r], r).start()
            return carry
        lax.fori_loop(0, blk, body, 0, unroll=8)

    @pl.when(i == 0)
    def _():
        start_gather(tok_ref, 0)

    changed = jnp.logical_or(i == 0, be_ref[i] != be_ref[jnp.maximum(i - 1, 0)])

    @pl.when(changed)
    def _():
        wgb[...] = wg_ref[0].astype(BF16)
        wub[...] = wu_ref[0].astype(BF16)
        wdb[...] = wd_ref[0].astype(BF16)

    start_gather(tokn_ref, 1 - slot)
    _wait_tiles(u_hbm, xbuf.at[slot], sem.at[slot])
    xb = _load_row_tiles(xbuf.at[slot], blk).astype(BF16)
    hg = jnp.dot(xb, wgb[...], preferred_element_type=F32)
    hu = jnp.dot(xb, wub[...], preferred_element_type=F32)
    h = (_silu(hg) * hu).astype(BF16)
    _store_row_tiles(o_ref, jnp.dot(h, wdb[...], preferred_element_type=F32))

    @pl.when(i == n - 1)
    def _():
        _wait_tiles(u_hbm, xbuf.at[1 - slot], sem.at[1 - slot])


def _experts(block_expert, row_token3, u2_tiles, w_gate, w_up, w_down, *, blk):
    nblk = row_token3.shape[0]
    e, d, ff = w_gate.shape
    kern = functools.partial(_expert_kernel, blk=blk)
    grid_spec = pltpu.PrefetchScalarGridSpec(
        num_scalar_prefetch=1,
        grid=(nblk,),
        in_specs=[pl.BlockSpec((1, 1, blk), lambda i, be: (i, 0, 0), memory_space=pltpu.SMEM),
                  pl.BlockSpec((1, 1, blk), lambda i, be: (jnp.minimum(i + 1, nblk - 1), 0, 0),
                               memory_space=pltpu.SMEM),
                  pl.BlockSpec(memory_space=pl.ANY),
                  pl.BlockSpec((1, d, ff), lambda i, be: (be[i], 0, 0)),
                  pl.BlockSpec((1, d, ff), lambda i, be: (be[i], 0, 0)),
                  pl.BlockSpec((1, ff, d), lambda i, be: (be[i], 0, 0))],
        out_specs=pl.BlockSpec((blk * ROW_SUB, LANES), lambda i, be: (i, 0)),
        scratch_shapes=[pltpu.VMEM((2, blk * ROW_SUB, LANES), F32),
                        pltpu.VMEM((d, ff), BF16),
                        pltpu.VMEM((d, ff), BF16),
                        pltpu.VMEM((ff, d), BF16),
                        pltpu.SemaphoreType.DMA((2,))],
    )
    return pl.pallas_call(
        kern,
        grid_spec=grid_spec,
        out_shape=jax.ShapeDtypeStruct((nblk * blk * ROW_SUB, LANES), F32),
        compiler_params=pltpu.CompilerParams(
            dimension_semantics=("arbitrary",), vmem_limit_bytes=VMEM_LIMIT),
        name="experts",
    )(block_expert, row_token3, row_token3, u2_tiles, w_gate, w_up, w_down)


def _combine_kernel(d0_ref, d0n_ref, d1_ref, d1n_ref, rows_hbm, x1_ref, rt_ref, mod_ref, lnw_ref, lnb_ref,
                    o_ref, buf, sem, *, tm, alpha):
    i = pl.program_id(0)
    n = pl.num_programs(0)
    slot = i % 2

    def start_gather(i0_ref, i1_ref, s):
        def body(r, carry):
            _tile_copy(rows_hbm, buf.at[s, 0], sem.at[s], i0_ref[0, 0, r], r).start()
            _tile_copy(rows_hbm, buf.at[s, 1], sem.at[s], i1_ref[0, 0, r], r).start()
            return carry
        lax.fori_loop(0, tm, body, 0, unroll=4)

    def wait_gather(s):
        _wait_tiles(rows_hbm, buf.at[s, 0], sem.at[s])
        _wait_tiles(rows_hbm, buf.at[s, 1], sem.at[s])

    @pl.when(i == 0)
    def _():
        start_gather(d0_ref, d1_ref, 0)

    start_gather(d0n_ref, d1n_ref, 1 - slot)
    wait_gather(slot)
    rt = rt_ref[...]
    y = rt[:, 2:3] * _load_row_tiles(buf.at[slot, 0], tm) + rt[:, 3:4] * _load_row_tiles(buf.at[slot, 1], tm)
    gate2 = mod_ref[0, 5:6, :]
    z = alpha * x1_ref[...] + gate2 * y
    o_ref[...] = _layer_norm(z) * lnw_ref[...] + lnb_ref[...]

    @pl.when(i == n - 1)
    def _():
        wait_gather(1 - slot)


def _combine(dest0, dest1, rows_tiles, x1, route, mod, ln_w, ln_b, *, seq, tm, alpha):
    t, d = x1.shape
    nt = t // tm
    kern = functools.partial(_combine_kernel, tm=tm, alpha=alpha)
    cur = lambda i: (i, 0, 0)
    nxt = lambda i: (jnp.minimum(i + 1, nt - 1), 0, 0)
    idx_spec = lambda m: pl.BlockSpec((1, 1, tm), m, memory_space=pltpu.SMEM)
    row_spec = pl.BlockSpec((tm, d), lambda i: (i, 0))
    vec_spec = pl.BlockSpec((1, d), lambda i: (0, 0))
    return pl.pallas_call(
        kern,
        grid=(nt,),
        in_specs=[idx_spec(cur), idx_spec(nxt), idx_spec(cur), idx_spec(nxt),
                  pl.BlockSpec(memory_space=pl.ANY),
                  row_spec,
                  pl.BlockSpec((tm, LANES), lambda i: (i, 0)),
                  pl.BlockSpec((1,) + mod.shape[1:], lambda i: ((i * tm) // seq, 0, 0)),
                  vec_spec, vec_spec],
        out_specs=row_spec,
        out_shape=jax.ShapeDtypeStruct((t, d), F32),
        scratch_shapes=[pltpu.VMEM((2, 2, tm * ROW_SUB, LANES), F32),
                        pltpu.SemaphoreType.DMA((2,))],
        compiler_params=pltpu.CompilerParams(
            dimension_semantics=("arbitrary",), vmem_limit_bytes=VMEM_LIMIT),
        name="combine",
    )(dest0, dest0, dest1, dest1, rows_tiles, x1, route, mod, ln_w, ln_b)


def _dispatch_plan(route, counts, blk):
    t = route.shape[0]
    eid = route[:, :2].astype(jnp.int32)
    pos = route[:, 4:6].astype(jnp.int32)
    counts = counts[:N_EXPERTS].astype(jnp.int32)
    pcounts = ((counts + blk - 1) // blk) * blk
    pend = jnp.cumsum(pcounts)
    pstart = pend - pcounts
    experts = jnp.arange(N_EXPERTS, dtype=jnp.int32)
    dest = jnp.sum(jnp.where(eid[:, :, None] == experts, pstart, 0), axis=-1) + pos
    tk = 2 * t
    p_rows = ((tk + N_EXPERTS * (blk - 1) + blk - 1) // blk) * blk
    nblk = p_rows // blk
    tok = jnp.broadcast_to(jnp.arange(t, dtype=jnp.int32)[:, None], (t, 2))
    row_token = jnp.zeros((p_rows,), jnp.int32).at[dest.reshape(-1)].set(
        tok.reshape(-1), unique_indices=True, mode="promise_in_bounds")
    block_start = jnp.arange(nblk, dtype=jnp.int32) * blk
    block_expert = jnp.minimum(jnp.sum((pend[None, :] <= block_start[:, None]).astype(jnp.int32), axis=1),
                               N_EXPERTS - 1)
    return row_token, block_expert, dest[:, 0], dest[:, 1]


def _layer(x, c, positions, w_ada, b_ada, w_in, w_g2, b_g2, gla_nw, lq1, lk1, lq2, lk2, diff_nw, w_out,
           ln1_w, ln1_b, w_rg, b_rg, w_re, b_re, w_eg, w_eu, w_ed, ln2_w, ln2_b, *, lam_init,
           tm_in, tn_in, gla_rows, bq, bk, tm_out, moe_blk, tm_cmb):
    bsz, seq, d = x.shape
    t = bsz * seq
    alpha = (2.0 * DEPTH) ** 0.25
    gla_dk = d // (2 * GLA_HEADS)
    gla_dv = d // GLA_HEADS
    dh = d // (2 * DIFF_HEADS)
    hk = GLA_HEADS * gla_dk
    hv = GLA_HEADS * gla_dv
    dq = DIFF_HEADS * 2 * dh
    gr_col = 2 * hk + 2 * hv
    q_col, k_col = 0, hk
    v_col, g_col = 2 * hk, 2 * hk + hv
    dq_col = gr_col
    dk_col = dq_col + dq
    dv_col = dk_col + dq
    mg_col = dv_col + dq
    w_main = jnp.concatenate([w_in[:, :gr_col], w_in[:, gr_col + GLA_GATE_RANK:]], axis=1).astype(BF16)
    w_gr = jnp.pad(w_in[:, gr_col:gr_col + GLA_GATE_RANK], ((0, 0), (0, LANES - GLA_GATE_RANK))).astype(BF16)
    w_g2p = jnp.pad(w_g2, ((0, LANES - GLA_GATE_RANK), (0, 0)))
    x2 = x.reshape(t, d)

    ada = _ada(c, w_ada, b_ada.reshape(1, -1))
    mod = ada.reshape(bsz, 6, d)

    half = dh // 2
    inv = ROPE_THETA ** (-jnp.arange(half, dtype=F32) / half)
    inv_row = jnp.tile(inv, LANES // half).reshape(1, LANES)
    cs = _rope_tables(positions.reshape(t, 1), inv_row, tm_in)

    proj, log_g = _inproj(x2, mod, cs, w_main, w_gr, w_g2p, b_g2.reshape(1, -1), seq=seq, tm=tm_in, tn=tn_in,
                          q_tile=dq_col // tn_in, k_tile=dk_col // tn_in, q_scale=dh ** -0.5 * LOG2_E)

    o_gla = _gla(proj, log_g, gla_nw.reshape(1, -1), bsz=bsz, seq=seq, rows=gla_rows, dk=gla_dk, dv=gla_dv,
                 q_col=q_col, k_col=k_col, v_col=v_col, g_col=g_col)

    lam_vecs = jnp.pad(jnp.stack([lq1, lk1, lq2, lk2]), ((0, 4), (0, LANES - dh)))
    o_diff = _diff(lam_vecs, proj, diff_nw.reshape(1, -1), bsz=bsz, seq=seq, bq=bq, bk=bk, dh=dh,
                   q_col=dq_col, k_col=dk_col, v_col=dv_col, lam_init=lam_init)

    w_r = jnp.pad(jnp.concatenate([w_rg, w_re], axis=1), ((0, 0), (0, LANES - N_GROUPS - N_EXPERTS)))
    b_r = jnp.pad(jnp.concatenate([b_rg, b_re]), (0, LANES - N_GROUPS - N_EXPERTS)).reshape(1, LANES)
    x1, u2_tiles, route, counts = _outproj(o_gla, o_diff, proj, x2, mod, w_out.astype(BF16), ln1_w.reshape(1, -1),
                                           ln1_b.reshape(1, -1), w_r, b_r, seq=seq, tm=tm_out, ga_col=mg_col,
                                           alpha=alpha)

    row_token, block_expert, dest0, dest1 = _dispatch_plan(route, counts[0], moe_blk)
    rows = _experts(block_expert, row_token.reshape(-1, 1, moe_blk), u2_tiles, w_eg, w_eu, w_ed, blk=moe_blk)
    out = _combine(dest0.reshape(-1, 1, tm_cmb), dest1.reshape(-1, 1, tm_cmb), rows, x1, route, mod,
                   ln2_w.reshape(1, -1), ln2_b.reshape(1, -1), seq=seq, tm=tm_cmb, alpha=alpha)
    return out.reshape(bsz, seq, d)


def kernel(x, c, positions, w_ada, b_ada, w_in, w_gla_gate2, b_gla_gate2, gla_norm_w, diff_lambda_q1,
           diff_lambda_k1, diff_lambda_q2, diff_lambda_k2, diff_norm_w, w_out, ln1_w, ln1_b, w_router_group,
           b_router_group, w_router_expert, b_router_expert, w_exp_gate, w_exp_up, w_exp_down, ln2_w, ln2_b):
    assert w_ada.shape[0] == DEPTH
    for l in range(DEPTH):
        lam_init = 0.8 - 0.6 * math.exp(-0.3 * l)
        x = _layer(x, c, positions, w_ada[l], b_ada[l], w_in[l], w_gla_gate2[l], b_gla_gate2[l], gla_norm_w[l],
                   diff_lambda_q1[l], diff_lambda_k1[l], diff_lambda_q2[l], diff_lambda_k2[l], diff_norm_w[l],
                   w_out[l], ln1_w[l], ln1_b[l], w_router_group[l], b_router_group[l], w_router_expert[l],
                   b_router_expert[l], w_exp_gate[l], w_exp_up[l], w_exp_down[l], ln2_w[l], ln2_b[l],
                   lam_init=lam_init, tm_in=1024, tn_in=1024, gla_rows=512, bq=512, bk=512, tm_out=512,
                   moe_blk=256, tm_cmb=256)
    return x
```

```python
import functools
import math

import jax
import jax.numpy as jnp
from jax import lax
from jax.experimental import pallas as pl
from jax.experimental.pallas import tpu as pltpu

F32 = jnp.float32
BF16 = jnp.bfloat16
HIGHEST = lax.Precision.HIGHEST

DEPTH = 1
GLA_HEADS = 4
GLA_GATE_RANK = 16
GLA_TAU = 16.0
GLA_CHUNK = 64
DIFF_HEADS = 8
ROPE_THETA = 10000.0
N_GROUPS = 4
EXPERTS_PER_GROUP = 8
N_EXPERTS = N_GROUPS * EXPERTS_PER_GROUP
LN_EPS = 1e-5
RMS_EPS = 1e-6
LOG2_E = math.log2(math.e)
LANES = 128
VMEM_LIMIT = 56 * 1024 * 1024

NT_DIMS = (((1,), (1,)), ((), ()))
TN_DIMS = (((0,), (0,)), ((), ()))


def _layer_norm(x):
    mu = jnp.mean(x, axis=-1, keepdims=True)
    xc = x - mu
    var = jnp.mean(xc * xc, axis=-1, keepdims=True)
    return xc * lax.rsqrt(var + LN_EPS)


def _silu(x):
    return x * jax.nn.sigmoid(x)


def _ada_kernel(c_ref, w_ref, b_ref, o_ref):
    s = _silu(c_ref[...])
    o_ref[...] = jnp.dot(s, w_ref[...], preferred_element_type=F32, precision=HIGHEST) + b_ref[...]


def _ada(c, w, b):
    bsz, d = c.shape
    n = w.shape[1]
    tn = d
    return pl.pallas_call(
        _ada_kernel,
        grid=(n // tn,),
        in_specs=[pl.BlockSpec((bsz, d), lambda j: (0, 0)),
                  pl.BlockSpec((d, tn), lambda j: (0, j)),
                  pl.BlockSpec((1, tn), lambda j: (0, j))],
        out_specs=pl.BlockSpec((bsz, tn), lambda j: (0, j)),
        out_shape=jax.ShapeDtypeStruct((bsz, n), F32),
        name="ada",
    )(c, w, b)


def _rope_kernel(pos_ref, inv_ref, cs_ref):
    ang = pos_ref[...].astype(F32) * inv_ref[...]
    lane = lax.broadcasted_iota(jnp.int32, ang.shape, 1)
    first_half = (lane % 64) < 32
    sin = jnp.sin(ang)
    cs_ref[:, :LANES] = jnp.cos(ang)
    cs_ref[:, LANES:] = jnp.where(first_half, -sin, sin)


def _rope_tables(pos_col, inv_row, tm):
    t = pos_col.shape[0]
    return pl.pallas_call(
        _rope_kernel,
        grid=(t // tm,),
        in_specs=[pl.BlockSpec((tm, 1), lambda i: (i, 0)),
                  pl.BlockSpec((1, LANES), lambda i: (0, 0))],
        out_specs=pl.BlockSpec((tm, 2 * LANES), lambda i: (i, 0)),
        out_shape=jax.ShapeDtypeStruct((t, 2 * LANES), F32),
        name="rope_tables",
    )(pos_col, inv_row)


def _inproj_kernel(x_ref, mod_ref, cs_ref, w_ref, wgr_ref, wg2_ref, bg2_ref, o_ref, lg_ref, u_ref,
                   *, q_tile, k_tile, q_scale):
    j = pl.program_id(1)

    @pl.when(j == 0)
    def _():
        shift = mod_ref[0, 0:1, :]
        scale = mod_ref[0, 1:2, :]
        u = (_layer_norm(x_ref[...]) * (1.0 + scale) + shift).astype(BF16)
        u_ref[...] = u
        gr = jnp.dot(u, wgr_ref[...], preferred_element_type=F32)
        z = jnp.dot(gr, wg2_ref[...], preferred_element_type=F32, precision=HIGHEST) + bg2_ref[...]
        log_sig = jnp.minimum(z, 0.0) - jnp.log(1.0 + jnp.exp(-jnp.abs(z)))
        lg_ref[...] = log_sig * (1.0 / GLA_TAU)

    acc = jnp.dot(u_ref[...], w_ref[...], preferred_element_type=F32)
    is_rope = jnp.logical_or(j == q_tile, j == k_tile)

    @pl.when(is_rope)
    def _():
        tn = acc.shape[1]
        sc = jnp.where(j == q_tile, q_scale, 1.0).astype(F32)
        cos = cs_ref[:, :LANES] * sc
        sin = cs_ref[:, LANES:] * sc
        lane = lax.broadcasted_iota(jnp.int32, cos.shape, 1)
        first_half = (lane % 64) < 32
        for h in range(tn // LANES):
            t = acc[:, h * LANES:(h + 1) * LANES]
            partner = jnp.where(first_half, pltpu.roll(t, LANES - 32, 1), pltpu.roll(t, 32, 1))
            o_ref[:, h * LANES:(h + 1) * LANES] = (t * cos + partner * sin).astype(o_ref.dtype)

    @pl.when(jnp.logical_not(is_rope))
    def _():
        o_ref[...] = acc.astype(o_ref.dtype)


def _inproj(x2, mod, cs, w_main, w_gr, w_g2, b_g2, *, seq, tm, tn, q_tile, k_tile, q_scale):
    t, d = x2.shape
    n = w_main.shape[1]
    ng = w_g2.shape[1]
    kern = functools.partial(_inproj_kernel, q_tile=q_tile, k_tile=k_tile, q_scale=q_scale)
    return pl.pallas_call(
        kern,
        grid=(t // tm, n // tn),
        in_specs=[pl.BlockSpec((tm, d), lambda i, j: (i, 0)),
                  pl.BlockSpec((1,) + mod.shape[1:], lambda i, j: ((i * tm) // seq, 0, 0)),
                  pl.BlockSpec((tm, 2 * LANES), lambda i, j: (i, 0)),
                  pl.BlockSpec((d, tn), lambda i, j: (0, j)),
                  pl.BlockSpec((d, LANES), lambda i, j: (0, 0)),
                  pl.BlockSpec((LANES, ng), lambda i, j: (0, 0)),
                  pl.BlockSpec((1, ng), lambda i, j: (0, 0))],
        out_specs=[pl.BlockSpec((tm, tn), lambda i, j: (i, j)),
                   pl.BlockSpec((tm, ng), lambda i, j: (i, 0))],
        out_shape=[jax.ShapeDtypeStruct((t, n), BF16),
                   jax.ShapeDtypeStruct((t, ng), F32)],
        scratch_shapes=[pltpu.VMEM((tm, d), BF16)],
        compiler_params=pltpu.CompilerParams(
            dimension_semantics=("arbitrary", "arbitrary"), vmem_limit_bytes=VMEM_LIMIT),
        name="inproj",
    )(x2, mod, cs, w_main, w_gr, w_g2, b_g2)


def _gla_kernel(q_ref, k_ref, v_ref, g_ref, lg_ref, nw_ref, o_ref, st_ref, *, chunk, q_scale):
    @pl.when(pl.program_id(2) == 0)
    def _():
        st_ref[...] = jnp.zeros_like(st_ref)

    rows = q_ref.shape[0]
    n_chunks = rows // chunk

    b = lg_ref[...]
    pos = lax.broadcasted_iota(jnp.int32, b.shape, 0) % chunk
    step = 1
    while step < chunk:
        b = b + jnp.where(pos >= step, pltpu.roll(b, step, 0), 0.0)
        step *= 2

    k = k_ref[...].astype(F32)
    q_t = (q_ref[...].astype(F32) * q_scale * jnp.exp(b)).astype(BF16)
    k_t = (k * jnp.exp(-b)).astype(BF16)
    v = v_ref[...]

    r = lax.broadcasted_iota(jnp.int32, (rows, rows), 0)
    c = lax.broadcasted_iota(jnp.int32, (rows, rows), 1)
    keep = jnp.logical_and(c <= r, (r // chunk) == (c // chunk))
    attn = lax.dot_general(q_t, k_t, NT_DIMS, preferred_element_type=F32)
    o = jnp.dot(jnp.where(keep, attn, 0.0).astype(BF16), v, preferred_element_type=F32)

    st = st_ref[...]
    o_inter = []
    for ci in range(n_chunks):
        sl = slice(ci * chunk, (ci + 1) * chunk)
        b_c = b[sl, :]
        b_last = b_c[chunk - 1:chunk, :]
        k_d = (k[sl, :] * jnp.exp(b_last - b_c)).astype(BF16)
        kv = lax.dot_general(v[sl, :], k_d, TN_DIMS, preferred_element_type=F32)
        o_inter.append(lax.dot_general(q_t[sl, :], st.astype(BF16), NT_DIMS, preferred_element_type=F32))
        st = st * jnp.exp(b_last) + kv
    st_ref[...] = st
    o = o + jnp.concatenate(o_inter, axis=0)

    ms = jnp.mean(o * o, axis=-1, keepdims=True)
    y = o * lax.rsqrt(ms + RMS_EPS) * nw_ref[...] * _silu(g_ref[...].astype(F32))
    o_ref[...] = y.astype(o_ref.dtype)


def _gla(proj, log_g, norm_w, *, bsz, seq, rows, dk, dv, q_col, k_col, v_col, g_col):
    t = proj.shape[0]
    nl = seq // rows
    kern = functools.partial(_gla_kernel, chunk=GLA_CHUNK, q_scale=dk ** -0.5)
    row = lambda b, h, l: b * nl + l
    return pl.pallas_call(
        kern,
        grid=(bsz, GLA_HEADS, nl),
        in_specs=[pl.BlockSpec((rows, dk), lambda b, h, l: (row(b, h, l), q_col // dk + h)),
                  pl.BlockSpec((rows, dk), lambda b, h, l: (row(b, h, l), k_col // dk + h)),
                  pl.BlockSpec((rows, dv), lambda b, h, l: (row(b, h, l), v_col // dv + h)),
                  pl.BlockSpec((rows, dv), lambda b, h, l: (row(b, h, l), g_col // dv + h)),
                  pl.BlockSpec((rows, dk), lambda b, h, l: (row(b, h, l), h)),
                  pl.BlockSpec((1, dv), lambda b, h, l: (0, 0))],
        out_specs=pl.BlockSpec((rows, dv), lambda b, h, l: (row(b, h, l), h)),
        out_shape=jax.ShapeDtypeStruct((t, GLA_HEADS * dv), BF16),
        scratch_shapes=[pltpu.VMEM((dv, dk), F32)],
        compiler_params=pltpu.CompilerParams(
            dimension_semantics=("arbitrary", "arbitrary", "arbitrary"), vmem_limit_bytes=VMEM_LIMIT),
        name="gla",
    )(proj, proj, proj, proj, log_g, norm_w)


def _diff_kernel(lam_ref, q_ref, k_ref, v_ref, nw_ref, o_ref, qs_ref, vt_ref, sa_ref, sb_ref, m_ref, l_ref, acc_ref,
                 *, bq, bk, dh, lam_init):
    qi = pl.program_id(2)
    q0 = qi * bq
    seq = k_ref.shape[0]

    @pl.when(qi == 0)
    def _():
        for c in range(seq // bk):
            vt_ref[:, c * bk:(c + 1) * bk] = v_ref[c * bk:(c + 1) * bk, :].astype(F32).T.astype(BF16)

    q = q_ref[...]
    lane = lax.broadcasted_iota(jnp.int32, q.shape, 1)
    zero = jnp.zeros_like(q)
    qs_ref[:bq, :] = jnp.where(lane < dh, q, zero)
    qs_ref[bq:, :] = jnp.where(lane >= dh, q, zero)
    m_ref[...] = jnp.full_like(m_ref, -jnp.inf)
    l_ref[...] = jnp.zeros_like(l_ref)
    acc_ref[...] = jnp.zeros_like(acc_ref)

    def scores(kj, dst):
        k0 = pl.multiple_of(kj * bk, bk)
        dst[...] = lax.dot_general(k_ref[pl.ds(k0, bk), :], qs_ref[...], NT_DIMS, preferred_element_type=F32)

    def update(src, kj, masked):
        k0 = pl.multiple_of(kj * bk, bk)
        s = src[...]
        if masked:
            key = k0 + lax.broadcasted_iota(jnp.int32, s.shape, 0)
            col = lax.broadcasted_iota(jnp.int32, s.shape, 1)
            qpos = q0 + jnp.where(col >= bq, col - bq, col)
            s = jnp.where(key <= qpos, s, -jnp.inf)
        m_prev = m_ref[...]
        m_new = jnp.maximum(m_prev, jnp.max(s, axis=0, keepdims=True))
        alpha = jnp.exp2(m_prev - m_new)
        p = jnp.exp2(s - m_new)
        l_ref[...] = alpha * l_ref[...] + jnp.sum(p, axis=0, keepdims=True)
        acc_ref[...] = alpha * acc_ref[...] + jnp.dot(
            vt_ref[:, pl.ds(k0, bk)], p.astype(BF16), preferred_element_type=F32)
        m_ref[...] = m_new

    n_full = (q0 + 1) // bk
    scores(0, sa_ref)

    def pair_body(jj, carry):
        j = 2 * jj
        scores(j + 1, sb_ref)
        update(sa_ref, j, False)
        scores(j + 2, sa_ref)
        update(sb_ref, j + 1, False)
        return carry

    lax.fori_loop(0, n_full // 2, pair_body, 0)
    odd = (n_full % 2) == 1

    @pl.when(odd)
    def _():
        scores(n_full, sb_ref)
        update(sa_ref, n_full - 1, False)
        update(sb_ref, n_full, True)

    @pl.when(jnp.logical_not(odd))
    def _():
        update(sa_ref, n_full, True)

    lv = lam_ref[...]
    s1 = jnp.sum(lv[0:1, :] * lv[1:2, :], axis=-1, keepdims=True)
    s2 = jnp.sum(lv[2:3, :] * lv[3:4, :], axis=-1, keepdims=True)
    lam = jnp.exp(s1) - jnp.exp(s2) + lam_init
    o_all = acc_ref[...] / l_ref[...]
    o_t = o_all[:, :bq] - lam * o_all[:, bq:]
    ms = jnp.mean(o_t * o_t, axis=0, keepdims=True)
    y = (o_t * lax.rsqrt(ms + RMS_EPS)).T * nw_ref[...] * (1.0 - lam_init)
    o_ref[...] = y.astype(o_ref.dtype)


def _diff(lam_vecs, proj, norm_w, *, bsz, seq, bq, bk, dh, q_col, k_col, v_col, lam_init):
    assert bk % bq == 0 and seq % bk == 0
    t = proj.shape[0]
    w = 2 * dh
    nq = seq // bq
    kern = functools.partial(_diff_kernel, bq=bq, bk=bk, dh=dh, lam_init=lam_init)
    return pl.pallas_call(
        kern,
        grid=(bsz, DIFF_HEADS, nq),
        in_specs=[pl.BlockSpec(lam_vecs.shape, lambda b, h, i: (0, 0)),
                  pl.BlockSpec((bq, w), lambda b, h, i: (b * nq + i, q_col // w + h)),
                  pl.BlockSpec((seq, w), lambda b, h, i: (b, k_col // w + h)),
                  pl.BlockSpec((seq, w), lambda b, h, i: (b, v_col // w + h)),
                  pl.BlockSpec((1, w), lambda b, h, i: (0, 0))],
        out_specs=pl.BlockSpec((bq, w), lambda b, h, i: (b * nq + i, h)),
        out_shape=jax.ShapeDtypeStruct((t, DIFF_HEADS * w), BF16),
        scratch_shapes=[pltpu.VMEM((2 * bq, w), BF16),
                        pltpu.VMEM((w, seq), BF16),
                        pltpu.VMEM((bk, 2 * bq), F32),
                        pltpu.VMEM((bk, 2 * bq), F32),
                        pltpu.VMEM((1, 2 * bq), F32),
                        pltpu.VMEM((1, 2 * bq), F32),
                        pltpu.VMEM((w, 2 * bq), F32)],
        compiler_params=pltpu.CompilerParams(
            dimension_semantics=("arbitrary", "arbitrary", "arbitrary"), vmem_limit_bytes=VMEM_LIMIT),
        name="diff_attn",
    )(lam_vecs, proj, proj, proj, norm_w)


ROW_SUB = 8


def _store_row_tiles(ref, val):
    rows = val.shape[0]
    for s in range(ROW_SUB):
        ref[pl.ds(s, rows, stride=ROW_SUB), :] = val[:, s * LANES:(s + 1) * LANES]


def _load_row_tiles(ref, rows):
    return jnp.concatenate([ref[pl.ds(s, rows, stride=ROW_SUB), :] for s in range(ROW_SUB)], axis=1)


def _tile_copy(src_hbm, dst_vmem, sem, src_row, dst_row):
    return pltpu.make_async_copy(src_hbm.at[pl.ds(src_row * ROW_SUB, ROW_SUB), :],
                                 dst_vmem.at[pl.ds(dst_row * ROW_SUB, ROW_SUB), :], sem)


def _wait_tiles(src_hbm, dst_vmem, sem):
    pltpu.make_async_copy(src_hbm.at[pl.ds(0, dst_vmem.shape[0]), :], dst_vmem, sem).wait()


def _outproj_kernel(og_ref, od_ref, ga_ref, gb_ref, x_ref, mod_ref, wo_ref, lnw_ref, lnb_ref, wr_ref, br_ref,
                    x1_ref, u2_ref, rt_ref, cnt_ref, tri_ref, run_ref, *, alpha):
    @pl.when(pl.program_id(0) == 0)
    def _():
        r = lax.broadcasted_iota(jnp.int32, tri_ref.shape, 0)
        c = lax.broadcasted_iota(jnp.int32, tri_ref.shape, 1)
        tri_ref[...] = (c < r).astype(BF16)
        run_ref[...] = jnp.zeros_like(run_ref)

    g_a = jax.nn.sigmoid(ga_ref[...].astype(F32))
    g_b = jax.nn.sigmoid(gb_ref[...].astype(F32))
    merged = (g_a * og_ref[...].astype(F32) + g_b * od_ref[...].astype(F32)).astype(BF16)
    y = jnp.dot(merged, wo_ref[...], preferred_element_type=F32)
    gate1 = mod_ref[0, 2:3, :]
    shift2 = mod_ref[0, 3:4, :]
    scale2 = mod_ref[0, 4:5, :]
    x1 = _layer_norm(alpha * x_ref[...] + gate1 * y) * lnw_ref[...] + lnb_ref[...]
    x1_ref[...] = x1
    u2 = _layer_norm(x1) * (1.0 + scale2) + shift2
    _store_row_tiles(u2_ref, u2)

    logits = jnp.dot(u2, wr_ref[...], preferred_element_type=F32, precision=HIGHEST) + br_ref[...]
    lane = lax.broadcasted_iota(jnp.int32, logits.shape, 1)
    neg = jnp.float32(-jnp.inf)
    big = jnp.int32(LANES)
    lg = jnp.where(lane < N_GROUPS, logits, neg)
    g_max = jnp.max(lg, axis=-1, keepdims=True)
    w_grp = 1.0 / jnp.sum(jnp.exp(lg - g_max), axis=-1, keepdims=True)
    g_idx = jnp.min(jnp.where(lg == g_max, lane, big), axis=-1, keepdims=True)
    lo = N_GROUPS + EXPERTS_PER_GROUP * g_idx
    le = jnp.where(jnp.logical_and(lane >= lo, lane < lo + EXPERTS_PER_GROUP), logits, neg)
    v1 = jnp.max(le, axis=-1, keepdims=True)
    i1 = jnp.min(jnp.where(le == v1, lane, big), axis=-1, keepdims=True)
    le2 = jnp.where(lane == i1, neg, le)
    v2 = jnp.max(le2, axis=-1, keepdims=True)
    i2 = jnp.min(jnp.where(le2 == v2, lane, big), axis=-1, keepdims=True)
    e2 = jnp.exp(v2 - v1)
    den = 1.0 + e2
    c1 = w_grp / den
    c2 = w_grp * e2 / den

    sel1 = lane == i1 - N_GROUPS
    sel2 = lane == i2 - N_GROUPS
    onehot = jnp.logical_or(sel1, sel2).astype(BF16)
    before = jnp.dot(tri_ref[...], onehot, preferred_element_type=F32) + run_ref[...]
    pos1 = jnp.sum(jnp.where(sel1, before, 0.0), axis=-1, keepdims=True)
    pos2 = jnp.sum(jnp.where(sel2, before, 0.0), axis=-1, keepdims=True)
    run_ref[...] += jnp.sum(onehot.astype(F32), axis=0, keepdims=True)
    cnt_ref[...] = jnp.broadcast_to(run_ref[...], cnt_ref.shape)

    cols = ((i1 - N_GROUPS).astype(F32), (i2 - N_GROUPS).astype(F32), c1, c2, pos1, pos2)
    rt = jnp.zeros_like(logits)
    for li, col in enumerate(cols):
        rt = jnp.where(lane == li, col, rt)
    rt_ref[...] = rt


def _outproj(o_gla, o_diff, proj, x2, mod, w_out, ln_w, ln_b, w_r, b_r, *, seq, tm, ga_col, alpha):
    t, d = x2.shape
    assert d == ROW_SUB * LANES
    kern = functools.partial(_outproj_kernel, alpha=alpha)
    row_spec = pl.BlockSpec((tm, d), lambda i: (i, 0))
    vec_spec = pl.BlockSpec((1, d), lambda i: (0, 0))
    return pl.pallas_call(
        kern,
        grid=(t // tm,),
        in_specs=[row_spec, row_spec,
                  pl.BlockSpec((tm, d), lambda i: (i, ga_col // d)),
                  pl.BlockSpec((tm, d), lambda i: (i, ga_col // d + 1)),
                  row_spec,
                  pl.BlockSpec((1,) + mod.shape[1:], lambda i: ((i * tm) // seq, 0, 0)),
                  pl.BlockSpec((d, d), lambda i: (0, 0)),
                  vec_spec, vec_spec,
                  pl.BlockSpec((d, LANES), lambda i: (0, 0)),
                  pl.BlockSpec((1, LANES), lambda i: (0, 0))],
        out_specs=[row_spec,
                   pl.BlockSpec((tm * ROW_SUB, LANES), lambda i: (i, 0)),
                   pl.BlockSpec((tm, LANES), lambda i: (i, 0)),
                   pl.BlockSpec((ROW_SUB, LANES), lambda i: (0, 0))],
        out_shape=[jax.ShapeDtypeStruct((t, d), F32),
                   jax.ShapeDtypeStruct((t * ROW_SUB, LANES), F32),
                   jax.ShapeDtypeStruct((t, LANES), F32),
                   jax.ShapeDtypeStruct((ROW_SUB, LANES), F32)],
        scratch_shapes=[pltpu.VMEM((tm, tm), BF16),
                        pltpu.VMEM((1, LANES), F32)],
        compiler_params=pltpu.CompilerParams(
            dimension_semantics=("arbitrary",), vmem_limit_bytes=VMEM_LIMIT),
        name="outproj",
    )(o_gla, o_diff, proj, proj, x2, mod, w_out, ln_w, ln_b, w_r, b_r)


def _dispatch_kernel(fill_ref, d0_ref, d1_ref, u_hbm, xs_hbm, zbuf, sem_fill, sem, *, tm, blk):
    i = pl.program_id(0)

    @pl.when(i == 0)
    def _():
        zbuf[...] = jnp.zeros_like(zbuf)
        n_fill = fill_ref.shape[0]

        def fill_copy(j):
            return pltpu.make_async_copy(zbuf, xs_hbm.at[pl.ds(fill_ref[j] * ROW_SUB, blk * ROW_SUB), :], sem_fill)

        def start_fill(j, carry):
            @pl.when(fill_ref[j] >= 0)
            def _():
                fill_copy(j).start()
            return carry

        def wait_fill(j, carry):
            @pl.when(fill_ref[j] >= 0)
            def _():
                fill_copy(j).wait()
            return carry

        lax.fori_loop(0, n_fill, start_fill, 0)
        lax.fori_loop(0, n_fill, wait_fill, 0)

    base = i * tm

    def row_copy(dest_ref, r):
        return pltpu.make_async_copy(u_hbm.at[pl.ds((base + r) * ROW_SUB, ROW_SUB), :],
                                     xs_hbm.at[pl.ds(dest_ref[0, 0, r] * ROW_SUB, ROW_SUB), :], sem)

    def body(r, carry):
        row_copy(d0_ref, r).start(priority=0)
        row_copy(d1_ref, r).start(priority=1)
        return carry

    lax.fori_loop(0, tm, body, 0, unroll=4)
    for _ in range(2):
        pltpu.make_async_copy(u_hbm.at[pl.ds(0, tm * ROW_SUB), :], xs_hbm.at[pl.ds(0, tm * ROW_SUB), :], sem).wait()


def _dispatch(fill_start, dest0, dest1, u2_tiles, *, p_rows, tm, blk):
    t = u2_tiles.shape[0] // ROW_SUB
    kern = functools.partial(_dispatch_kernel, tm=tm, blk=blk)
    idx_spec = pl.BlockSpec((1, 1, tm), lambda i, fs: (i, 0, 0), memory_space=pltpu.SMEM)
    grid_spec = pltpu.PrefetchScalarGridSpec(
        num_scalar_prefetch=1,
        grid=(t // tm,),
        in_specs=[idx_spec, idx_spec, pl.BlockSpec(memory_space=pl.ANY)],
        out_specs=pl.BlockSpec(memory_space=pl.ANY),
        scratch_shapes=[pltpu.VMEM((blk * ROW_SUB, LANES), F32),
                        pltpu.SemaphoreType.DMA(()),
                        pltpu.SemaphoreType.DMA(())],
    )
    return pl.pallas_call(
        kern,
        grid_spec=grid_spec,
        out_shape=jax.ShapeDtypeStruct((p_rows * ROW_SUB, LANES), F32),
        compiler_params=pltpu.CompilerParams(dimension_semantics=("arbitrary",)),
        name="dispatch",
    )(fill_start, dest0, dest1, u2_tiles)


def _expert_kernel(be_ref, na_ref, x_ref, wg_ref, wu_ref, wd_ref, o_ref, wgb, wub, wdb, *, blk):
    i = pl.program_id(0)
    active = i < na_ref[0]

    @pl.when(active)
    def _():
        changed = jnp.logical_or(i == 0, be_ref[i] != be_ref[jnp.maximum(i - 1, 0)])

        @pl.when(changed)
        def _():
            wgb[...] = wg_ref[0].astype(BF16)
            wub[...] = wu_ref[0].astype(BF16)
            wdb[...] = wd_ref[0].astype(BF16)

        xb = _load_row_tiles(x_ref, blk).astype(BF16)
        hg = jnp.dot(xb, wgb[...], preferred_element_type=F32)
        hu = jnp.dot(xb, wub[...], preferred_element_type=F32)
        h = (_silu(hg) * hu).astype(BF16)
        _store_row_tiles(o_ref, jnp.dot(h, wdb[...], preferred_element_type=F32))

    @pl.when(jnp.logical_not(active))
    def _():
        o_ref[...] = jnp.zeros_like(o_ref)


def _experts(block_expert, n_active, xs_tiles, w_gate, w_up, w_down, *, blk):
    nblk = block_expert.shape[0]
    e, d, ff = w_gate.shape
    kern = functools.partial(_expert_kernel, blk=blk)
    grid_spec = pltpu.PrefetchScalarGridSpec(
        num_scalar_prefetch=2,
        grid=(nblk,),
        in_specs=[pl.BlockSpec((blk * ROW_SUB, LANES), lambda i, be, na: (jnp.minimum(i, na[0] - 1), 0)),
                  pl.BlockSpec((1, d, ff), lambda i, be, na: (be[i], 0, 0)),
                  pl.BlockSpec((1, d, ff), lambda i, be, na: (be[i], 0, 0)),
                  pl.BlockSpec((1, ff, d), lambda i, be, na: (be[i], 0, 0))],
        out_specs=pl.BlockSpec((blk * ROW_SUB, LANES), lambda i, be, na: (i, 0)),
        scratch_shapes=[pltpu.VMEM((d, ff), BF16),
                        pltpu.VMEM((d, ff), BF16),
                        pltpu.VMEM((ff, d), BF16)],
    )
    return pl.pallas_call(
        kern,
        grid_spec=grid_spec,
        out_shape=jax.ShapeDtypeStruct((nblk * blk * ROW_SUB, LANES), F32),
        compiler_params=pltpu.CompilerParams(
            dimension_semantics=("arbitrary",), vmem_limit_bytes=VMEM_LIMIT),
        name="experts",
    )(block_expert, n_active, xs_tiles, w_gate, w_up, w_down)


def _combine_kernel(d0_ref, d0n_ref, d1_ref, d1n_ref, rows_hbm, x1_ref, rt_ref, mod_ref, lnw_ref, lnb_ref,
                    o_ref, buf, sem, *, tm, alpha):
    i = pl.program_id(0)
    n = pl.num_programs(0)
    slot = i % 2

    def start_gather(i0_ref, i1_ref, s):
        def body(r, carry):
            _tile_copy(rows_hbm, buf.at[s, 0], sem.at[s], i0_ref[0, 0, r], r).start(priority=0)
            _tile_copy(rows_hbm, buf.at[s, 1], sem.at[s], i1_ref[0, 0, r], r).start(priority=1)
            return carry
        lax.fori_loop(0, tm, body, 0, unroll=4)

    def wait_gather(s):
        _wait_tiles(rows_hbm, buf.at[s, 0], sem.at[s])
        _wait_tiles(rows_hbm, buf.at[s, 1], sem.at[s])

    @pl.when(i == 0)
    def _():
        start_gather(d0_ref, d1_ref, 0)

    start_gather(d0n_ref, d1n_ref, 1 - slot)
    wait_gather(slot)
    rt = rt_ref[...]
    y = rt[:, 2:3] * _load_row_tiles(buf.at[slot, 0], tm) + rt[:, 3:4] * _load_row_tiles(buf.at[slot, 1], tm)
    gate2 = mod_ref[0, 5:6, :]
    z = alpha * x1_ref[...] + gate2 * y
    o_ref[...] = _layer_norm(z) * lnw_ref[...] + lnb_ref[...]

    @pl.when(i == n - 1)
    def _():
        wait_gather(1 - slot)


def _combine(dest0, dest1, rows_tiles, x1, route, mod, ln_w, ln_b, *, seq, tm, alpha):
    t, d = x1.shape
    nt = t // tm
    kern = functools.partial(_combine_kernel, tm=tm, alpha=alpha)
    cur = lambda i: (i, 0, 0)
    nxt = lambda i: (jnp.minimum(i + 1, nt - 1), 0, 0)
    idx_spec = lambda m: pl.BlockSpec((1, 1, tm), m, memory_space=pltpu.SMEM)
    row_spec = pl.BlockSpec((tm, d), lambda i: (i, 0))
    vec_spec = pl.BlockSpec((1, d), lambda i: (0, 0))
    return pl.pallas_call(
        kern,
        grid=(nt,),
        in_specs=[idx_spec(cur), idx_spec(nxt), idx_spec(cur), idx_spec(nxt),
                  pl.BlockSpec(memory_space=pl.ANY),
                  row_spec,
                  pl.BlockSpec((tm, LANES), lambda i: (i, 0)),
                  pl.BlockSpec((1,) + mod.shape[1:], lambda i: ((i * tm) // seq, 0, 0)),
                  vec_spec, vec_spec],
        out_specs=row_spec,
        out_shape=jax.ShapeDtypeStruct((t, d), F32),
        scratch_shapes=[pltpu.VMEM((2, 2, tm * ROW_SUB, LANES), F32),
                        pltpu.SemaphoreType.DMA((2,))],
        compiler_params=pltpu.CompilerParams(
            dimension_semantics=("arbitrary",), vmem_limit_bytes=VMEM_LIMIT),
        name="combine",
    )(dest0, dest0, dest1, dest1, rows_tiles, x1, route, mod, ln_w, ln_b)


def _dispatch_plan(route, counts, blk):
    t = route.shape[0]
    eid = route[:, :2].astype(jnp.int32)
    pos = route[:, 4:6].astype(jnp.int32)
    counts = counts[:N_EXPERTS].astype(jnp.int32)
    pcounts = ((counts + blk - 1) // blk) * blk
    pend = jnp.cumsum(pcounts)
    pstart = pend - pcounts
    experts = jnp.arange(N_EXPERTS, dtype=jnp.int32)
    dest = jnp.sum(jnp.where(eid[:, :, None] == experts, pstart, 0), axis=-1) + pos
    p_rows = ((2 * t + N_EXPERTS * (blk - 1) + blk - 1) // blk) * blk
    nblk = p_rows // blk
    block_start = jnp.arange(nblk, dtype=jnp.int32) * blk
    block_expert = jnp.minimum(jnp.sum((pend[None, :] <= block_start[:, None]).astype(jnp.int32), axis=1),
                               N_EXPERTS - 1)
    n_active = (pend[N_EXPERTS - 1:] // blk).astype(jnp.int32)
    tail = pend[N_EXPERTS - 1] + jnp.arange(N_EXPERTS, dtype=jnp.int32) * blk
    fill_start = jnp.concatenate([jnp.where(pcounts > 0, pend - blk, -1), jnp.where(tail < p_rows, tail, -1)])
    return dest[:, 0], dest[:, 1], fill_start, block_expert, n_active, p_rows


def _layer(x, c, positions, w_ada, b_ada, w_in, w_g2, b_g2, gla_nw, lq1, lk1, lq2, lk2, diff_nw, w_out,
           ln1_w, ln1_b, w_rg, b_rg, w_re, b_re, w_eg, w_eu, w_ed, ln2_w, ln2_b, *, lam_init,
           tm_in, tn_in, gla_rows, bq, bk, tm_out, moe_blk, tm_dsp, tm_cmb):
    bsz, seq, d = x.shape
    t = bsz * seq
    alpha = (2.0 * DEPTH) ** 0.25
    gla_dk = d // (2 * GLA_HEADS)
    gla_dv = d // GLA_HEADS
    dh = d // (2 * DIFF_HEADS)
    hk = GLA_HEADS * gla_dk
    hv = GLA_HEADS * gla_dv
    dq = DIFF_HEADS * 2 * dh
    gr_col = 2 * hk + 2 * hv
    q_col, k_col = 0, hk
    v_col, g_col = 2 * hk, 2 * hk + hv
    dq_col = gr_col
    dk_col = dq_col + dq
    dv_col = dk_col + dq
    mg_col = dv_col + dq
    w_main = jnp.concatenate([w_in[:, :gr_col], w_in[:, gr_col + GLA_GATE_RANK:]], axis=1).astype(BF16)
    w_gr = jnp.pad(w_in[:, gr_col:gr_col + GLA_GATE_RANK], ((0, 0), (0, LANES - GLA_GATE_RANK))).astype(BF16)
    w_g2p = jnp.pad(w_g2, ((0, LANES - GLA_GATE_RANK), (0, 0)))
    x2 = x.reshape(t, d)

    ada = _ada(c, w_ada, b_ada.reshape(1, -1))
    mod = ada.reshape(bsz, 6, d)

    half = dh // 2
    inv = ROPE_THETA ** (-jnp.arange(half, dtype=F32) / half)
    inv_row = jnp.tile(inv, LANES // half).reshape(1, LANES)
    cs = _rope_tables(positions.reshape(t, 1), inv_row, tm_in)

    proj, log_g = _inproj(x2, mod, cs, w_main, w_gr, w_g2p, b_g2.reshape(1, -1), seq=seq, tm=tm_in, tn=tn_in,
                          q_tile=dq_col // tn_in, k_tile=dk_col // tn_in, q_scale=dh ** -0.5 * LOG2_E)

    o_gla = _gla(proj, log_g, gla_nw.reshape(1, -1), bsz=bsz, seq=seq, rows=gla_rows, dk=gla_dk, dv=gla_dv,
                 q_col=q_col, k_col=k_col, v_col=v_col, g_col=g_col)

    lam_vecs = jnp.pad(jnp.stack([lq1, lk1, lq2, lk2]), ((0, 4), (0, LANES - dh)))
    o_diff = _diff(lam_vecs, proj, diff_nw.reshape(1, -1), bsz=bsz, seq=seq, bq=bq, bk=bk, dh=dh,
                   q_col=dq_col, k_col=dk_col, v_col=dv_col, lam_init=lam_init)

    w_r = jnp.pad(jnp.concatenate([w_rg, w_re], axis=1), ((0, 0), (0, LANES - N_GROUPS - N_EXPERTS)))
    b_r = jnp.pad(jnp.concatenate([b_rg, b_re]), (0, LANES - N_GROUPS - N_EXPERTS)).reshape(1, LANES)
    x1, u2_tiles, route, counts = _outproj(o_gla, o_diff, proj, x2, mod, w_out.astype(BF16), ln1_w.reshape(1, -1),
                                           ln1_b.reshape(1, -1), w_r, b_r, seq=seq, tm=tm_out, ga_col=mg_col,
                                           alpha=alpha)

    dest0, dest1, fill_start, block_expert, n_active, p_rows = _dispatch_plan(route, counts[0], moe_blk)
    xs_tiles = _dispatch(fill_start, dest0.reshape(-1, 1, tm_dsp), dest1.reshape(-1, 1, tm_dsp), u2_tiles,
                         p_rows=p_rows, tm=tm_dsp, blk=moe_blk)
    rows = _experts(block_expert, n_active, xs_tiles, w_eg, w_eu, w_ed, blk=moe_blk)
    out = _combine(dest0.reshape(-1, 1, tm_cmb), dest1.reshape(-1, 1, tm_cmb), rows, x1, route, mod,
                   ln2_w.reshape(1, -1), ln2_b.reshape(1, -1), seq=seq, tm=tm_cmb, alpha=alpha)
    return out.reshape(bsz, seq, d)


def kernel(x, c, positions, w_ada, b_ada, w_in, w_gla_gate2, b_gla_gate2, gla_norm_w, diff_lambda_q1,
           diff_lambda_k1, diff_lambda_q2, diff_lambda_k2, diff_norm_w, w_out, ln1_w, ln1_b, w_router_group,
           b_router_group, w_router_expert, b_router_expert, w_exp_gate, w_exp_up, w_exp_down, ln2_w, ln2_b):
    assert w_ada.shape[0] == DEPTH
    for l in range(DEPTH):
        lam_init = 0.8 - 0.6 * math.exp(-0.3 * l)
        x = _layer(x, c, positions, w_ada[l], b_ada[l], w_in[l], w_gla_gate2[l], b_gla_gate2[l], gla_norm_w[l],
                   diff_lambda_q1[l], diff_lambda_k1[l], diff_lambda_q2[l], diff_lambda_k2[l], diff_norm_w[l],
                   w_out[l], ln1_w[l], ln1_b[l], w_router_group[l], b_router_group[l], w_router_expert[l],
                   b_router_expert[l], w_exp_gate[l], w_exp_up[l], w_exp_down[l], ln2_w[l], ln2_b[l],
                   lam_init=lam_init, tm_in=1024, tn_in=1024, gla_rows=512, bq=512, bk=512, tm_out=512,
                   moe_blk=256, tm_dsp=2048, tm_cmb=256)
    return x
```

```python
import functools
import math

import jax
import jax.numpy as jnp
from jax import lax
from jax.experimental import pallas as pl
from jax.experimental.pallas import tpu as pltpu

F32 = jnp.float32
BF16 = jnp.bfloat16
HIGHEST = lax.Precision.HIGHEST

DEPTH = 1
GLA_HEADS = 4
GLA_GATE_RANK = 16
GLA_TAU = 16.0
GLA_CHUNK = 64
DIFF_HEADS = 8
ROPE_THETA = 10000.0
N_GROUPS = 4
EXPERTS_PER_GROUP = 8
N_EXPERTS = N_GROUPS * EXPERTS_PER_GROUP
LN_EPS = 1e-5
RMS_EPS = 1e-6
LOG2_E = math.log2(math.e)
LANES = 128
VMEM_LIMIT = 56 * 1024 * 1024

NT_DIMS = (((1,), (1,)), ((), ()))
TN_DIMS = (((0,), (0,)), ((), ()))


def _layer_norm(x):
    mu = jnp.mean(x, axis=-1, keepdims=True)
    xc = x - mu
    var = jnp.mean(xc * xc, axis=-1, keepdims=True)
    return xc * lax.rsqrt(var + LN_EPS)


def _silu(x):
    return x * jax.nn.sigmoid(x)


def _ada_kernel(c_ref, w_ref, b_ref, o_ref):
    s = _silu(c_ref[...])
    o_ref[...] = jnp.dot(s, w_ref[...], preferred_element_type=F32, precision=HIGHEST) + b_ref[...]


def _ada(c, w, b):
    bsz, d = c.shape
    n = w.shape[1]
    tn = d
    return pl.pallas_call(
        _ada_kernel,
        grid=(n // tn,),
        in_specs=[pl.BlockSpec((bsz, d), lambda j: (0, 0)),
                  pl.BlockSpec((d, tn), lambda j: (0, j)),
                  pl.BlockSpec((1, tn), lambda j: (0, j))],
        out_specs=pl.BlockSpec((bsz, tn), lambda j: (0, j)),
        out_shape=jax.ShapeDtypeStruct((bsz, n), F32),
        name="ada",
    )(c, w, b)


def _rope_kernel(pos_ref, inv_ref, cs_ref):
    ang = pos_ref[...].astype(F32) * inv_ref[...]
    lane = lax.broadcasted_iota(jnp.int32, ang.shape, 1)
    first_half = (lane % 64) < 32
    sin = jnp.sin(ang)
    cs_ref[:, :LANES] = jnp.cos(ang)
    cs_ref[:, LANES:] = jnp.where(first_half, -sin, sin)


def _rope_tables(pos_col, inv_row, tm):
    t = pos_col.shape[0]
    return pl.pallas_call(
        _rope_kernel,
        grid=(t // tm,),
        in_specs=[pl.BlockSpec((tm, 1), lambda i: (i, 0)),
                  pl.BlockSpec((1, LANES), lambda i: (0, 0))],
        out_specs=pl.BlockSpec((tm, 2 * LANES), lambda i: (i, 0)),
        out_shape=jax.ShapeDtypeStruct((t, 2 * LANES), F32),
        name="rope_tables",
    )(pos_col, inv_row)


def _inproj_kernel(x_ref, mod_ref, cs_ref, w_ref, wgr_ref, wg2_ref, bg2_ref, o_ref, lg_ref, u_ref,
                   *, q_tile, k_tile, q_scale):
    j = pl.program_id(1)

    @pl.when(j == 0)
    def _():
        shift = mod_ref[0, 0:1, :]
        scale = mod_ref[0, 1:2, :]
        u = (_layer_norm(x_ref[...]) * (1.0 + scale) + shift).astype(BF16)
        u_ref[...] = u
        gr = jnp.dot(u, wgr_ref[...], preferred_element_type=F32)
        z = jnp.dot(gr, wg2_ref[...], preferred_element_type=F32, precision=HIGHEST) + bg2_ref[...]
        log_sig = jnp.minimum(z, 0.0) - jnp.log(1.0 + jnp.exp(-jnp.abs(z)))
        lg_ref[...] = log_sig * (1.0 / GLA_TAU)

    acc = jnp.dot(u_ref[...], w_ref[...], preferred_element_type=F32)
    is_rope = jnp.logical_or(j == q_tile, j == k_tile)

    @pl.when(is_rope)
    def _():
        tn = acc.shape[1]
        sc = jnp.where(j == q_tile, q_scale, 1.0).astype(F32)
        cos = cs_ref[:, :LANES] * sc
        sin = cs_ref[:, LANES:] * sc
        lane = lax.broadcasted_iota(jnp.int32, cos.shape, 1)
        first_half = (lane % 64) < 32
        for h in range(tn // LANES):
            t = acc[:, h * LANES:(h + 1) * LANES]
            partner = jnp.where(first_half, pltpu.roll(t, LANES - 32, 1), pltpu.roll(t, 32, 1))
            o_ref[:, h * LANES:(h + 1) * LANES] = (t * cos + partner * sin).astype(o_ref.dtype)

    @pl.when(jnp.logical_not(is_rope))
    def _():
        o_ref[...] = acc.astype(o_ref.dtype)


def _inproj(x2, mod, cs, w_main, w_gr, w_g2, b_g2, *, seq, tm, tn, q_tile, k_tile, q_scale):
    t, d = x2.shape
    n = w_main.shape[1]
    ng = w_g2.shape[1]
    kern = functools.partial(_inproj_kernel, q_tile=q_tile, k_tile=k_tile, q_scale=q_scale)
    return pl.pallas_call(
        kern,
        grid=(t // tm, n // tn),
        in_specs=[pl.BlockSpec((tm, d), lambda i, j: (i, 0)),
                  pl.BlockSpec((1,) + mod.shape[1:], lambda i, j: ((i * tm) // seq, 0, 0)),
                  pl.BlockSpec((tm, 2 * LANES), lambda i, j: (i, 0)),
                  pl.BlockSpec((d, tn), lambda i, j: (0, j)),
                  pl.BlockSpec((d, LANES), lambda i, j: (0, 0)),
                  pl.BlockSpec((LANES, ng), lambda i, j: (0, 0)),
                  pl.BlockSpec((1, ng), lambda i, j: (0, 0))],
        out_specs=[pl.BlockSpec((tm, tn), lambda i, j: (i, j)),
                   pl.BlockSpec((tm, ng), lambda i, j: (i, 0))],
        out_shape=[jax.ShapeDtypeStruct((t, n), BF16),
                   jax.ShapeDtypeStruct((t, ng), F32)],
        scratch_shapes=[pltpu.VMEM((tm, d), BF16)],
        compiler_params=pltpu.CompilerParams(
            dimension_semantics=("arbitrary", "arbitrary"), vmem_limit_bytes=VMEM_LIMIT),
        name="inproj",
    )(x2, mod, cs, w_main, w_gr, w_g2, b_g2)


def _gla_kernel(q_ref, k_ref, v_ref, g_ref, lg_ref, nw_ref, o_ref, st_ref, *, chunk, q_scale):
    @pl.when(pl.program_id(2) == 0)
    def _():
        st_ref[...] = jnp.zeros_like(st_ref)

    rows = q_ref.shape[0]
    n_chunks = rows // chunk

    b = lg_ref[...]
    pos = lax.broadcasted_iota(jnp.int32, b.shape, 0) % chunk
    step = 1
    while step < chunk:
        b = b + jnp.where(pos >= step, pltpu.roll(b, step, 0), 0.0)
        step *= 2

    k = k_ref[...].astype(F32)
    q_t = (q_ref[...].astype(F32) * q_scale * jnp.exp(b)).astype(BF16)
    k_t = (k * jnp.exp(-b)).astype(BF16)
    v = v_ref[...]

    r = lax.broadcasted_iota(jnp.int32, (rows, rows), 0)
    c = lax.broadcasted_iota(jnp.int32, (rows, rows), 1)
    keep = jnp.logical_and(c <= r, (r // chunk) == (c // chunk))
    attn = lax.dot_general(q_t, k_t, NT_DIMS, preferred_element_type=F32)
    o = jnp.dot(jnp.where(keep, attn, 0.0).astype(BF16), v, preferred_element_type=F32)

    st = st_ref[...]
    o_inter = []
    for ci in range(n_chunks):
        sl = slice(ci * chunk, (ci + 1) * chunk)
        b_c = b[sl, :]
        b_last = b_c[chunk - 1:chunk, :]
        k_d = (k[sl, :] * jnp.exp(b_last - b_c)).astype(BF16)
        kv = lax.dot_general(v[sl, :], k_d, TN_DIMS, preferred_element_type=F32)
        o_inter.append(lax.dot_general(q_t[sl, :], st.astype(BF16), NT_DIMS, preferred_element_type=F32))
        st = st * jnp.exp(b_last) + kv
    st_ref[...] = st
    o = o + jnp.concatenate(o_inter, axis=0)

    ms = jnp.mean(o * o, axis=-1, keepdims=True)
    y = o * lax.rsqrt(ms + RMS_EPS) * nw_ref[...] * _silu(g_ref[...].astype(F32))
    o_ref[...] = y.astype(o_ref.dtype)


def _gla(proj, log_g, norm_w, *, bsz, seq, rows, dk, dv, q_col, k_col, v_col, g_col):
    t = proj.shape[0]
    nl = seq // rows
    kern = functools.partial(_gla_kernel, chunk=GLA_CHUNK, q_scale=dk ** -0.5)
    row = lambda b, h, l: b * nl + l
    return pl.pallas_call(
        kern,
        grid=(bsz, GLA_HEADS, nl),
        in_specs=[pl.BlockSpec((rows, dk), lambda b, h, l: (row(b, h, l), q_col // dk + h)),
                  pl.BlockSpec((rows, dk), lambda b, h, l: (row(b, h, l), k_col // dk + h)),
                  pl.BlockSpec((rows, dv), lambda b, h, l: (row(b, h, l), v_col // dv + h)),
                  pl.BlockSpec((rows, dv), lambda b, h, l: (row(b, h, l), g_col // dv + h)),
                  pl.BlockSpec((rows, dk), lambda b, h, l: (row(b, h, l), h)),
                  pl.BlockSpec((1, dv), lambda b, h, l: (0, 0))],
        out_specs=pl.BlockSpec((rows, dv), lambda b, h, l: (row(b, h, l), h)),
        out_shape=jax.ShapeDtypeStruct((t, GLA_HEADS * dv), BF16),
        scratch_shapes=[pltpu.VMEM((dv, dk), F32)],
        compiler_params=pltpu.CompilerParams(
            dimension_semantics=("arbitrary", "arbitrary", "arbitrary"), vmem_limit_bytes=VMEM_LIMIT),
        name="gla",
    )(proj, proj, proj, proj, log_g, norm_w)


def _diff_kernel(lam_ref, q_ref, k_ref, v_ref, nw_ref, o_ref, qs_ref, vt_ref, sa_ref, sb_ref, m_ref, l_ref, acc_ref,
                 *, bq, bk, dh, lam_init):
    qi = pl.program_id(2)
    q0 = qi * bq
    seq = k_ref.shape[0]

    @pl.when(qi == 0)
    def _():
        for c in range(seq // bk):
            vt_ref[:, c * bk:(c + 1) * bk] = v_ref[c * bk:(c + 1) * bk, :].astype(F32).T.astype(BF16)

    q = q_ref[...]
    lane = lax.broadcasted_iota(jnp.int32, q.shape, 1)
    zero = jnp.zeros_like(q)
    qs_ref[:bq, :] = jnp.where(lane < dh, q, zero)
    qs_ref[bq:, :] = jnp.where(lane >= dh, q, zero)
    m_ref[...] = jnp.full_like(m_ref, -jnp.inf)
    l_ref[...] = jnp.zeros_like(l_ref)
    acc_ref[...] = jnp.zeros_like(acc_ref)

    def scores(kj, dst):
        k0 = pl.multiple_of(kj * bk, bk)
        dst[...] = lax.dot_general(k_ref[pl.ds(k0, bk), :], qs_ref[...], NT_DIMS, preferred_element_type=F32)

    def update(src, kj, masked):
        k0 = pl.multiple_of(kj * bk, bk)
        s = src[...]
        if masked:
            key = k0 + lax.broadcasted_iota(jnp.int32, s.shape, 0)
            col = lax.broadcasted_iota(jnp.int32, s.shape, 1)
            qpos = q0 + jnp.where(col >= bq, col - bq, col)
            s = jnp.where(key <= qpos, s, -jnp.inf)
        m_prev = m_ref[...]
        m_new = jnp.maximum(m_prev, jnp.max(s, axis=0, keepdims=True))
        alpha = jnp.exp2(m_prev - m_new)
        p = jnp.exp2(s - m_new)
        l_ref[...] = alpha * l_ref[...] + jnp.sum(p, axis=0, keepdims=True)
        acc_ref[...] = alpha * acc_ref[...] + jnp.dot(
            vt_ref[:, pl.ds(k0, bk)], p.astype(BF16), preferred_element_type=F32)
        m_ref[...] = m_new

    n_full = (q0 + 1) // bk
    scores(0, sa_ref)

    def pair_body(jj, carry):
        j = 2 * jj
        scores(j + 1, sb_ref)
        update(sa_ref, j, False)
        scores(j + 2, sa_ref)
        update(sb_ref, j + 1, False)
        return carry

    lax.fori_loop(0, n_full // 2, pair_body, 0)
    odd = (n_full % 2) == 1

    @pl.when(odd)
    def _():
        scores(n_full, sb_ref)
        update(sa_ref, n_full - 1, False)
        update(sb_ref, n_full, True)

    @pl.when(jnp.logical_not(odd))
    def _():
        update(sa_ref, n_full, True)

    lv = lam_ref[...]
    s1 = jnp.sum(lv[0:1, :] * lv[1:2, :], axis=-1, keepdims=True)
    s2 = jnp.sum(lv[2:3, :] * lv[3:4, :], axis=-1, keepdims=True)
    lam = jnp.exp(s1) - jnp.exp(s2) + lam_init
    o_all = acc_ref[...] / l_ref[...]
    o_t = o_all[:, :bq] - lam * o_all[:, bq:]
    ms = jnp.mean(o_t * o_t, axis=0, keepdims=True)
    y = (o_t * lax.rsqrt(ms + RMS_EPS)).T * nw_ref[...] * (1.0 - lam_init)
    o_ref[...] = y.astype(o_ref.dtype)


def _diff(lam_vecs, proj, norm_w, *, bsz, seq, bq, bk, dh, q_col, k_col, v_col, lam_init):
    assert bk % bq == 0 and seq % bk == 0
    t = proj.shape[0]
    w = 2 * dh
    nq = seq // bq
    kern = functools.partial(_diff_kernel, bq=bq, bk=bk, dh=dh, lam_init=lam_init)
    return pl.pallas_call(
        kern,
        grid=(bsz, DIFF_HEADS, nq),
        in_specs=[pl.BlockSpec(lam_vecs.shape, lambda b, h, i: (0, 0)),
                  pl.BlockSpec((bq, w), lambda b, h, i: (b * nq + i, q_col // w + h)),
                  pl.BlockSpec((seq, w), lambda b, h, i: (b, k_col // w + h)),
                  pl.BlockSpec((seq, w), lambda b, h, i: (b, v_col // w + h)),
                  pl.BlockSpec((1, w), lambda b, h, i: (0, 0))],
        out_specs=pl.BlockSpec((bq, w), lambda b, h, i: (b * nq + i, h)),
        out_shape=jax.ShapeDtypeStruct((t, DIFF_HEADS * w), BF16),
        scratch_shapes=[pltpu.VMEM((2 * bq, w), BF16),
                        pltpu.VMEM((w, seq), BF16),
                        pltpu.VMEM((bk, 2 * bq), F32),
                        pltpu.VMEM((bk, 2 * bq), F32),
                        pltpu.VMEM((1, 2 * bq), F32),
                        pltpu.VMEM((1, 2 * bq), F32),
                        pltpu.VMEM((w, 2 * bq), F32)],
        compiler_params=pltpu.CompilerParams(
            dimension_semantics=("arbitrary", "arbitrary", "arbitrary"), vmem_limit_bytes=VMEM_LIMIT),
        name="diff_attn",
    )(lam_vecs, proj, proj, proj, norm_w)


ROW_SUB = 8


def _store_row_tiles(ref, val):
    rows = val.shape[0]
    for s in range(ROW_SUB):
        ref[pl.ds(s, rows, stride=ROW_SUB), :] = val[:, s * LANES:(s + 1) * LANES]


def _load_row_tiles(ref, rows):
    return jnp.concatenate([ref[pl.ds(s, rows, stride=ROW_SUB), :] for s in range(ROW_SUB)], axis=1)


def _tile_copy(src_hbm, dst_vmem, sem, src_row, dst_row):
    return pltpu.make_async_copy(src_hbm.at[pl.ds(src_row * ROW_SUB, ROW_SUB), :],
                                 dst_vmem.at[pl.ds(dst_row * ROW_SUB, ROW_SUB), :], sem)


def _wait_tiles(src_hbm, dst_vmem, sem):
    pltpu.make_async_copy(src_hbm.at[pl.ds(0, dst_vmem.shape[0]), :], dst_vmem, sem).wait()


def _outproj_kernel(og_ref, od_ref, ga_ref, gb_ref, x_ref, mod_ref, wo_ref, lnw_ref, lnb_ref, wr_ref, br_ref,
                    x1_ref, u2_ref, rt_ref, cnt_ref, tri_ref, run_ref, *, alpha):
    @pl.when(pl.program_id(0) == 0)
    def _():
        r = lax.broadcasted_iota(jnp.int32, tri_ref.shape, 0)
        c = lax.broadcasted_iota(jnp.int32, tri_ref.shape, 1)
        tri_ref[...] = (c < r).astype(BF16)
        run_ref[...] = jnp.zeros_like(run_ref)

    g_a = jax.nn.sigmoid(ga_ref[...].astype(F32))
    g_b = jax.nn.sigmoid(gb_ref[...].astype(F32))
    merged = (g_a * og_ref[...].astype(F32) + g_b * od_ref[...].astype(F32)).astype(BF16)
    y = jnp.dot(merged, wo_ref[...], preferred_element_type=F32)
    gate1 = mod_ref[0, 2:3, :]
    shift2 = mod_ref[0, 3:4, :]
    scale2 = mod_ref[0, 4:5, :]
    x1 = _layer_norm(alpha * x_ref[...] + gate1 * y) * lnw_ref[...] + lnb_ref[...]
    x1_ref[...] = x1
    u2 = _layer_norm(x1) * (1.0 + scale2) + shift2
    _store_row_tiles(u2_ref, u2)

    logits = jnp.dot(u2, wr_ref[...], preferred_element_type=F32, precision=HIGHEST) + br_ref[...]
    lane = lax.broadcasted_iota(jnp.int32, logits.shape, 1)
    neg = jnp.float32(-jnp.inf)
    big = jnp.int32(LANES)
    lg = jnp.where(lane < N_GROUPS, logits, neg)
    g_max = jnp.max(lg, axis=-1, keepdims=True)
    w_grp = 1.0 / jnp.sum(jnp.exp(lg - g_max), axis=-1, keepdims=True)
    g_idx = jnp.min(jnp.where(lg == g_max, lane, big), axis=-1, keepdims=True)
    lo = N_GROUPS + EXPERTS_PER_GROUP * g_idx
    le = jnp.where(jnp.logical_and(lane >= lo, lane < lo + EXPERTS_PER_GROUP), logits, neg)
    v1 = jnp.max(le, axis=-1, keepdims=True)
    i1 = jnp.min(jnp.where(le == v1, lane, big), axis=-1, keepdims=True)
    le2 = jnp.where(lane == i1, neg, le)
    v2 = jnp.max(le2, axis=-1, keepdims=True)
    i2 = jnp.min(jnp.where(le2 == v2, lane, big), axis=-1, keepdims=True)
    e2 = jnp.exp(v2 - v1)
    den = 1.0 + e2
    c1 = w_grp / den
    c2 = w_grp * e2 / den

    sel1 = lane == i1 - N_GROUPS
    sel2 = lane == i2 - N_GROUPS
    onehot = jnp.logical_or(sel1, sel2).astype(BF16)
    before = jnp.dot(tri_ref[...], onehot, preferred_element_type=F32) + run_ref[...]
    pos1 = jnp.sum(jnp.where(sel1, before, 0.0), axis=-1, keepdims=True)
    pos2 = jnp.sum(jnp.where(sel2, before, 0.0), axis=-1, keepdims=True)
    run_ref[...] += jnp.sum(onehot.astype(F32), axis=0, keepdims=True)
    cnt_ref[...] = jnp.broadcast_to(run_ref[...], cnt_ref.shape)

    cols = ((i1 - N_GROUPS).astype(F32), (i2 - N_GROUPS).astype(F32), c1, c2, pos1, pos2)
    rt = jnp.zeros_like(logits)
    for li, col in enumerate(cols):
        rt = jnp.where(lane == li, col, rt)
    rt_ref[...] = rt


def _outproj(o_gla, o_diff, proj, x2, mod, w_out, ln_w, ln_b, w_r, b_r, *, seq, tm, ga_col, alpha):
    t, d = x2.shape
    assert d == ROW_SUB * LANES
    kern = functools.partial(_outproj_kernel, alpha=alpha)
    row_spec = pl.BlockSpec((tm, d), lambda i: (i, 0))
    vec_spec = pl.BlockSpec((1, d), lambda i: (0, 0))
    return pl.pallas_call(
        kern,
        grid=(t // tm,),
        in_specs=[row_spec, row_spec,
                  pl.BlockSpec((tm, d), lambda i: (i, ga_col // d)),
                  pl.BlockSpec((tm, d), lambda i: (i, ga_col // d + 1)),
                  row_spec,
                  pl.BlockSpec((1,) + mod.shape[1:], lambda i: ((i * tm) // seq, 0, 0)),
                  pl.BlockSpec((d, d), lambda i: (0, 0)),
                  vec_spec, vec_spec,
                  pl.BlockSpec((d, LANES), lambda i: (0, 0)),
                  pl.BlockSpec((1, LANES), lambda i: (0, 0))],
        out_specs=[row_spec,
                   pl.BlockSpec((tm * ROW_SUB, LANES), lambda i: (i, 0)),
                   pl.BlockSpec((tm, LANES), lambda i: (i, 0)),
                   pl.BlockSpec((ROW_SUB, LANES), lambda i: (0, 0))],
        out_shape=[jax.ShapeDtypeStruct((t, d), F32),
                   jax.ShapeDtypeStruct((t * ROW_SUB, LANES), F32),
                   jax.ShapeDtypeStruct((t, LANES), F32),
                   jax.ShapeDtypeStruct((ROW_SUB, LANES), F32)],
        scratch_shapes=[pltpu.VMEM((tm, tm), BF16),
                        pltpu.VMEM((1, LANES), F32)],
        compiler_params=pltpu.CompilerParams(
            dimension_semantics=("arbitrary",), vmem_limit_bytes=VMEM_LIMIT),
        name="outproj",
    )(o_gla, o_diff, proj, proj, x2, mod, w_out, ln_w, ln_b, w_r, b_r)


def _dispatch_kernel(fill_ref, d0_ref, d1_ref, u_ref, xs_hbm, zbuf, sem_fill, sem, *, tm, blk):
    i = pl.program_id(0)

    @pl.when(i == 0)
    def _():
        zbuf[...] = jnp.zeros_like(zbuf)
        n_fill = fill_ref.shape[0]

        def fill_copy(j):
            return pltpu.make_async_copy(zbuf, xs_hbm.at[pl.ds(fill_ref[j] * ROW_SUB, blk * ROW_SUB), :], sem_fill)

        def start_fill(j, carry):
            @pl.when(fill_ref[j] >= 0)
            def _():
                fill_copy(j).start()
            return carry

        def wait_fill(j, carry):
            @pl.when(fill_ref[j] >= 0)
            def _():
                fill_copy(j).wait()
            return carry

        lax.fori_loop(0, n_fill, start_fill, 0)
        lax.fori_loop(0, n_fill, wait_fill, 0)

    def row_copy(dest_ref, r):
        return pltpu.make_async_copy(u_ref.at[pl.ds(r * ROW_SUB, ROW_SUB), :],
                                     xs_hbm.at[pl.ds(dest_ref[0, 0, r] * ROW_SUB, ROW_SUB), :], sem)

    def body(r, carry):
        row_copy(d0_ref, r).start(priority=0)
        row_copy(d1_ref, r).start(priority=1)
        return carry

    lax.fori_loop(0, tm, body, 0, unroll=4)
    for _ in range(2):
        pltpu.make_async_copy(u_ref, xs_hbm.at[pl.ds(0, tm * ROW_SUB), :], sem).wait()


def _dispatch(fill_start, dest0, dest1, u2_tiles, *, p_rows, tm, blk):
    t = u2_tiles.shape[0] // ROW_SUB
    kern = functools.partial(_dispatch_kernel, tm=tm, blk=blk)
    idx_spec = pl.BlockSpec((1, 1, tm), lambda i, fs: (i, 0, 0), memory_space=pltpu.SMEM)
    grid_spec = pltpu.PrefetchScalarGridSpec(
        num_scalar_prefetch=1,
        grid=(t // tm,),
        in_specs=[idx_spec, idx_spec, pl.BlockSpec((tm * ROW_SUB, LANES), lambda i, fs: (i, 0))],
        out_specs=pl.BlockSpec(memory_space=pl.ANY),
        scratch_shapes=[pltpu.VMEM((blk * ROW_SUB, LANES), F32),
                        pltpu.SemaphoreType.DMA(()),
                        pltpu.SemaphoreType.DMA(())],
    )
    return pl.pallas_call(
        kern,
        grid_spec=grid_spec,
        out_shape=jax.ShapeDtypeStruct((p_rows * ROW_SUB, LANES), F32),
        compiler_params=pltpu.CompilerParams(dimension_semantics=("arbitrary",)),
        name="dispatch",
    )(fill_start, dest0, dest1, u2_tiles)


def _expert_kernel(be_ref, na_ref, x_ref, wg_ref, wu_ref, wd_ref, o_ref, wgb, wub, wdb, *, blk):
    i = pl.program_id(0)
    active = i < na_ref[0]

    @pl.when(active)
    def _():
        changed = jnp.logical_or(i == 0, be_ref[i] != be_ref[jnp.maximum(i - 1, 0)])

        @pl.when(changed)
        def _():
            wgb[...] = wg_ref[0].astype(BF16)
            wub[...] = wu_ref[0].astype(BF16)
            wdb[...] = wd_ref[0].astype(BF16)

        xb = _load_row_tiles(x_ref, blk).astype(BF16)
        hg = jnp.dot(xb, wgb[...], preferred_element_type=F32)
        hu = jnp.dot(xb, wub[...], preferred_element_type=F32)
        h = (_silu(hg) * hu).astype(BF16)
        _store_row_tiles(o_ref, jnp.dot(h, wdb[...], preferred_element_type=F32))

    @pl.when(jnp.logical_not(active))
    def _():
        o_ref[...] = jnp.zeros_like(o_ref)


def _experts(block_expert, n_active, xs_tiles, w_gate, w_up, w_down, *, blk):
    nblk = block_expert.shape[0]
    e, d, ff = w_gate.shape
    kern = functools.partial(_expert_kernel, blk=blk)
    grid_spec = pltpu.PrefetchScalarGridSpec(
        num_scalar_prefetch=2,
        grid=(nblk,),
        in_specs=[pl.BlockSpec((blk * ROW_SUB, LANES), lambda i, be, na: (jnp.minimum(i, na[0] - 1), 0)),
                  pl.BlockSpec((1, d, ff), lambda i, be, na: (be[i], 0, 0)),
                  pl.BlockSpec((1, d, ff), lambda i, be, na: (be[i], 0, 0)),
                  pl.BlockSpec((1, ff, d), lambda i, be, na: (be[i], 0, 0))],
        out_specs=pl.BlockSpec((blk * ROW_SUB, LANES), lambda i, be, na: (i, 0)),
        scratch_shapes=[pltpu.VMEM((d, ff), BF16),
                        pltpu.VMEM((d, ff), BF16),
                        pltpu.VMEM((ff, d), BF16)],
    )
    return pl.pallas_call(
        kern,
        grid_spec=grid_spec,
        out_shape=jax.ShapeDtypeStruct((nblk * blk * ROW_SUB, LANES), F32),
        compiler_params=pltpu.CompilerParams(
            dimension_semantics=("arbitrary",), vmem_limit_bytes=VMEM_LIMIT),
        name="experts",
    )(block_expert, n_active, xs_tiles, w_gate, w_up, w_down)


def _combine_kernel(d0_ref, d0n_ref, d1_ref, d1n_ref, rows_hbm, x1_ref, rt_ref, mod_ref, lnw_ref, lnb_ref,
                    o_ref, buf, sem, *, tm, alpha):
    i = pl.program_id(0)
    n = pl.num_programs(0)
    slot = i % 2

    def start_gather(i0_ref, i1_ref, s):
        def body(r, carry):
            _tile_copy(rows_hbm, buf.at[s, 0], sem.at[s], i0_ref[0, 0, r], r).start(priority=0)
            _tile_copy(rows_hbm, buf.at[s, 1], sem.at[s], i1_ref[0, 0, r], r).start(priority=1)
            return carry
        lax.fori_loop(0, tm, body, 0, unroll=4)

    def wait_gather(s):
        _wait_tiles(rows_hbm, buf.at[s, 0], sem.at[s])
        _wait_tiles(rows_hbm, buf.at[s, 1], sem.at[s])

    @pl.when(i == 0)
    def _():
        start_gather(d0_ref, d1_ref, 0)

    start_gather(d0n_ref, d1n_ref, 1 - slot)
    wait_gather(slot)
    rt = rt_ref[...]
    y = rt[:, 2:3] * _load_row_tiles(buf.at[slot, 0], tm) + rt[:, 3:4] * _load_row_tiles(buf.at[slot, 1], tm)
    gate2 = mod_ref[0, 5:6, :]
    z = alpha * x1_ref[...] + gate2 * y
    o_ref[...] = _layer_norm(z) * lnw_ref[...] + lnb_ref[...]

    @pl.when(i == n - 1)
    def _():
        wait_gather(1 - slot)


def _combine(dest0, dest1, rows_tiles, x1, route, mod, ln_w, ln_b, *, seq, tm, alpha):
    t, d = x1.shape
    nt = t // tm
    kern = functools.partial(_combine_kernel, tm=tm, alpha=alpha)
    cur = lambda i: (i, 0, 0)
    nxt = lambda i: (jnp.minimum(i + 1, nt - 1), 0, 0)
    idx_spec = lambda m: pl.BlockSpec((1, 1, tm), m, memory_space=pltpu.SMEM)
    row_spec = pl.BlockSpec((tm, d), lambda i: (i, 0))
    vec_spec = pl.BlockSpec((1, d), lambda i: (0, 0))
    return pl.pallas_call(
        kern,
        grid=(nt,),
        in_specs=[idx_spec(cur), idx_spec(nxt), idx_spec(cur), idx_spec(nxt),
                  pl.BlockSpec(memory_space=pl.ANY),
                  row_spec,
                  pl.BlockSpec((tm, LANES), lambda i: (i, 0)),
                  pl.BlockSpec((1,) + mod.shape[1:], lambda i: ((i * tm) // seq, 0, 0)),
                  vec_spec, vec_spec],
        out_specs=row_spec,
        out_shape=jax.ShapeDtypeStruct((t, d), F32),
        scratch_shapes=[pltpu.VMEM((2, 2, tm * ROW_SUB, LANES), F32),
                        pltpu.SemaphoreType.DMA((2,))],
        compiler_params=pltpu.CompilerParams(
            dimension_semantics=("arbitrary",), vmem_limit_bytes=VMEM_LIMIT),
        name="combine",
    )(dest0, dest0, dest1, dest1, rows_tiles, x1, route, mod, ln_w, ln_b)


def _dispatch_plan(route, counts, blk):
    t = route.shape[0]
    eid = route[:, :2].astype(jnp.int32)
    pos = route[:, 4:6].astype(jnp.int32)
    counts = counts[:N_EXPERTS].astype(jnp.int32)
    pcounts = ((counts + blk - 1) // blk) * blk
    pend = jnp.cumsum(pcounts)
    pstart = pend - pcounts
    experts = jnp.arange(N_EXPERTS, dtype=jnp.int32)
    dest = jnp.sum(jnp.where(eid[:, :, None] == experts, pstart, 0), axis=-1) + pos
    p_rows = ((2 * t + N_EXPERTS * (blk - 1) + blk - 1) // blk) * blk
    nblk = p_rows // blk
    block_start = jnp.arange(nblk, dtype=jnp.int32) * blk
    block_expert = jnp.minimum(jnp.sum((pend[None, :] <= block_start[:, None]).astype(jnp.int32), axis=1),
                               N_EXPERTS - 1)
    n_active = (pend[N_EXPERTS - 1:] // blk).astype(jnp.int32)
    tail = pend[N_EXPERTS - 1] + jnp.arange(N_EXPERTS, dtype=jnp.int32) * blk
    fill_start = jnp.concatenate([jnp.where(pcounts > 0, pend - blk, -1), jnp.where(tail < p_rows, tail, -1)])
    return dest[:, 0], dest[:, 1], fill_start, block_expert, n_active, p_rows


def _layer(x, c, positions, w_ada, b_ada, w_in, w_g2, b_g2, gla_nw, lq1, lk1, lq2, lk2, diff_nw, w_out,
           ln1_w, ln1_b, w_rg, b_rg, w_re, b_re, w_eg, w_eu, w_ed, ln2_w, ln2_b, *, lam_init,
           tm_in, tn_in, gla_rows, bq, bk, tm_out, moe_blk, tm_dsp, tm_cmb):
    bsz, seq, d = x.shape
    t = bsz * seq
    alpha = (2.0 * DEPTH) ** 0.25
    gla_dk = d // (2 * GLA_HEADS)
    gla_dv = d // GLA_HEADS
    dh = d // (2 * DIFF_HEADS)
    hk = GLA_HEADS * gla_dk
    hv = GLA_HEADS * gla_dv
    dq = DIFF_HEADS * 2 * dh
    gr_col = 2 * hk + 2 * hv
    q_col, k_col = 0, hk
    v_col, g_col = 2 * hk, 2 * hk + hv
    dq_col = gr_col
    dk_col = dq_col + dq
    dv_col = dk_col + dq
    mg_col = dv_col + dq
    w_main = jnp.concatenate([w_in[:, :gr_col], w_in[:, gr_col + GLA_GATE_RANK:]], axis=1).astype(BF16)
    w_gr = jnp.pad(w_in[:, gr_col:gr_col + GLA_GATE_RANK], ((0, 0), (0, LANES - GLA_GATE_RANK))).astype(BF16)
    w_g2p = jnp.pad(w_g2, ((0, LANES - GLA_GATE_RANK), (0, 0)))
    x2 = x.reshape(t, d)

    ada = _ada(c, w_ada, b_ada.reshape(1, -1))
    mod = ada.reshape(bsz, 6, d)

    half = dh // 2
    inv = ROPE_THETA ** (-jnp.arange(half, dtype=F32) / half)
    inv_row = jnp.tile(inv, LANES // half).reshape(1, LANES)
    cs = _rope_tables(positions.reshape(t, 1), inv_row, tm_in)

    proj, log_g = _inproj(x2, mod, cs, w_main, w_gr, w_g2p, b_g2.reshape(1, -1), seq=seq, tm=tm_in, tn=tn_in,
                          q_tile=dq_col // tn_in, k_tile=dk_col // tn_in, q_scale=dh ** -0.5 * LOG2_E)

    o_gla = _gla(proj, log_g, gla_nw.reshape(1, -1), bsz=bsz, seq=seq, rows=gla_rows, dk=gla_dk, dv=gla_dv,
                 q_col=q_col, k_col=k_col, v_col=v_col, g_col=g_col)

    lam_vecs = jnp.pad(jnp.stack([lq1, lk1, lq2, lk2]), ((0, 4), (0, LANES - dh)))
    o_diff = _diff(lam_vecs, proj, diff_nw.reshape(1, -1), bsz=bsz, seq=seq, bq=bq, bk=bk, dh=dh,
                   q_col=dq_col, k_col=dk_col, v_col=dv_col, lam_init=lam_init)

    w_r = jnp.pad(jnp.concatenate([w_rg, w_re], axis=1), ((0, 0), (0, LANES - N_GROUPS - N_EXPERTS)))
    b_r = jnp.pad(jnp.concatenate([b_rg, b_re]), (0, LANES - N_GROUPS - N_EXPERTS)).reshape(1, LANES)
    x1, u2_tiles, route, counts = _outproj(o_gla, o_diff, proj, x2, mod, w_out.astype(BF16), ln1_w.reshape(1, -1),
                                           ln1_b.reshape(1, -1), w_r, b_r, seq=seq, tm=tm_out, ga_col=mg_col,
                                           alpha=alpha)

    dest0, dest1, fill_start, block_expert, n_active, p_rows = _dispatch_plan(route, counts[0], moe_blk)
    xs_tiles = _dispatch(fill_start, dest0.reshape(-1, 1, tm_dsp), dest1.reshape(-1, 1, tm_dsp), u2_tiles,
                         p_rows=p_rows, tm=tm_dsp, blk=moe_blk)
    rows = _experts(block_expert, n_active, xs_tiles, w_eg, w_eu, w_ed, blk=moe_blk)
    out = _combine(dest0.reshape(-1, 1, tm_cmb), dest1.reshape(-1, 1, tm_cmb), rows, x1, route, mod,
                   ln2_w.reshape(1, -1), ln2_b.reshape(1, -1), seq=seq, tm=tm_cmb, alpha=alpha)
    return out.reshape(bsz, seq, d)


def kernel(x, c, positions, w_ada, b_ada, w_in, w_gla_gate2, b_gla_gate2, gla_norm_w, diff_lambda_q1,
           diff_lambda_k1, diff_lambda_q2, diff_lambda_k2, diff_norm_w, w_out, ln1_w, ln1_b, w_router_group,
           b_router_group, w_router_expert, b_router_expert, w_exp_gate, w_exp_up, w_exp_down, ln2_w, ln2_b):
    assert w_ada.shape[0] == DEPTH
    for l in range(DEPTH):
        lam_init = 0.8 - 0.6 * math.exp(-0.3 * l)
        x = _layer(x, c, positions, w_ada[l], b_ada[l], w_in[l], w_gla_gate2[l], b_gla_gate2[l], gla_norm_w[l],
                   diff_lambda_q1[l], diff_lambda_k1[l], diff_lambda_q2[l], diff_lambda_k2[l], diff_norm_w[l],
                   w_out[l], ln1_w[l], ln1_b[l], w_router_group[l], b_router_group[l], w_router_expert[l],
                   b_router_expert[l], w_exp_gate[l], w_exp_up[l], w_exp_down[l], ln2_w[l], ln2_b[l],
                   lam_init=lam_init, tm_in=1024, tn_in=1024, gla_rows=512, bq=512, bk=512, tm_out=512,
                   moe_blk=256, tm_dsp=512, tm_cmb=256)
    return x
```

```python
import functools
import math

import jax
import jax.numpy as jnp
from jax import lax
from jax.experimental import pallas as pl
from jax.experimental.pallas import tpu as pltpu

F32 = jnp.float32
BF16 = jnp.bfloat16
HIGHEST = lax.Precision.HIGHEST

DEPTH = 1
GLA_HEADS = 4
GLA_GATE_RANK = 16
GLA_TAU = 16.0
GLA_CHUNK = 64
DIFF_HEADS = 8
ROPE_THETA = 10000.0
N_GROUPS = 4
EXPERTS_PER_GROUP = 8
N_EXPERTS = N_GROUPS * EXPERTS_PER_GROUP
LN_EPS = 1e-5
RMS_EPS = 1e-6
LOG2_E = math.log2(math.e)
LANES = 128
VMEM_LIMIT = 56 * 1024 * 1024

NT_DIMS = (((1,), (1,)), ((), ()))
TN_DIMS = (((0,), (0,)), ((), ()))


def _layer_norm(x):
    mu = jnp.mean(x, axis=-1, keepdims=True)
    xc = x - mu
    var = jnp.mean(xc * xc, axis=-1, keepdims=True)
    return xc * lax.rsqrt(var + LN_EPS)


def _silu(x):
    return x * jax.nn.sigmoid(x)


def _ada_kernel(c_ref, w_ref, b_ref, o_ref):
    s = _silu(c_ref[...])
    o_ref[...] = jnp.dot(s, w_ref[...], preferred_element_type=F32, precision=HIGHEST) + b_ref[...]


def _ada(c, w, b):
    bsz, d = c.shape
    n = w.shape[1]
    tn = d
    return pl.pallas_call(
        _ada_kernel,
        grid=(n // tn,),
        in_specs=[pl.BlockSpec((bsz, d), lambda j: (0, 0)),
                  pl.BlockSpec((d, tn), lambda j: (0, j)),
                  pl.BlockSpec((1, tn), lambda j: (0, j))],
        out_specs=pl.BlockSpec((bsz, tn), lambda j: (0, j)),
        out_shape=jax.ShapeDtypeStruct((bsz, n), F32),
        name="ada",
    )(c, w, b)


def _rope_kernel(pos_ref, inv_ref, cs_ref):
    ang = pos_ref[...].astype(F32) * inv_ref[...]
    lane = lax.broadcasted_iota(jnp.int32, ang.shape, 1)
    first_half = (lane % 64) < 32
    sin = jnp.sin(ang)
    cs_ref[:, :LANES] = jnp.cos(ang)
    cs_ref[:, LANES:] = jnp.where(first_half, -sin, sin)


def _rope_tables(pos_col, inv_row, tm):
    t = pos_col.shape[0]
    return pl.pallas_call(
        _rope_kernel,
        grid=(t // tm,),
        in_specs=[pl.BlockSpec((tm, 1), lambda i: (i, 0)),
                  pl.BlockSpec((1, LANES), lambda i: (0, 0))],
        out_specs=pl.BlockSpec((tm, 2 * LANES), lambda i: (i, 0)),
        out_shape=jax.ShapeDtypeStruct((t, 2 * LANES), F32),
        name="rope_tables",
    )(pos_col, inv_row)


def _inproj_kernel(x_ref, mod_ref, cs_ref, w_ref, wgr_ref, wg2_ref, bg2_ref, o_ref, lg_ref, u_ref,
                   *, q_tile, k_tile, q_scale):
    j = pl.program_id(1)

    @pl.when(j == 0)
    def _():
        shift = mod_ref[0, 0:1, :]
        scale = mod_ref[0, 1:2, :]
        u = (_layer_norm(x_ref[...]) * (1.0 + scale) + shift).astype(BF16)
        u_ref[...] = u
        gr = jnp.dot(u, wgr_ref[...], preferred_element_type=F32)
        z = jnp.dot(gr, wg2_ref[...], preferred_element_type=F32, precision=HIGHEST) + bg2_ref[...]
        log_sig = jnp.minimum(z, 0.0) - jnp.log(1.0 + jnp.exp(-jnp.abs(z)))
        lg_ref[...] = log_sig * (1.0 / GLA_TAU)

    acc = jnp.dot(u_ref[...], w_ref[...], preferred_element_type=F32)
    is_rope = jnp.logical_or(j == q_tile, j == k_tile)

    @pl.when(is_rope)
    def _():
        tn = acc.shape[1]
        sc = jnp.where(j == q_tile, q_scale, 1.0).astype(F32)
        cos = cs_ref[:, :LANES] * sc
        sin = cs_ref[:, LANES:] * sc
        lane = lax.broadcasted_iota(jnp.int32, cos.shape, 1)
        first_half = (lane % 64) < 32
        for h in range(tn // LANES):
            t = acc[:, h * LANES:(h + 1) * LANES]
            partner = jnp.where(first_half, pltpu.roll(t, LANES - 32, 1), pltpu.roll(t, 32, 1))
            o_ref[:, h * LANES:(h + 1) * LANES] = (t * cos + partner * sin).astype(o_ref.dtype)

    @pl.when(jnp.logical_not(is_rope))
    def _():
        o_ref[...] = acc.astype(o_ref.dtype)


def _inproj(x2, mod, cs, w_main, w_gr, w_g2, b_g2, *, seq, tm, tn, q_tile, k_tile, q_scale):
    t, d = x2.shape
    n = w_main.shape[1]
    ng = w_g2.shape[1]
    kern = functools.partial(_inproj_kernel, q_tile=q_tile, k_tile=k_tile, q_scale=q_scale)
    return pl.pallas_call(
        kern,
        grid=(t // tm, n // tn),
        in_specs=[pl.BlockSpec((tm, d), lambda i, j: (i, 0)),
                  pl.BlockSpec((1,) + mod.shape[1:], lambda i, j: ((i * tm) // seq, 0, 0)),
                  pl.BlockSpec((tm, 2 * LANES), lambda i, j: (i, 0)),
                  pl.BlockSpec((d, tn), lambda i, j: (0, j)),
                  pl.BlockSpec((d, LANES), lambda i, j: (0, 0)),
                  pl.BlockSpec((LANES, ng), lambda i, j: (0, 0)),
                  pl.BlockSpec((1, ng), lambda i, j: (0, 0))],
        out_specs=[pl.BlockSpec((tm, tn), lambda i, j: (i, j)),
                   pl.BlockSpec((tm, ng), lambda i, j: (i, 0))],
        out_shape=[jax.ShapeDtypeStruct((t, n), BF16),
                   jax.ShapeDtypeStruct((t, ng), F32)],
        scratch_shapes=[pltpu.VMEM((tm, d), BF16)],
        compiler_params=pltpu.CompilerParams(
            dimension_semantics=("arbitrary", "arbitrary"), vmem_limit_bytes=VMEM_LIMIT),
        name="inproj",
    )(x2, mod, cs, w_main, w_gr, w_g2, b_g2)


def _gla_kernel(q_ref, k_ref, v_ref, g_ref, lg_ref, nw_ref, o_ref, st_ref, *, chunk, q_scale):
    @pl.when(pl.program_id(2) == 0)
    def _():
        st_ref[...] = jnp.zeros_like(st_ref)

    rows = q_ref.shape[0]
    n_chunks = rows // chunk

    b = lg_ref[...]
    pos = lax.broadcasted_iota(jnp.int32, b.shape, 0) % chunk
    step = 1
    while step < chunk:
        b = b + jnp.where(pos >= step, pltpu.roll(b, step, 0), 0.0)
        step *= 2

    k = k_ref[...].astype(F32)
    q_t = (q_ref[...].astype(F32) * q_scale * jnp.exp(b)).astype(BF16)
    k_t = (k * jnp.exp(-b)).astype(BF16)
    v = v_ref[...]

    r = lax.broadcasted_iota(jnp.int32, (rows, rows), 0)
    c = lax.broadcasted_iota(jnp.int32, (rows, rows), 1)
    keep = jnp.logical_and(c <= r, (r // chunk) == (c // chunk))
    attn = lax.dot_general(q_t, k_t, NT_DIMS, preferred_element_type=F32)
    o = jnp.dot(jnp.where(keep, attn, 0.0).astype(BF16), v, preferred_element_type=F32)

    st = st_ref[...]
    o_inter = []
    for ci in range(n_chunks):
        sl = slice(ci * chunk, (ci + 1) * chunk)
        b_c = b[sl, :]
        b_last = b_c[chunk - 1:chunk, :]
        k_d = (k[sl, :] * jnp.exp(b_last - b_c)).astype(BF16)
        kv = lax.dot_general(v[sl, :], k_d, TN_DIMS, preferred_element_type=F32)
        o_inter.append(lax.dot_general(q_t[sl, :], st.astype(BF16), NT_DIMS, preferred_element_type=F32))
        st = st * jnp.exp(b_last) + kv
    st_ref[...] = st
    o = o + jnp.concatenate(o_inter, axis=0)

    ms = jnp.mean(o * o, axis=-1, keepdims=True)
    y = o * lax.rsqrt(ms + RMS_EPS) * nw_ref[...] * _silu(g_ref[...].astype(F32))
    o_ref[...] = y.astype(o_ref.dtype)


def _gla(proj, log_g, norm_w, *, bsz, seq, rows, dk, dv, q_col, k_col, v_col, g_col):
    t = proj.shape[0]
    nl = seq // rows
    kern = functools.partial(_gla_kernel, chunk=GLA_CHUNK, q_scale=dk ** -0.5)
    row = lambda b, h, l: b * nl + l
    return pl.pallas_call(
        kern,
        grid=(bsz, GLA_HEADS, nl),
        in_specs=[pl.BlockSpec((rows, dk), lambda b, h, l: (row(b, h, l), q_col // dk + h)),
                  pl.BlockSpec((rows, dk), lambda b, h, l: (row(b, h, l), k_col // dk + h)),
                  pl.BlockSpec((rows, dv), lambda b, h, l: (row(b, h, l), v_col // dv + h)),
                  pl.BlockSpec((rows, dv), lambda b, h, l: (row(b, h, l), g_col // dv + h)),
                  pl.BlockSpec((rows, dk), lambda b, h, l: (row(b, h, l), h)),
                  pl.BlockSpec((1, dv), lambda b, h, l: (0, 0))],
        out_specs=pl.BlockSpec((rows, dv), lambda b, h, l: (row(b, h, l), h)),
        out_shape=jax.ShapeDtypeStruct((t, GLA_HEADS * dv), BF16),
        scratch_shapes=[pltpu.VMEM((dv, dk), F32)],
        compiler_params=pltpu.CompilerParams(
            dimension_semantics=("arbitrary", "arbitrary", "arbitrary"), vmem_limit_bytes=VMEM_LIMIT),
        name="gla",
    )(proj, proj, proj, proj, log_g, norm_w)


def _diff_kernel(lam_ref, q_ref, qn_ref, k_ref, v_ref, nw_ref, o_ref,
                 qs_ref, vt_ref, s0_ref, sa_ref, sb_ref, m_ref, l_ref, acc_ref, *, bq, bk, dh, lam_init):
    qi = pl.program_id(2)
    q0 = qi * bq
    seq = k_ref.shape[0]

    def mask_queries(q):
        lane = lax.broadcasted_iota(jnp.int32, q.shape, 1)
        zero = jnp.zeros_like(q)
        qs_ref[:bq, :] = jnp.where(lane < dh, q, zero)
        qs_ref[bq:, :] = jnp.where(lane >= dh, q, zero)

    def scores(kj, dst):
        k0 = pl.multiple_of(kj * bk, bk)
        dst[...] = lax.dot_general(k_ref[pl.ds(k0, bk), :], qs_ref[...], NT_DIMS, preferred_element_type=F32)

    def update(src, kj, masked):
        k0 = pl.multiple_of(kj * bk, bk)
        s = src[...]
        if masked:
            key = k0 + lax.broadcasted_iota(jnp.int32, s.shape, 0)
            col = lax.broadcasted_iota(jnp.int32, s.shape, 1)
            qpos = q0 + jnp.where(col >= bq, col - bq, col)
            s = jnp.where(key <= qpos, s, -jnp.inf)
        m_prev = m_ref[...]
        m_new = jnp.maximum(m_prev, jnp.max(s, axis=0, keepdims=True))
        alpha = jnp.exp2(m_prev - m_new)
        p = jnp.exp2(s - m_new)
        l_ref[...] = alpha * l_ref[...] + jnp.sum(p, axis=0, keepdims=True)
        acc_ref[...] = alpha * acc_ref[...] + jnp.dot(
            vt_ref[:, pl.ds(k0, bk)], p.astype(BF16), preferred_element_type=F32)
        m_ref[...] = m_new

    @pl.when(qi == 0)
    def _():
        for c in range(seq // bk):
            vt_ref[:, c * bk:(c + 1) * bk] = v_ref[c * bk:(c + 1) * bk, :].astype(F32).T.astype(BF16)
        mask_queries(q_ref[...])
        scores(0, s0_ref)

    m_ref[...] = jnp.full_like(m_ref, -jnp.inf)
    l_ref[...] = jnp.zeros_like(l_ref)
    acc_ref[...] = jnp.zeros_like(acc_ref)

    n_full = (q0 + 1) // bk

    @pl.when(n_full == 0)
    def _():
        update(s0_ref, 0, True)

    @pl.when(n_full > 0)
    def _():
        scores(1, sa_ref)
        update(s0_ref, 0, False)
        rest = n_full - 1

        def pair_body(jj, carry):
            j = 1 + 2 * jj
            scores(j + 1, sb_ref)
            update(sa_ref, j, False)
            scores(j + 2, sa_ref)
            update(sb_ref, j + 1, False)
            return carry

        lax.fori_loop(0, rest // 2, pair_body, 0)
        odd = (rest % 2) == 1

        @pl.when(odd)
        def _():
            scores(n_full, sb_ref)
            update(sa_ref, n_full - 1, False)
            update(sb_ref, n_full, True)

        @pl.when(jnp.logical_not(odd))
        def _():
            update(sa_ref, n_full, True)

    mask_queries(qn_ref[...])
    scores(0, s0_ref)

    lv = lam_ref[...]
    s1 = jnp.sum(lv[0:1, :] * lv[1:2, :], axis=-1, keepdims=True)
    s2 = jnp.sum(lv[2:3, :] * lv[3:4, :], axis=-1, keepdims=True)
    lam = jnp.exp(s1) - jnp.exp(s2) + lam_init
    o_all = acc_ref[...] / l_ref[...]
    o_t = o_all[:, :bq] - lam * o_all[:, bq:]
    ms = jnp.mean(o_t * o_t, axis=0, keepdims=True)
    y = (o_t * lax.rsqrt(ms + RMS_EPS)).T * nw_ref[...] * (1.0 - lam_init)
    o_ref[...] = y.astype(o_ref.dtype)


def _diff(lam_vecs, proj, norm_w, *, bsz, seq, bq, bk, dh, q_col, k_col, v_col, lam_init):
    assert bk % bq == 0 and seq % bk == 0
    t = proj.shape[0]
    w = 2 * dh
    nq = seq // bq
    kern = functools.partial(_diff_kernel, bq=bq, bk=bk, dh=dh, lam_init=lam_init)
    return pl.pallas_call(
        kern,
        grid=(bsz, DIFF_HEADS, nq),
        in_specs=[pl.BlockSpec(lam_vecs.shape, lambda b, h, i: (0, 0)),
                  pl.BlockSpec((bq, w), lambda b, h, i: (b * nq + i, q_col // w + h)),
                  pl.BlockSpec((bq, w), lambda b, h, i: (b * nq + jnp.minimum(i + 1, nq - 1), q_col // w + h)),
                  pl.BlockSpec((seq, w), lambda b, h, i: (b, k_col // w + h)),
                  pl.BlockSpec((seq, w), lambda b, h, i: (b, v_col // w + h)),
                  pl.BlockSpec((1, w), lambda b, h, i: (0, 0))],
        out_specs=pl.BlockSpec((bq, w), lambda b, h, i: (b * nq + i, h)),
        out_shape=jax.ShapeDtypeStruct((t, DIFF_HEADS * w), BF16),
        scratch_shapes=[pltpu.VMEM((2 * bq, w), BF16),
                        pltpu.VMEM((w, seq), BF16),
                        pltpu.VMEM((bk, 2 * bq), F32),
                        pltpu.VMEM((bk, 2 * bq), F32),
                        pltpu.VMEM((bk, 2 * bq), F32),
                        pltpu.VMEM((1, 2 * bq), F32),
                        pltpu.VMEM((1, 2 * bq), F32),
                        pltpu.VMEM((w, 2 * bq), F32)],
        compiler_params=pltpu.CompilerParams(
            dimension_semantics=("arbitrary", "arbitrary", "arbitrary"), vmem_limit_bytes=VMEM_LIMIT),
        name="diff_attn",
    )(lam_vecs, proj, proj, proj, proj, norm_w)


ROW_SUB = 8


def _store_row_tiles(ref, val):
    rows = val.shape[0]
    for s in range(ROW_SUB):
        ref[pl.ds(s, rows, stride=ROW_SUB), :] = val[:, s * LANES:(s + 1) * LANES]


def _load_row_tiles(ref, rows):
    return jnp.concatenate([ref[pl.ds(s, rows, stride=ROW_SUB), :] for s in range(ROW_SUB)], axis=1)


def _tile_copy(src_hbm, dst_vmem, sem, src_row, dst_row):
    return pltpu.make_async_copy(src_hbm.at[pl.ds(src_row * ROW_SUB, ROW_SUB), :],
                                 dst_vmem.at[pl.ds(dst_row * ROW_SUB, ROW_SUB), :], sem)


def _wait_tiles(src_hbm, dst_vmem, sem):
    pltpu.make_async_copy(src_hbm.at[pl.ds(0, dst_vmem.shape[0]), :], dst_vmem, sem).wait()


def _outproj_kernel(og_ref, od_ref, ga_ref, gb_ref, x_ref, mod_ref, wo_ref, lnw_ref, lnb_ref, wr_ref, br_ref,
                    x1_ref, u2_ref, rt_ref, cnt_ref, tri_ref, run_ref, *, alpha):
    @pl.when(pl.program_id(0) == 0)
    def _():
        r = lax.broadcasted_iota(jnp.int32, tri_ref.shape, 0)
        c = lax.broadcasted_iota(jnp.int32, tri_ref.shape, 1)
        tri_ref[...] = (c < r).astype(BF16)
        run_ref[...] = jnp.zeros_like(run_ref)

    merged = jax.nn.sigmoid(ga_ref[...]) * og_ref[...] + jax.nn.sigmoid(gb_ref[...]) * od_ref[...]
    y = jnp.dot(merged, wo_ref[...], preferred_element_type=F32)
    gate1 = mod_ref[0, 2:3, :]
    shift2 = mod_ref[0, 3:4, :]
    scale2 = mod_ref[0, 4:5, :]
    x1 = _layer_norm(alpha * x_ref[...] + gate1 * y) * lnw_ref[...] + lnb_ref[...]
    x1_ref[...] = x1
    u2 = _layer_norm(x1) * (1.0 + scale2) + shift2
    _store_row_tiles(u2_ref, u2)

    logits = jnp.dot(u2.astype(BF16), wr_ref[...], preferred_element_type=F32) + br_ref[...]
    lane = lax.broadcasted_iota(jnp.int32, logits.shape, 1)
    neg = jnp.float32(-jnp.inf)
    big = jnp.int32(LANES)
    lg = jnp.where(lane < N_GROUPS, logits, neg)
    g_max = jnp.max(lg, axis=-1, keepdims=True)
    w_grp = 1.0 / jnp.sum(jnp.exp(lg - g_max), axis=-1, keepdims=True)
    g_idx = jnp.min(jnp.where(lg == g_max, lane, big), axis=-1, keepdims=True)
    lo = N_GROUPS + EXPERTS_PER_GROUP * g_idx
    le = jnp.where(jnp.logical_and(lane >= lo, lane < lo + EXPERTS_PER_GROUP), logits, neg)
    v1 = jnp.max(le, axis=-1, keepdims=True)
    i1 = jnp.min(jnp.where(le == v1, lane, big), axis=-1, keepdims=True)
    le2 = jnp.where(lane == i1, neg, le)
    v2 = jnp.max(le2, axis=-1, keepdims=True)
    i2 = jnp.min(jnp.where(le2 == v2, lane, big), axis=-1, keepdims=True)
    e2 = jnp.exp(v2 - v1)
    den = 1.0 + e2
    c1 = w_grp / den
    c2 = w_grp * e2 / den

    sel1 = lane == i1 - N_GROUPS
    sel2 = lane == i2 - N_GROUPS
    onehot = jnp.logical_or(sel1, sel2).astype(BF16)
    before = jnp.dot(tri_ref[...], onehot, preferred_element_type=F32) + run_ref[...]
    pos1 = jnp.sum(jnp.where(sel1, before, 0.0), axis=-1, keepdims=True)
    pos2 = jnp.sum(jnp.where(sel2, before, 0.0), axis=-1, keepdims=True)
    run_ref[...] += jnp.sum(onehot.astype(F32), axis=0, keepdims=True)
    cnt_ref[...] = jnp.broadcast_to(run_ref[...], cnt_ref.shape)

    cols = ((i1 - N_GROUPS).astype(F32), (i2 - N_GROUPS).astype(F32), c1, c2, pos1, pos2)
    rt = jnp.zeros_like(logits)
    for li, col in enumerate(cols):
        rt = jnp.where(lane == li, col, rt)
    rt_ref[...] = rt


def _outproj(o_gla, o_diff, proj, x2, mod, w_out, ln_w, ln_b, w_r, b_r, *, seq, tm, ga_col, alpha):
    t, d = x2.shape
    assert d == ROW_SUB * LANES
    kern = functools.partial(_outproj_kernel, alpha=alpha)
    row_spec = pl.BlockSpec((tm, d), lambda i: (i, 0))
    vec_spec = pl.BlockSpec((1, d), lambda i: (0, 0))
    return pl.pallas_call(
        kern,
        grid=(t // tm,),
        in_specs=[row_spec, row_spec,
                  pl.BlockSpec((tm, d), lambda i: (i, ga_col // d)),
                  pl.BlockSpec((tm, d), lambda i: (i, ga_col // d + 1)),
                  row_spec,
                  pl.BlockSpec((1,) + mod.shape[1:], lambda i: ((i * tm) // seq, 0, 0)),
                  pl.BlockSpec((d, d), lambda i: (0, 0)),
                  vec_spec, vec_spec,
                  pl.BlockSpec((d, LANES), lambda i: (0, 0)),
                  pl.BlockSpec((1, LANES), lambda i: (0, 0))],
        out_specs=[row_spec,
                   pl.BlockSpec((tm * ROW_SUB, LANES), lambda i: (i, 0)),
                   pl.BlockSpec((tm, LANES), lambda i: (i, 0)),
                   pl.BlockSpec((ROW_SUB, LANES), lambda i: (0, 0))],
        out_shape=[jax.ShapeDtypeStruct((t, d), F32),
                   jax.ShapeDtypeStruct((t * ROW_SUB, LANES), F32),
                   jax.ShapeDtypeStruct((t, LANES), F32),
                   jax.ShapeDtypeStruct((ROW_SUB, LANES), F32)],
        scratch_shapes=[pltpu.VMEM((tm, tm), BF16),
                        pltpu.VMEM((1, LANES), F32)],
        compiler_params=pltpu.CompilerParams(
            dimension_semantics=("arbitrary",), vmem_limit_bytes=VMEM_LIMIT),
        name="outproj",
    )(o_gla, o_diff, proj, proj, x2, mod, w_out, ln_w, ln_b, w_r, b_r)


def _dispatch_kernel(fill_ref, d0_ref, d1_ref, u_ref, xs_hbm, zbuf, sem_fill, sem, *, tm, blk):
    i = pl.program_id(0)

    @pl.when(i == 0)
    def _():
        zbuf[...] = jnp.zeros_like(zbuf)
        n_fill = fill_ref.shape[0]

        def fill_copy(j):
            return pltpu.make_async_copy(zbuf, xs_hbm.at[pl.ds(fill_ref[j] * ROW_SUB, blk * ROW_SUB), :], sem_fill)

        def start_fill(j, carry):
            @pl.when(fill_ref[j] >= 0)
            def _():
                fill_copy(j).start()
            return carry

        def wait_fill(j, carry):
            @pl.when(fill_ref[j] >= 0)
            def _():
                fill_copy(j).wait()
            return carry

        lax.fori_loop(0, n_fill, start_fill, 0)
        lax.fori_loop(0, n_fill, wait_fill, 0)

    def row_copy(dest_ref, r):
        return pltpu.make_async_copy(u_ref.at[pl.ds(r * ROW_SUB, ROW_SUB), :],
                                     xs_hbm.at[pl.ds(dest_ref[0, 0, r] * ROW_SUB, ROW_SUB), :], sem)

    def body(r, carry):
        row_copy(d0_ref, r).start(priority=0)
        row_copy(d1_ref, r).start(priority=1)
        return carry

    lax.fori_loop(0, tm, body, 0, unroll=4)
    for _ in range(2):
        pltpu.make_async_copy(u_ref, xs_hbm.at[pl.ds(0, tm * ROW_SUB), :], sem).wait()


def _dispatch(fill_start, dest0, dest1, u2_tiles, *, p_rows, tm, blk):
    t = u2_tiles.shape[0] // ROW_SUB
    kern = functools.partial(_dispatch_kernel, tm=tm, blk=blk)
    idx_spec = pl.BlockSpec((1, 1, tm), lambda i, fs: (i, 0, 0), memory_space=pltpu.SMEM)
    grid_spec = pltpu.PrefetchScalarGridSpec(
        num_scalar_prefetch=1,
        grid=(t // tm,),
        in_specs=[idx_spec, idx_spec, pl.BlockSpec((tm * ROW_SUB, LANES), lambda i, fs: (i, 0))],
        out_specs=pl.BlockSpec(memory_space=pl.ANY),
        scratch_shapes=[pltpu.VMEM((blk * ROW_SUB, LANES), F32),
                        pltpu.SemaphoreType.DMA(()),
                        pltpu.SemaphoreType.DMA(())],
    )
    return pl.pallas_call(
        kern,
        grid_spec=grid_spec,
        out_shape=jax.ShapeDtypeStruct((p_rows * ROW_SUB, LANES), F32),
        compiler_params=pltpu.CompilerParams(dimension_semantics=("arbitrary",)),
        name="dispatch",
    )(fill_start, dest0, dest1, u2_tiles)


def _expert_kernel(be_ref, na_ref, x_ref, wg_ref, wu_ref, wd_ref, o_ref, wgb, wub, wdb, *, blk):
    i = pl.program_id(0)
    active = i < na_ref[0]

    @pl.when(active)
    def _():
        changed = jnp.logical_or(i == 0, be_ref[i] != be_ref[jnp.maximum(i - 1, 0)])

        @pl.when(changed)
        def _():
            wgb[...] = wg_ref[0].astype(BF16)
            wub[...] = wu_ref[0].astype(BF16)
            wdb[...] = wd_ref[0].astype(BF16)

        xb = _load_row_tiles(x_ref, blk).astype(BF16)
        hg = jnp.dot(xb, wgb[...], preferred_element_type=F32)
        hu = jnp.dot(xb, wub[...], preferred_element_type=F32)
        h = (_silu(hg) * hu).astype(BF16)
        _store_row_tiles(o_ref, jnp.dot(h, wdb[...], preferred_element_type=F32))

    @pl.when(jnp.logical_not(active))
    def _():
        o_ref[...] = jnp.zeros_like(o_ref)


def _experts(block_expert, n_active, xs_tiles, w_gate, w_up, w_down, *, blk):
    nblk = block_expert.shape[0]
    e, d, ff = w_gate.shape
    kern = functools.partial(_expert_kernel, blk=blk)
    grid_spec = pltpu.PrefetchScalarGridSpec(
        num_scalar_prefetch=2,
        grid=(nblk,),
        in_specs=[pl.BlockSpec((blk * ROW_SUB, LANES), lambda i, be, na: (jnp.minimum(i, na[0] - 1), 0)),
                  pl.BlockSpec((1, d, ff), lambda i, be, na: (be[i], 0, 0)),
                  pl.BlockSpec((1, d, ff), lambda i, be, na: (be[i], 0, 0)),
                  pl.BlockSpec((1, ff, d), lambda i, be, na: (be[i], 0, 0))],
        out_specs=pl.BlockSpec((blk * ROW_SUB, LANES), lambda i, be, na: (i, 0)),
        scratch_shapes=[pltpu.VMEM((d, ff), BF16),
                        pltpu.VMEM((d, ff), BF16),
                        pltpu.VMEM((ff, d), BF16)],
    )
    return pl.pallas_call(
        kern,
        grid_spec=grid_spec,
        out_shape=jax.ShapeDtypeStruct((nblk * blk * ROW_SUB, LANES), F32),
        compiler_params=pltpu.CompilerParams(
            dimension_semantics=("arbitrary",), vmem_limit_bytes=VMEM_LIMIT),
        name="experts",
    )(block_expert, n_active, xs_tiles, w_gate, w_up, w_down)


def _combine_kernel(d0_ref, d0n_ref, d1_ref, d1n_ref, rows_hbm, x1_ref, rt_ref, mod_ref, lnw_ref, lnb_ref,
                    o_ref, buf, sem, *, tm, alpha):
    i = pl.program_id(0)
    n = pl.num_programs(0)
    slot = i % 2

    def start_gather(i0_ref, i1_ref, s):
        def body(r, carry):
            _tile_copy(rows_hbm, buf.at[s, 0], sem.at[s], i0_ref[0, 0, r], r).start(priority=0)
            _tile_copy(rows_hbm, buf.at[s, 1], sem.at[s], i1_ref[0, 0, r], r).start(priority=1)
            return carry
        lax.fori_loop(0, tm, body, 0, unroll=4)

    def wait_gather(s):
        _wait_tiles(rows_hbm, buf.at[s, 0], sem.at[s])
        _wait_tiles(rows_hbm, buf.at[s, 1], sem.at[s])

    @pl.when(i == 0)
    def _():
        start_gather(d0_ref, d1_ref, 0)

    start_gather(d0n_ref, d1n_ref, 1 - slot)
    wait_gather(slot)
    rt = rt_ref[...]
    y = rt[:, 2:3] * _load_row_tiles(buf.at[slot, 0], tm) + rt[:, 3:4] * _load_row_tiles(buf.at[slot, 1], tm)
    gate2 = mod_ref[0, 5:6, :]
    z = alpha * x1_ref[...] + gate2 * y
    o_ref[...] = _layer_norm(z) * lnw_ref[...] + lnb_ref[...]

    @pl.when(i == n - 1)
    def _():
        wait_gather(1 - slot)


def _combine(dest0, dest1, rows_tiles, x1, route, mod, ln_w, ln_b, *, seq, tm, alpha):
    t, d = x1.shape
    nt = t // tm
    kern = functools.partial(_combine_kernel, tm=tm, alpha=alpha)
    cur = lambda i: (i, 0, 0)
    nxt = lambda i: (jnp.minimum(i + 1, nt - 1), 0, 0)
    idx_spec = lambda m: pl.BlockSpec((1, 1, tm), m, memory_space=pltpu.SMEM)
    row_spec = pl.BlockSpec((tm, d), lambda i: (i, 0))
    vec_spec = pl.BlockSpec((1, d), lambda i: (0, 0))
    return pl.pallas_call(
        kern,
        grid=(nt,),
        in_specs=[idx_spec(cur), idx_spec(nxt), idx_spec(cur), idx_spec(nxt),
                  pl.BlockSpec(memory_space=pl.ANY),
                  row_spec,
                  pl.BlockSpec((tm, LANES), lambda i: (i, 0)),
                  pl.BlockSpec((1,) + mod.shape[1:], lambda i: ((i * tm) // seq, 0, 0)),
                  vec_spec, vec_spec],
        out_specs=row_spec,
        out_shape=jax.ShapeDtypeStruct((t, d), F32),
        scratch_shapes=[pltpu.VMEM((2, 2, tm * ROW_SUB, LANES), F32),
                        pltpu.SemaphoreType.DMA((2,))],
        compiler_params=pltpu.CompilerParams(
            dimension_semantics=("arbitrary",), vmem_limit_bytes=VMEM_LIMIT),
        name="combine",
    )(dest0, dest0, dest1, dest1, rows_tiles, x1, route, mod, ln_w, ln_b)


def _dispatch_plan(route, counts, blk):
    t = route.shape[0]
    eid = route[:, :2].astype(jnp.int32)
    pos = route[:, 4:6].astype(jnp.int32)
    counts = counts[:N_EXPERTS].astype(jnp.int32)
    pcounts = ((counts + blk - 1) // blk) * blk
    pend = jnp.cumsum(pcounts)
    pstart = pend - pcounts
    experts = jnp.arange(N_EXPERTS, dtype=jnp.int32)
    dest = jnp.sum(jnp.where(eid[:, :, None] == experts, pstart, 0), axis=-1) + pos
    p_rows = ((2 * t + N_EXPERTS * (blk - 1) + blk - 1) // blk) * blk
    nblk = p_rows // blk
    block_start = jnp.arange(nblk, dtype=jnp.int32) * blk
    block_expert = jnp.minimum(jnp.sum((pend[None, :] <= block_start[:, None]).astype(jnp.int32), axis=1),
                               N_EXPERTS - 1)
    n_active = (pend[N_EXPERTS - 1:] // blk).astype(jnp.int32)
    tail = pend[N_EXPERTS - 1] + jnp.arange(N_EXPERTS, dtype=jnp.int32) * blk
    fill_start = jnp.concatenate([jnp.where(pcounts > 0, pend - blk, -1), jnp.where(tail < p_rows, tail, -1)])
    return dest[:, 0], dest[:, 1], fill_start, block_expert, n_active, p_rows


def _layer(x, c, positions, w_ada, b_ada, w_in, w_g2, b_g2, gla_nw, lq1, lk1, lq2, lk2, diff_nw, w_out,
           ln1_w, ln1_b, w_rg, b_rg, w_re, b_re, w_eg, w_eu, w_ed, ln2_w, ln2_b, *, lam_init,
           tm_in, tn_in, gla_rows, bq, bk, tm_out, moe_blk, tm_dsp, tm_cmb):
    bsz, seq, d = x.shape
    t = bsz * seq
    alpha = (2.0 * DEPTH) ** 0.25
    gla_dk = d // (2 * GLA_HEADS)
    gla_dv = d // GLA_HEADS
    dh = d // (2 * DIFF_HEADS)
    hk = GLA_HEADS * gla_dk
    hv = GLA_HEADS * gla_dv
    dq = DIFF_HEADS * 2 * dh
    gr_col = 2 * hk + 2 * hv
    q_col, k_col = 0, hk
    v_col, g_col = 2 * hk, 2 * hk + hv
    dq_col = gr_col
    dk_col = dq_col + dq
    dv_col = dk_col + dq
    mg_col = dv_col + dq
    w_main = jnp.concatenate([w_in[:, :gr_col], w_in[:, gr_col + GLA_GATE_RANK:]], axis=1).astype(BF16)
    w_gr = jnp.pad(w_in[:, gr_col:gr_col + GLA_GATE_RANK], ((0, 0), (0, LANES - GLA_GATE_RANK))).astype(BF16)
    w_g2p = jnp.pad(w_g2, ((0, LANES - GLA_GATE_RANK), (0, 0)))
    x2 = x.reshape(t, d)

    ada = _ada(c, w_ada, b_ada.reshape(1, -1))
    mod = ada.reshape(bsz, 6, d)

    half = dh // 2
    inv = ROPE_THETA ** (-jnp.arange(half, dtype=F32) / half)
    inv_row = jnp.tile(inv, LANES // half).reshape(1, LANES)
    cs = _rope_tables(positions.reshape(t, 1), inv_row, tm_in)

    proj, log_g = _inproj(x2, mod, cs, w_main, w_gr, w_g2p, b_g2.reshape(1, -1), seq=seq, tm=tm_in, tn=tn_in,
                          q_tile=dq_col // tn_in, k_tile=dk_col // tn_in, q_scale=dh ** -0.5 * LOG2_E)

    o_gla = _gla(proj, log_g, gla_nw.reshape(1, -1), bsz=bsz, seq=seq, rows=gla_rows, dk=gla_dk, dv=gla_dv,
                 q_col=q_col, k_col=k_col, v_col=v_col, g_col=g_col)

    lam_vecs = jnp.pad(jnp.stack([lq1, lk1, lq2, lk2]), ((0, 4), (0, LANES - dh)))
    o_diff = _diff(lam_vecs, proj, diff_nw.reshape(1, -1), bsz=bsz, seq=seq, bq=bq, bk=bk, dh=dh,
                   q_col=dq_col, k_col=dk_col, v_col=dv_col, lam_init=lam_init)

    w_r = jnp.pad(jnp.concatenate([w_rg, w_re], axis=1), ((0, 0), (0, LANES - N_GROUPS - N_EXPERTS))).astype(BF16)
    b_r = jnp.pad(jnp.concatenate([b_rg, b_re]), (0, LANES - N_GROUPS - N_EXPERTS)).reshape(1, LANES)
    x1, u2_tiles, route, counts = _outproj(o_gla, o_diff, proj, x2, mod, w_out.astype(BF16), ln1_w.reshape(1, -1),
                                           ln1_b.reshape(1, -1), w_r, b_r, seq=seq, tm=tm_out, ga_col=mg_col,
                                           alpha=alpha)

    dest0, dest1, fill_start, block_expert, n_active, p_rows = _dispatch_plan(route, counts[0], moe_blk)
    xs_tiles = _dispatch(fill_start, dest0.reshape(-1, 1, tm_dsp), dest1.reshape(-1, 1, tm_dsp), u2_tiles,
                         p_rows=p_rows, tm=tm_dsp, blk=moe_blk)
    rows = _experts(block_expert, n_active, xs_tiles, w_eg, w_eu, w_ed, blk=moe_blk)
    out = _combine(dest0.reshape(-1, 1, tm_cmb), dest1.reshape(-1, 1, tm_cmb), rows, x1, route, mod,
                   ln2_w.reshape(1, -1), ln2_b.reshape(1, -1), seq=seq, tm=tm_cmb, alpha=alpha)
    return out.reshape(bsz, seq, d)


def kernel(x, c, positions, w_ada, b_ada, w_in, w_gla_gate2, b_gla_gate2, gla_norm_w, diff_lambda_q1,
           diff_lambda_k1, diff_lambda_q2, diff_lambda_k2, diff_norm_w, w_out, ln1_w, ln1_b, w_router_group,
           b_router_group, w_router_expert, b_router_expert, w_exp_gate, w_exp_up, w_exp_down, ln2_w, ln2_b):
    assert w_ada.shape[0] == DEPTH
    for l in range(DEPTH):
        lam_init = 0.8 - 0.6 * math.exp(-0.3 * l)
        x = _layer(x, c, positions, w_ada[l], b_ada[l], w_in[l], w_gla_gate2[l], b_gla_gate2[l], gla_norm_w[l],
                   diff_lambda_q1[l], diff_lambda_k1[l], diff_lambda_q2[l], diff_lambda_k2[l], diff_norm_w[l],
                   w_out[l], ln1_w[l], ln1_b[l], w_router_group[l], b_router_group[l], w_router_expert[l],
                   b_router_expert[l], w_exp_gate[l], w_exp_up[l], w_exp_down[l], ln2_w[l], ln2_b[l],
                   lam_init=lam_init, tm_in=1024, tn_in=1024, gla_rows=512, bq=512, bk=512, tm_out=512,
                   moe_blk=256, tm_dsp=512, tm_cmb=256)
    return x
```

```python
import functools
import math

import jax
import jax.numpy as jnp
from jax import lax
from jax.experimental import pallas as pl
from jax.experimental.pallas import tpu as pltpu

F32 = jnp.float32
BF16 = jnp.bfloat16
HIGHEST = lax.Precision.HIGHEST

DEPTH = 1
GLA_HEADS = 4
GLA_GATE_RANK = 16
GLA_TAU = 16.0
GLA_CHUNK = 64
GLA_SUB_ROWS = 512
DIFF_HEADS = 8
ROPE_THETA = 10000.0
N_GROUPS = 4
EXPERTS_PER_GROUP = 8
N_EXPERTS = N_GROUPS * EXPERTS_PER_GROUP
LN_EPS = 1e-5
RMS_EPS = 1e-6
LOG2_E = math.log2(math.e)
LANES = 128
VMEM_LIMIT = 56 * 1024 * 1024

NT_DIMS = (((1,), (1,)), ((), ()))
TN_DIMS = (((0,), (0,)), ((), ()))


def _layer_norm(x):
    mu = jnp.mean(x, axis=-1, keepdims=True)
    xc = x - mu
    var = jnp.mean(xc * xc, axis=-1, keepdims=True)
    return xc * lax.rsqrt(var + LN_EPS)


def _silu(x):
    return x * jax.nn.sigmoid(x)


def _ada_kernel(c_ref, w_ref, b_ref, o_ref):
    s = _silu(c_ref[...])
    o_ref[...] = jnp.dot(s, w_ref[...], preferred_element_type=F32, precision=HIGHEST) + b_ref[...]


def _ada(c, w, b):
    bsz, d = c.shape
    n = w.shape[1]
    tn = d
    return pl.pallas_call(
        _ada_kernel,
        grid=(n // tn,),
        in_specs=[pl.BlockSpec((bsz, d), lambda j: (0, 0)),
                  pl.BlockSpec((d, tn), lambda j: (0, j)),
                  pl.BlockSpec((1, tn), lambda j: (0, j))],
        out_specs=pl.BlockSpec((bsz, tn), lambda j: (0, j)),
        out_shape=jax.ShapeDtypeStruct((bsz, n), F32),
        name="ada",
    )(c, w, b)


def _rope_kernel(pos_ref, inv_ref, cs_ref):
    ang = pos_ref[...].astype(F32) * inv_ref[...]
    lane = lax.broadcasted_iota(jnp.int32, ang.shape, 1)
    sin = jnp.sin(ang)
    cs_ref[:, :LANES] = jnp.cos(ang)
    cs_ref[:, LANES:] = jnp.where(lane < LANES // 2, -sin, sin)


def _rope_tables(pos_col, inv_row, tm):
    t = pos_col.shape[0]
    return pl.pallas_call(
        _rope_kernel,
        grid=(t // tm,),
        in_specs=[pl.BlockSpec((tm, 1), lambda i: (i, 0)),
                  pl.BlockSpec((1, LANES), lambda i: (0, 0))],
        out_specs=pl.BlockSpec((tm, 2 * LANES), lambda i: (i, 0)),
        out_shape=jax.ShapeDtypeStruct((t, 2 * LANES), F32),
        name="rope_tables",
    )(pos_col, inv_row)


def _inproj_kernel(x_ref, mod_ref, cs_ref, w_ref, wgr_ref, wg2_ref, bg2_ref, o_ref, lg_ref, u_ref,
                   *, q_tile, k_tile, q_scale):
    j = pl.program_id(1)

    @pl.when(j == 0)
    def _():
        shift = mod_ref[0, 0:1, :]
        scale = mod_ref[0, 1:2, :]
        u = (_layer_norm(x_ref[...]) * (1.0 + scale) + shift).astype(BF16)
        u_ref[...] = u
        gr = jnp.dot(u, wgr_ref[...], preferred_element_type=F32)
        z = jnp.dot(gr.astype(BF16), wg2_ref[...], preferred_element_type=F32) + bg2_ref[...]
        log_sig = jnp.minimum(z, 0.0) - jnp.log(1.0 + jnp.exp(-jnp.abs(z)))
        lg_ref[...] = log_sig * (1.0 / GLA_TAU)

    acc = jnp.dot(u_ref[...], w_ref[...], preferred_element_type=F32)
    is_rope = jnp.logical_or(j == q_tile, j == k_tile)

    @pl.when(is_rope)
    def _():
        tn = acc.shape[1]
        sc = jnp.where(j == q_tile, q_scale, 1.0).astype(F32)
        cos = cs_ref[:, :LANES] * sc
        sin = cs_ref[:, LANES:] * sc
        for h in range(tn // LANES):
            t = acc[:, h * LANES:(h + 1) * LANES]
            partner = pltpu.roll(t, LANES // 2, 1)
            o_ref[:, h * LANES:(h + 1) * LANES] = (t * cos + partner * sin).astype(o_ref.dtype)

    @pl.when(jnp.logical_not(is_rope))
    def _():
        o_ref[...] = acc.astype(o_ref.dtype)


def _inproj(x2, mod, cs, w_main, w_gr, w_g2, b_g2, *, seq, tm, tn, q_tile, k_tile, q_scale):
    t, d = x2.shape
    n = w_main.shape[1]
    ng = w_g2.shape[1]
    kern = functools.partial(_inproj_kernel, q_tile=q_tile, k_tile=k_tile, q_scale=q_scale)
    return pl.pallas_call(
        kern,
        grid=(t // tm, n // tn),
        in_specs=[pl.BlockSpec((tm, d), lambda i, j: (i, 0)),
                  pl.BlockSpec((1,) + mod.shape[1:], lambda i, j: ((i * tm) // seq, 0, 0)),
                  pl.BlockSpec((tm, 2 * LANES), lambda i, j: (i, 0)),
                  pl.BlockSpec((d, tn), lambda i, j: (0, j)),
                  pl.BlockSpec((d, LANES), lambda i, j: (0, 0)),
                  pl.BlockSpec((LANES, ng), lambda i, j: (0, 0)),
                  pl.BlockSpec((1, ng), lambda i, j: (0, 0))],
        out_specs=[pl.BlockSpec((tm, tn), lambda i, j: (i, j)),
                   pl.BlockSpec((tm, ng), lambda i, j: (i, 0))],
        out_shape=[jax.ShapeDtypeStruct((t, n), BF16),
                   jax.ShapeDtypeStruct((t, ng), F32)],
        scratch_shapes=[pltpu.VMEM((tm, d), BF16)],
        compiler_params=pltpu.CompilerParams(
            dimension_semantics=("arbitrary", "arbitrary"), vmem_limit_bytes=VMEM_LIMIT),
        name="inproj",
    )(x2, mod, cs, w_main, w_gr, w_g2, b_g2)


def _gla_kernel(q_ref, k_ref, v_ref, g_ref, lg_ref, nw_ref, o_ref, st_ref, *, chunk, sub_rows, q_scale):
    @pl.when(pl.program_id(2) == 0)
    def _():
        st_ref[...] = jnp.zeros_like(st_ref)

    n_chunks = sub_rows // chunk
    pos = lax.broadcasted_iota(jnp.int32, (sub_rows, lg_ref.shape[1]), 0) % chunk
    r = lax.broadcasted_iota(jnp.int32, (sub_rows, sub_rows), 0)
    c = lax.broadcasted_iota(jnp.int32, (sub_rows, sub_rows), 1)
    keep = jnp.logical_and(c <= r, (r // chunk) == (c // chunk))
    st = st_ref[...]

    for si in range(q_ref.shape[0] // sub_rows):
        rs = slice(si * sub_rows, (si + 1) * sub_rows)

        b = lg_ref[rs, :]
        step = 1
        while step < chunk:
            b = b + jnp.where(pos >= step, pltpu.roll(b, step, 0), 0.0)
            step *= 2

        k = k_ref[rs, :].astype(F32)
        q_t = (q_ref[rs, :].astype(F32) * q_scale * jnp.exp(b)).astype(BF16)
        k_t = (k * jnp.exp(-b)).astype(BF16)
        v = v_ref[rs, :]

        attn = lax.dot_general(q_t, k_t, NT_DIMS, preferred_element_type=F32)
        o = jnp.dot(jnp.where(keep, attn, 0.0).astype(BF16), v, preferred_element_type=F32)

        o_inter = []
        for ci in range(n_chunks):
            sl = slice(ci * chunk, (ci + 1) * chunk)
            b_c = b[sl, :]
            b_last = b_c[chunk - 1:chunk, :]
            k_d = (k[sl, :] * jnp.exp(b_last - b_c)).astype(BF16)
            kv = lax.dot_general(v[sl, :], k_d, TN_DIMS, preferred_element_type=F32)
            o_inter.append(lax.dot_general(q_t[sl, :], st.astype(BF16), NT_DIMS, preferred_element_type=F32))
            st = st * jnp.exp(b_last) + kv
        o = o + jnp.concatenate(o_inter, axis=0)

        ms = jnp.mean(o * o, axis=-1, keepdims=True)
        y = o * lax.rsqrt(ms + RMS_EPS) * nw_ref[...] * _silu(g_ref[rs, :].astype(F32))
        o_ref[rs, :] = y.astype(o_ref.dtype)

    st_ref[...] = st


def _gla(proj, log_g, norm_w, *, bsz, seq, rows, dk, dv, q_col, k_col, v_col, g_col):
    t = proj.shape[0]
    nl = seq // rows
    kern = functools.partial(_gla_kernel, chunk=GLA_CHUNK, sub_rows=min(rows, GLA_SUB_ROWS), q_scale=dk ** -0.5)
    row = lambda b, h, l: b * nl + l
    return pl.pallas_call(
        kern,
        grid=(bsz, GLA_HEADS, nl),
        in_specs=[pl.BlockSpec((rows, dk), lambda b, h, l: (row(b, h, l), q_col // dk + h)),
                  pl.BlockSpec((rows, dk), lambda b, h, l: (row(b, h, l), k_col // dk + h)),
                  pl.BlockSpec((rows, dv), lambda b, h, l: (row(b, h, l), v_col // dv + h)),
                  pl.BlockSpec((rows, dv), lambda b, h, l: (row(b, h, l), g_col // dv + h)),
                  pl.BlockSpec((rows, dk), lambda b, h, l: (row(b, h, l), h)),
                  pl.BlockSpec((1, dv), lambda b, h, l: (0, 0))],
        out_specs=pl.BlockSpec((rows, dv), lambda b, h, l: (row(b, h, l), h)),
        out_shape=jax.ShapeDtypeStruct((t, GLA_HEADS * dv), BF16),
        scratch_shapes=[pltpu.VMEM((dv, dk), F32)],
        compiler_params=pltpu.CompilerParams(
            dimension_semantics=("arbitrary", "arbitrary", "arbitrary"), vmem_limit_bytes=VMEM_LIMIT),
        name="gla",
    )(proj, proj, proj, proj, log_g, norm_w)


def _diff_kernel(lam_ref, q_ref, qn_ref, k_ref, v_ref, nw_ref, o_ref,
                 qs_ref, vt_ref, s0_ref, sa_ref, sb_ref, m_ref, l_ref, acc_ref, *, bq, bk, dh, lam_init):
    qi = pl.program_id(2)
    q0 = qi * bq
    seq = k_ref.shape[0]

    def mask_queries(q):
        lane = lax.broadcasted_iota(jnp.int32, q.shape, 1)
        zero = jnp.zeros_like(q)
        map0 = (lane % dh) < dh // 2
        qs_ref[:bq, :] = jnp.where(map0, q, zero)
        qs_ref[bq:, :] = jnp.where(map0, zero, q)

    def scores(kj, dst):
        k0 = pl.multiple_of(kj * bk, bk)
        dst[...] = lax.dot_general(k_ref[pl.ds(k0, bk), :], qs_ref[...], NT_DIMS, preferred_element_type=F32)

    def update(src, kj, masked):
        k0 = pl.multiple_of(kj * bk, bk)
        s = src[...]
        if masked:
            key = k0 + lax.broadcasted_iota(jnp.int32, s.shape, 0)
            col = lax.broadcasted_iota(jnp.int32, s.shape, 1)
            qpos = q0 + jnp.where(col >= bq, col - bq, col)
            s = jnp.where(key <= qpos, s, -jnp.inf)
        m_prev = m_ref[...]
        m_new = jnp.maximum(m_prev, jnp.max(s, axis=0, keepdims=True))
        alpha = jnp.exp2(m_prev - m_new)
        p = jnp.exp2(s - m_new)
        l_ref[...] = alpha * l_ref[...] + jnp.sum(p, axis=0, keepdims=True)
        acc_ref[...] = alpha * acc_ref[...] + jnp.dot(
            vt_ref[:, pl.ds(k0, bk)], p.astype(BF16), preferred_element_type=F32)
        m_ref[...] = m_new

    @pl.when(qi == 0)
    def _():
        for c in range(seq // bk):
            vt_ref[:, c * bk:(c + 1) * bk] = v_ref[c * bk:(c + 1) * bk, :].astype(F32).T.astype(BF16)
        mask_queries(q_ref[...])
        scores(0, s0_ref)

    m_ref[...] = jnp.full_like(m_ref, -jnp.inf)
    l_ref[...] = jnp.zeros_like(l_ref)
    acc_ref[...] = jnp.zeros_like(acc_ref)

    n_full = (q0 + 1) // bk

    @pl.when(n_full == 0)
    def _():
        update(s0_ref, 0, True)

    @pl.when(n_full > 0)
    def _():
        scores(1, sa_ref)
        update(s0_ref, 0, False)
        rest = n_full - 1

        def pair_body(jj, carry):
            j = 1 + 2 * jj
            scores(j + 1, sb_ref)
            update(sa_ref, j, False)
            scores(j + 2, sa_ref)
            update(sb_ref, j + 1, False)
            return carry

        lax.fori_loop(0, rest // 2, pair_body, 0)
        odd = (rest % 2) == 1

        @pl.when(odd)
        def _():
            scores(n_full, sb_ref)
            update(sa_ref, n_full - 1, False)
            update(sb_ref, n_full, True)

        @pl.when(jnp.logical_not(odd))
        def _():
            update(sa_ref, n_full, True)

    mask_queries(qn_ref[...])
    scores(0, s0_ref)

    lv = lam_ref[...]
    s1 = jnp.sum(lv[0:1, :] * lv[1:2, :], axis=-1, keepdims=True)
    s2 = jnp.sum(lv[2:3, :] * lv[3:4, :], axis=-1, keepdims=True)
    lam = jnp.exp(s1) - jnp.exp(s2) + lam_init
    o_all = acc_ref[...] / l_ref[...]
    o_t = o_all[:, :bq] - lam * o_all[:, bq:]
    ms = jnp.mean(o_t * o_t, axis=0, keepdims=True)
    y = (o_t * lax.rsqrt(ms + RMS_EPS)).T * nw_ref[...] * (1.0 - lam_init)
    o_ref[...] = y.astype(o_ref.dtype)


def _diff(lam_vecs, proj, norm_w, *, bsz, seq, bq, bk, dh, q_col, k_col, v_col, lam_init):
    assert bk % bq == 0 and seq % bk == 0
    t = proj.shape[0]
    w = 2 * dh
    nq = seq // bq
    kern = functools.partial(_diff_kernel, bq=bq, bk=bk, dh=dh, lam_init=lam_init)
    return pl.pallas_call(
        kern,
        grid=(bsz, DIFF_HEADS, nq),
        in_specs=[pl.BlockSpec(lam_vecs.shape, lambda b, h, i: (0, 0)),
                  pl.BlockSpec((bq, w), lambda b, h, i: (b * nq + i, q_col // w + h)),
                  pl.BlockSpec((bq, w), lambda b, h, i: (b * nq + jnp.minimum(i + 1, nq - 1), q_col // w + h)),
                  pl.BlockSpec((seq, w), lambda b, h, i: (b, k_col // w + h)),
                  pl.BlockSpec((seq, w), lambda b, h, i: (b, v_col // w + h)),
                  pl.BlockSpec((1, w), lambda b, h, i: (0, 0))],
        out_specs=pl.BlockSpec((bq, w), lambda b, h, i: (b * nq + i, h)),
        out_shape=jax.ShapeDtypeStruct((t, DIFF_HEADS * w), BF16),
        scratch_shapes=[pltpu.VMEM((2 * bq, w), BF16),
                        pltpu.VMEM((w, seq), BF16),
                        pltpu.VMEM((bk, 2 * bq), F32),
                        pltpu.VMEM((bk, 2 * bq), F32),
                        pltpu.VMEM((bk, 2 * bq), F32),
                        pltpu.VMEM((1, 2 * bq), F32),
                        pltpu.VMEM((1, 2 * bq), F32),
                        pltpu.VMEM((w, 2 * bq), F32)],
        compiler_params=pltpu.CompilerParams(
            dimension_semantics=("arbitrary", "arbitrary", "arbitrary"), vmem_limit_bytes=VMEM_LIMIT),
        name="diff_attn",
    )(lam_vecs, proj, proj, proj, proj, norm_w)


ROW_SUB = 8


def _store_row_tiles(ref, val):
    rows = val.shape[0]
    for s in range(ROW_SUB):
        ref[pl.ds(s, rows, stride=ROW_SUB), :] = val[:, s * LANES:(s + 1) * LANES]


def _load_row_tiles(ref, rows):
    return jnp.concatenate([ref[pl.ds(s, rows, stride=ROW_SUB), :] for s in range(ROW_SUB)], axis=1)


def _tile_copy(src_hbm, dst_vmem, sem, src_row, dst_row):
    return pltpu.make_async_copy(src_hbm.at[pl.ds(src_row * ROW_SUB, ROW_SUB), :],
                                 dst_vmem.at[pl.ds(dst_row * ROW_SUB, ROW_SUB), :], sem)


def _wait_tiles(src_hbm, dst_vmem, sem):
    pltpu.make_async_copy(src_hbm.at[pl.ds(0, dst_vmem.shape[0]), :], dst_vmem, sem).wait()


def _outproj_kernel(og_ref, od_ref, ga_ref, gb_ref, x_ref, mod_ref, wo_ref, lnw_ref, lnb_ref, wr_ref, br_ref,
                    x1_ref, u2_ref, rt_ref, cnt_ref, tri_ref, run_ref, *, alpha):
    @pl.when(pl.program_id(0) == 0)
    def _():
        r = lax.broadcasted_iota(jnp.int32, tri_ref.shape, 0)
        c = lax.broadcasted_iota(jnp.int32, tri_ref.shape, 1)
        tri_ref[...] = (c < r).astype(BF16)
        run_ref[...] = jnp.zeros_like(run_ref)

    merged = jax.nn.sigmoid(ga_ref[...]) * og_ref[...] + jax.nn.sigmoid(gb_ref[...]) * od_ref[...]
    y = jnp.dot(merged, wo_ref[...], preferred_element_type=F32)
    gate1 = mod_ref[0, 2:3, :]
    shift2 = mod_ref[0, 3:4, :]
    scale2 = mod_ref[0, 4:5, :]
    x1 = _layer_norm(alpha * x_ref[...] + gate1 * y) * lnw_ref[...] + lnb_ref[...]
    x1_ref[...] = x1
    u2 = _layer_norm(x1) * (1.0 + scale2) + shift2
    _store_row_tiles(u2_ref, u2)

    logits = jnp.dot(u2.astype(BF16), wr_ref[...], preferred_element_type=F32) + br_ref[...]
    lane = lax.broadcasted_iota(jnp.int32, logits.shape, 1)
    neg = jnp.float32(-jnp.inf)
    big = jnp.int32(LANES)
    lg = jnp.where(lane < N_GROUPS, logits, neg)
    g_max = jnp.max(lg, axis=-1, keepdims=True)
    w_grp = 1.0 / jnp.sum(jnp.exp(lg - g_max), axis=-1, keepdims=True)
    g_idx = jnp.min(jnp.where(lg == g_max, lane, big), axis=-1, keepdims=True)
    lo = N_GROUPS + EXPERTS_PER_GROUP * g_idx
    le = jnp.where(jnp.logical_and(lane >= lo, lane < lo + EXPERTS_PER_GROUP), logits, neg)
    v1 = jnp.max(le, axis=-1, keepdims=True)
    i1 = jnp.min(jnp.where(le == v1, lane, big), axis=-1, keepdims=True)
    le2 = jnp.where(lane == i1, neg, le)
    v2 = jnp.max(le2, axis=-1, keepdims=True)
    i2 = jnp.min(jnp.where(le2 == v2, lane, big), axis=-1, keepdims=True)
    e2 = jnp.exp(v2 - v1)
    den = 1.0 + e2
    c1 = w_grp / den
    c2 = w_grp * e2 / den

    sel1 = lane == i1 - N_GROUPS
    sel2 = lane == i2 - N_GROUPS
    onehot = jnp.logical_or(sel1, sel2).astype(BF16)
    before = jnp.dot(tri_ref[...], onehot, preferred_element_type=F32) + run_ref[...]
    pos1 = jnp.sum(jnp.where(sel1, before, 0.0), axis=-1, keepdims=True)
    pos2 = jnp.sum(jnp.where(sel2, before, 0.0), axis=-1, keepdims=True)
    run_ref[...] += jnp.sum(onehot.astype(F32), axis=0, keepdims=True)
    cnt_ref[...] = jnp.broadcast_to(run_ref[...], cnt_ref.shape)

    cols = ((i1 - N_GROUPS).astype(F32), (i2 - N_GROUPS).astype(F32), c1, c2, pos1, pos2)
    rt = jnp.zeros_like(logits)
    for li, col in enumerate(cols):
        rt = jnp.where(lane == li, col, rt)
    rt_ref[...] = rt


def _outproj(o_gla, o_diff, proj, x2, mod, w_out, ln_w, ln_b, w_r, b_r, *, seq, tm, ga_col, alpha):
    t, d = x2.shape
    assert d == ROW_SUB * LANES
    kern = functools.partial(_outproj_kernel, alpha=alpha)
    row_spec = pl.BlockSpec((tm, d), lambda i: (i, 0))
    vec_spec = pl.BlockSpec((1, d), lambda i: (0, 0))
    return pl.pallas_call(
        kern,
        grid=(t // tm,),
        in_specs=[row_spec, row_spec,
                  pl.BlockSpec((tm, d), lambda i: (i, ga_col // d)),
                  pl.BlockSpec((tm, d), lambda i: (i, ga_col // d + 1)),
                  row_spec,
                  pl.BlockSpec((1,) + mod.shape[1:], lambda i: ((i * tm) // seq, 0, 0)),
                  pl.BlockSpec((d, d), lambda i: (0, 0)),
                  vec_spec, vec_spec,
                  pl.BlockSpec((d, LANES), lambda i: (0, 0)),
                  pl.BlockSpec((1, LANES), lambda i: (0, 0))],
        out_specs=[row_spec,
                   pl.BlockSpec((tm * ROW_SUB, LANES), lambda i: (i, 0)),
                   pl.BlockSpec((tm, LANES), lambda i: (i, 0)),
                   pl.BlockSpec((ROW_SUB, LANES), lambda i: (0, 0))],
        out_shape=[jax.ShapeDtypeStruct((t, d), F32),
                   jax.ShapeDtypeStruct((t * ROW_SUB, LANES), F32),
                   jax.ShapeDtypeStruct((t, LANES), F32),
                   jax.ShapeDtypeStruct((ROW_SUB, LANES), F32)],
        scratch_shapes=[pltpu.VMEM((tm, tm), BF16),
                        pltpu.VMEM((1, LANES), F32)],
        compiler_params=pltpu.CompilerParams(
            dimension_semantics=("arbitrary",), vmem_limit_bytes=VMEM_LIMIT),
        name="outproj",
    )(o_gla, o_diff, proj, proj, x2, mod, w_out, ln_w, ln_b, w_r, b_r)


def _dispatch_kernel(fill_ref, d0_ref, d1_ref, u_ref, xs_hbm, zbuf, sem_fill, sem, *, tm, blk):
    i = pl.program_id(0)

    @pl.when(i == 0)
    def _():
        zbuf[...] = jnp.zeros_like(zbuf)
        n_fill = fill_ref.shape[0]

        def fill_copy(j):
            return pltpu.make_async_copy(zbuf, xs_hbm.at[pl.ds(fill_ref[j] * ROW_SUB, blk * ROW_SUB), :], sem_fill)

        def start_fill(j, carry):
            @pl.when(fill_ref[j] >= 0)
            def _():
                fill_copy(j).start()
            return carry

        def wait_fill(j, carry):
            @pl.when(fill_ref[j] >= 0)
            def _():
                fill_copy(j).wait()
            return carry

        lax.fori_loop(0, n_fill, start_fill, 0)
        lax.fori_loop(0, n_fill, wait_fill, 0)

    def row_copy(dest_ref, r):
        return pltpu.make_async_copy(u_ref.at[pl.ds(r * ROW_SUB, ROW_SUB), :],
                                     xs_hbm.at[pl.ds(dest_ref[0, 0, r] * ROW_SUB, ROW_SUB), :], sem)

    def body(r, carry):
        row_copy(d0_ref, r).start(priority=0)
        row_copy(d1_ref, r).start(priority=1)
        return carry

    lax.fori_loop(0, tm, body, 0, unroll=4)
    for _ in range(2):
        pltpu.make_async_copy(u_ref, xs_hbm.at[pl.ds(0, tm * ROW_SUB), :], sem).wait()


def _dispatch(fill_start, dest0, dest1, u2_tiles, *, p_rows, tm, blk):
    t = u2_tiles.shape[0] // ROW_SUB
    kern = functools.partial(_dispatch_kernel, tm=tm, blk=blk)
    idx_spec = pl.BlockSpec((1, 1, tm), lambda i, fs: (i, 0, 0), memory_space=pltpu.SMEM)
    grid_spec = pltpu.PrefetchScalarGridSpec(
        num_scalar_prefetch=1,
        grid=(t // tm,),
        in_specs=[idx_spec, idx_spec, pl.BlockSpec((tm * ROW_SUB, LANES), lambda i, fs: (i, 0))],
        out_specs=pl.BlockSpec(memory_space=pl.ANY),
        scratch_shapes=[pltpu.VMEM((blk * ROW_SUB, LANES), F32),
                        pltpu.SemaphoreType.DMA(()),
                        pltpu.SemaphoreType.DMA(())],
    )
    return pl.pallas_call(
        kern,
        grid_spec=grid_spec,
        out_shape=jax.ShapeDtypeStruct((p_rows * ROW_SUB, LANES), F32),
        compiler_params=pltpu.CompilerParams(dimension_semantics=("arbitrary",)),
        name="dispatch",
    )(fill_start, dest0, dest1, u2_tiles)


def _expert_kernel(be_ref, na_ref, x_ref, wg_ref, wu_ref, wd_ref, o_ref, wgb, wub, wdb, *, blk):
    i = pl.program_id(0)
    active = i < na_ref[0]

    @pl.when(active)
    def _():
        changed = jnp.logical_or(i == 0, be_ref[i] != be_ref[jnp.maximum(i - 1, 0)])

        @pl.when(changed)
        def _():
            wgb[...] = wg_ref[0].astype(BF16)
            wub[...] = wu_ref[0].astype(BF16)
            wdb[...] = wd_ref[0].astype(BF16)

        xb = _load_row_tiles(x_ref, blk).astype(BF16)
        hg = jnp.dot(xb, wgb[...], preferred_element_type=F32)
        hu = jnp.dot(xb, wub[...], preferred_element_type=F32)
        h = (_silu(hg) * hu).astype(BF16)
        _store_row_tiles(o_ref, jnp.dot(h, wdb[...], preferred_element_type=F32))

    @pl.when(jnp.logical_not(active))
    def _():
        o_ref[...] = jnp.zeros_like(o_ref)


def _experts(block_expert, n_active, xs_tiles, w_gate, w_up, w_down, *, blk):
    nblk = block_expert.shape[0]
    e, d, ff = w_gate.shape
    kern = functools.partial(_expert_kernel, blk=blk)
    grid_spec = pltpu.PrefetchScalarGridSpec(
        num_scalar_prefetch=2,
        grid=(nblk,),
        in_specs=[pl.BlockSpec((blk * ROW_SUB, LANES), lambda i, be, na: (jnp.minimum(i, na[0] - 1), 0)),
                  pl.BlockSpec((1, d, ff), lambda i, be, na: (be[i], 0, 0)),
                  pl.BlockSpec((1, d, ff), lambda i, be, na: (be[i], 0, 0)),
                  pl.BlockSpec((1, ff, d), lambda i, be, na: (be[i], 0, 0))],
        out_specs=pl.BlockSpec((blk * ROW_SUB, LANES), lambda i, be, na: (i, 0)),
        scratch_shapes=[pltpu.VMEM((d, ff), BF16),
                        pltpu.VMEM((d, ff), BF16),
                        pltpu.VMEM((ff, d), BF16)],
    )
    return pl.pallas_call(
        kern,
        grid_spec=grid_spec,
        out_shape=jax.ShapeDtypeStruct((nblk * blk * ROW_SUB, LANES), F32),
        compiler_params=pltpu.CompilerParams(
            dimension_semantics=("arbitrary",), vmem_limit_bytes=VMEM_LIMIT),
        name="experts",
    )(block_expert, n_active, xs_tiles, w_gate, w_up, w_down)


def _combine_kernel(d0_ref, d0n_ref, d1_ref, d1n_ref, rows_hbm, x1_ref, rt_ref, mod_ref, lnw_ref, lnb_ref,
                    o_ref, buf, sem, *, tm, alpha):
    i = pl.program_id(0)
    n = pl.num_programs(0)
    slot = i % 2

    def start_gather(i0_ref, i1_ref, s):
        def body(r, carry):
            _tile_copy(rows_hbm, buf.at[s, 0], sem.at[s], i0_ref[0, 0, r], r).start(priority=0)
            _tile_copy(rows_hbm, buf.at[s, 1], sem.at[s], i1_ref[0, 0, r], r).start(priority=1)
            return carry
        lax.fori_loop(0, tm, body, 0, unroll=4)

    def wait_gather(s):
        _wait_tiles(rows_hbm, buf.at[s, 0], sem.at[s])
        _wait_tiles(rows_hbm, buf.at[s, 1], sem.at[s])

    @pl.when(i == 0)
    def _():
        start_gather(d0_ref, d1_ref, 0)

    start_gather(d0n_ref, d1n_ref, 1 - slot)
    wait_gather(slot)
    rt = rt_ref[...]
    y = rt[:, 2:3] * _load_row_tiles(buf.at[slot, 0], tm) + rt[:, 3:4] * _load_row_tiles(buf.at[slot, 1], tm)
    gate2 = mod_ref[0, 5:6, :]
    z = alpha * x1_ref[...] + gate2 * y
    o_ref[...] = _layer_norm(z) * lnw_ref[...] + lnb_ref[...]

    @pl.when(i == n - 1)
    def _():
        wait_gather(1 - slot)


def _combine(dest0, dest1, rows_tiles, x1, route, mod, ln_w, ln_b, *, seq, tm, alpha):
    t, d = x1.shape
    nt = t // tm
    kern = functools.partial(_combine_kernel, tm=tm, alpha=alpha)
    cur = lambda i: (i, 0, 0)
    nxt = lambda i: (jnp.minimum(i + 1, nt - 1), 0, 0)
    idx_spec = lambda m: pl.BlockSpec((1, 1, tm), m, memory_space=pltpu.SMEM)
    row_spec = pl.BlockSpec((tm, d), lambda i: (i, 0))
    vec_spec = pl.BlockSpec((1, d), lambda i: (0, 0))
    return pl.pallas_call(
        kern,
        grid=(nt,),
        in_specs=[idx_spec(cur), idx_spec(nxt), idx_spec(cur), idx_spec(nxt),
                  pl.BlockSpec(memory_space=pl.ANY),
                  row_spec,
                  pl.BlockSpec((tm, LANES), lambda i: (i, 0)),
                  pl.BlockSpec((1,) + mod.shape[1:], lambda i: ((i * tm) // seq, 0, 0)),
                  vec_spec, vec_spec],
        out_specs=row_spec,
        out_shape=jax.ShapeDtypeStruct((t, d), F32),
        scratch_shapes=[pltpu.VMEM((2, 2, tm * ROW_SUB, LANES), F32),
                        pltpu.SemaphoreType.DMA((2,))],
        compiler_params=pltpu.CompilerParams(
            dimension_semantics=("arbitrary",), vmem_limit_bytes=VMEM_LIMIT),
        name="combine",
    )(dest0, dest0, dest1, dest1, rows_tiles, x1, route, mod, ln_w, ln_b)


def _dispatch_plan(route, counts, blk):
    t = route.shape[0]
    eid = route[:, :2].astype(jnp.int32)
    pos = route[:, 4:6].astype(jnp.int32)
    counts = counts[:N_EXPERTS].astype(jnp.int32)
    pcounts = ((counts + blk - 1) // blk) * blk
    pend = jnp.cumsum(pcounts)
    pstart = pend - pcounts
    experts = jnp.arange(N_EXPERTS, dtype=jnp.int32)
    dest = jnp.sum(jnp.where(eid[:, :, None] == experts, pstart, 0), axis=-1) + pos
    p_rows = ((2 * t + N_EXPERTS * (blk - 1) + blk - 1) // blk) * blk
    nblk = p_rows // blk
    block_start = jnp.arange(nblk, dtype=jnp.int32) * blk
    block_expert = jnp.minimum(jnp.sum((pend[None, :] <= block_start[:, None]).astype(jnp.int32), axis=1),
                               N_EXPERTS - 1)
    n_active = (pend[N_EXPERTS - 1:] // blk).astype(jnp.int32)
    tail = pend[N_EXPERTS - 1] + jnp.arange(N_EXPERTS, dtype=jnp.int32) * blk
    fill_start = jnp.concatenate([jnp.where(pcounts > 0, pend - blk, -1), jnp.where(tail < p_rows, tail, -1)])
    return dest[:, 0], dest[:, 1], fill_start, block_expert, n_active, p_rows


def _layer(x, c, positions, w_ada, b_ada, w_in, w_g2, b_g2, gla_nw, lq1, lk1, lq2, lk2, diff_nw, w_out,
           ln1_w, ln1_b, w_rg, b_rg, w_re, b_re, w_eg, w_eu, w_ed, ln2_w, ln2_b, *, lam_init,
           tm_in, tn_in, gla_rows, bq, bk, tm_out, moe_blk, tm_dsp, tm_cmb):
    bsz, seq, d = x.shape
    t = bsz * seq
    alpha = (2.0 * DEPTH) ** 0.25
    gla_dk = d // (2 * GLA_HEADS)
    gla_dv = d // GLA_HEADS
    dh = d // (2 * DIFF_HEADS)
    hk = GLA_HEADS * gla_dk
    hv = GLA_HEADS * gla_dv
    dq = DIFF_HEADS * 2 * dh
    gr_col = 2 * hk + 2 * hv
    q_col, k_col = 0, hk
    v_col, g_col = 2 * hk, 2 * hk + hv
    dq_col = gr_col
    dk_col = dq_col + dq
    dv_col = dk_col + dq
    mg_col = dv_col + dq
    w_main = jnp.concatenate([w_in[:, :gr_col], w_in[:, gr_col + GLA_GATE_RANK:]], axis=1).astype(BF16)
    half = dh // 2
    lane = jnp.arange(2 * dh)
    in_head = (lane // half % 2) * dh + (lane // dh) * half + lane % half
    cols = (jnp.arange(DIFF_HEADS)[:, None] * 2 * dh + in_head[None, :]).reshape(-1)
    w_main = jnp.concatenate([w_main[:, :dq_col], w_main[:, dq_col + cols], w_main[:, dk_col + cols],
                              w_main[:, dv_col:]], axis=1)
    w_gr = jnp.pad(w_in[:, gr_col:gr_col + GLA_GATE_RANK], ((0, 0), (0, LANES - GLA_GATE_RANK))).astype(BF16)
    w_g2p = jnp.pad(w_g2, ((0, LANES - GLA_GATE_RANK), (0, 0))).astype(BF16)
    x2 = x.reshape(t, d)

    ada = _ada(c, w_ada, b_ada.reshape(1, -1))
    mod = ada.reshape(bsz, 6, d)

    inv = ROPE_THETA ** (-jnp.arange(half, dtype=F32) / half)
    inv_row = jnp.tile(inv, LANES // half).reshape(1, LANES)
    cs = _rope_tables(positions.reshape(t, 1), inv_row, tm_in)

    proj, log_g = _inproj(x2, mod, cs, w_main, w_gr, w_g2p, b_g2.reshape(1, -1), seq=seq, tm=tm_in, tn=tn_in,
                          q_tile=dq_col // tn_in, k_tile=dk_col // tn_in, q_scale=dh ** -0.5 * LOG2_E)

    o_gla = _gla(proj, log_g, gla_nw.reshape(1, -1), bsz=bsz, seq=seq, rows=gla_rows, dk=gla_dk, dv=gla_dv,
                 q_col=q_col, k_col=k_col, v_col=v_col, g_col=g_col)

    lam_vecs = jnp.pad(jnp.stack([lq1, lk1, lq2, lk2]), ((0, 4), (0, LANES - dh)))
    o_diff = _diff(lam_vecs, proj, diff_nw.reshape(1, -1), bsz=bsz, seq=seq, bq=bq, bk=bk, dh=dh,
                   q_col=dq_col, k_col=dk_col, v_col=dv_col, lam_init=lam_init)

    w_r = jnp.pad(jnp.concatenate([w_rg, w_re], axis=1), ((0, 0), (0, LANES - N_GROUPS - N_EXPERTS))).astype(BF16)
    b_r = jnp.pad(jnp.concatenate([b_rg, b_re]), (0, LANES - N_GROUPS - N_EXPERTS)).reshape(1, LANES)
    x1, u2_tiles, route, counts = _outproj(o_gla, o_diff, proj, x2, mod, w_out.astype(BF16), ln1_w.reshape(1, -1),
                                           ln1_b.reshape(1, -1), w_r, b_r, seq=seq, tm=tm_out, ga_col=mg_col,
                                           alpha=alpha)

    dest0, dest1, fill_start, block_expert, n_active, p_rows = _dispatch_plan(route, counts[0], moe_blk)
    xs_tiles = _dispatch(fill_start, dest0.reshape(-1, 1, tm_dsp), dest1.reshape(-1, 1, tm_dsp), u2_tiles,
                         p_rows=p_rows, tm=tm_dsp, blk=moe_blk)
    rows = _experts(block_expert, n_active, xs_tiles, w_eg, w_eu, w_ed, blk=moe_blk)
    out = _combine(dest0.reshape(-1, 1, tm_cmb), dest1.reshape(-1, 1, tm_cmb), rows, x1, route, mod,
                   ln2_w.reshape(1, -1), ln2_b.reshape(1, -1), seq=seq, tm=tm_cmb, alpha=alpha)
    return out.reshape(bsz, seq, d)


def kernel(x, c, positions, w_ada, b_ada, w_in, w_gla_gate2, b_gla_gate2, gla_norm_w, diff_lambda_q1,
           diff_lambda_k1, diff_lambda_q2, diff_lambda_k2, diff_norm_w, w_out, ln1_w, ln1_b, w_router_group,
           b_router_group, w_router_expert, b_router_expert, w_exp_gate, w_exp_up, w_exp_down, ln2_w, ln2_b):
    assert w_ada.shape[0] == DEPTH
    for l in range(DEPTH):
        lam_init = 0.8 - 0.6 * math.exp(-0.3 * l)
        x = _layer(x, c, positions, w_ada[l], b_ada[l], w_in[l], w_gla_gate2[l], b_gla_gate2[l], gla_norm_w[l],
                   diff_lambda_q1[l], diff_lambda_k1[l], diff_lambda_q2[l], diff_lambda_k2[l], diff_norm_w[l],
                   w_out[l], ln1_w[l], ln1_b[l], w_router_group[l], b_router_group[l], w_router_expert[l],
                   b_router_expert[l], w_exp_gate[l], w_exp_up[l], w_exp_down[l], ln2_w[l], ln2_b[l],
                   lam_init=lam_init, tm_in=1024, tn_in=1024, gla_rows=1024, bq=512, bk=512, tm_out=512,
                   moe_blk=256, tm_dsp=512, tm_cmb=256)
    return x
```

```python
import functools
import math

import jax
import jax.numpy as jnp
from jax import lax
from jax.experimental import pallas as pl
from jax.experimental.pallas import tpu as pltpu

F32 = jnp.float32
BF16 = jnp.bfloat16
HIGHEST = lax.Precision.HIGHEST

DEPTH = 1
GLA_HEADS = 4
GLA_GATE_RANK = 16
GLA_TAU = 16.0
GLA_CHUNK = 64
GLA_SUB_ROWS = 512
DIFF_HEADS = 8
ROPE_THETA = 10000.0
N_GROUPS = 4
EXPERTS_PER_GROUP = 8
N_EXPERTS = N_GROUPS * EXPERTS_PER_GROUP
LN_EPS = 1e-5
RMS_EPS = 1e-6
LOG2_E = math.log2(math.e)
LANES = 128
VMEM_LIMIT = 56 * 1024 * 1024

NT_DIMS = (((1,), (1,)), ((), ()))
TN_DIMS = (((0,), (0,)), ((), ()))


def _layer_norm(x):
    mu = jnp.mean(x, axis=-1, keepdims=True)
    xc = x - mu
    var = jnp.mean(xc * xc, axis=-1, keepdims=True)
    return xc * lax.rsqrt(var + LN_EPS)


def _silu(x):
    return x * jax.nn.sigmoid(x)


def _ada_kernel(c_ref, w_ref, b_ref, o_ref):
    s = _silu(c_ref[...])
    o_ref[...] = jnp.dot(s, w_ref[...], preferred_element_type=F32, precision=HIGHEST) + b_ref[...]


def _ada(c, w, b):
    bsz, d = c.shape
    n = w.shape[1]
    tn = d
    return pl.pallas_call(
        _ada_kernel,
        grid=(n // tn,),
        in_specs=[pl.BlockSpec((bsz, d), lambda j: (0, 0)),
                  pl.BlockSpec((d, tn), lambda j: (0, j)),
                  pl.BlockSpec((1, tn), lambda j: (0, j))],
        out_specs=pl.BlockSpec((bsz, tn), lambda j: (0, j)),
        out_shape=jax.ShapeDtypeStruct((bsz, n), F32),
        name="ada",
    )(c, w, b)


def _rope_kernel(pos_ref, inv_ref, cs_ref):
    ang = pos_ref[...].astype(F32) * inv_ref[...]
    lane = lax.broadcasted_iota(jnp.int32, ang.shape, 1)
    sin = jnp.sin(ang)
    cs_ref[:, :LANES] = jnp.cos(ang)
    cs_ref[:, LANES:] = jnp.where(lane < LANES // 2, -sin, sin)


def _rope_tables(pos_col, inv_row, tm):
    t = pos_col.shape[0]
    return pl.pallas_call(
        _rope_kernel,
        grid=(t // tm,),
        in_specs=[pl.BlockSpec((tm, 1), lambda i: (i, 0)),
                  pl.BlockSpec((1, LANES), lambda i: (0, 0))],
        out_specs=pl.BlockSpec((tm, 2 * LANES), lambda i: (i, 0)),
        out_shape=jax.ShapeDtypeStruct((t, 2 * LANES), F32),
        name="rope_tables",
    )(pos_col, inv_row)


def _inproj_kernel(x_ref, mod_ref, cs_ref, w_ref, wgr_ref, wg2_ref, bg2_ref, o_ref, lg_ref, u_ref,
                   *, rope_slabs, q_scale):
    j = pl.program_id(1)

    @pl.when(j == 0)
    def _():
        shift = mod_ref[0, 0:1, :]
        scale = mod_ref[0, 1:2, :]
        u = (_layer_norm(x_ref[...]) * (1.0 + scale) + shift).astype(BF16)
        u_ref[...] = u
        gr = jnp.dot(u, wgr_ref[...], preferred_element_type=F32)
        z = jnp.dot(gr.astype(BF16), wg2_ref[...], preferred_element_type=F32) + bg2_ref[...]
        log_sig = jnp.minimum(z, 0.0) - jnp.log(1.0 + jnp.exp(-jnp.abs(z)))
        lg_ref[...] = log_sig * (1.0 / GLA_TAU)

    acc = jnp.dot(u_ref[...], w_ref[...], preferred_element_type=F32)
    plain = j >= 0
    for tile, slabs in rope_slabs.items():
        plain = jnp.logical_and(plain, j != tile)

        @pl.when(j == tile)
        def _(slabs=slabs):
            cos = cs_ref[:, :LANES]
            sin = cs_ref[:, LANES:]
            for h in range(acc.shape[1] // LANES):
                t = acc[:, h * LANES:(h + 1) * LANES]
                if h in slabs:
                    sc = q_scale if slabs[h] else 1.0
                    partner = pltpu.roll(t, LANES // 2, 1)
                    t = t * (cos * sc) + partner * (sin * sc)
                o_ref[:, h * LANES:(h + 1) * LANES] = t.astype(o_ref.dtype)

    @pl.when(plain)
    def _():
        o_ref[...] = acc.astype(o_ref.dtype)


def _inproj(x2, mod, cs, w_main, w_gr, w_g2, b_g2, *, seq, tm, tn, q_col, k_col, rope_cols, q_scale):
    t, d = x2.shape
    n = w_main.shape[1]
    ng = w_g2.shape[1]
    rope_slabs = {}
    for col in range(0, n, LANES):
        for start, is_q in ((q_col, True), (k_col, False)):
            if start <= col < start + rope_cols:
                rope_slabs.setdefault(col // tn, {})[(col % tn) // LANES] = is_q
    kern = functools.partial(_inproj_kernel, rope_slabs=rope_slabs, q_scale=q_scale)
    return pl.pallas_call(
        kern,
        grid=(t // tm, n // tn),
        in_specs=[pl.BlockSpec((tm, d), lambda i, j: (i, 0)),
                  pl.BlockSpec((1,) + mod.shape[1:], lambda i, j: ((i * tm) // seq, 0, 0)),
                  pl.BlockSpec((tm, 2 * LANES), lambda i, j: (i, 0)),
                  pl.BlockSpec((d, tn), lambda i, j: (0, j)),
                  pl.BlockSpec((d, LANES), lambda i, j: (0, 0)),
                  pl.BlockSpec((LANES, ng), lambda i, j: (0, 0)),
                  pl.BlockSpec((1, ng), lambda i, j: (0, 0))],
        out_specs=[pl.BlockSpec((tm, tn), lambda i, j: (i, j)),
                   pl.BlockSpec((tm, ng), lambda i, j: (i, 0))],
        out_shape=[jax.ShapeDtypeStruct((t, n), BF16),
                   jax.ShapeDtypeStruct((t, ng), F32)],
        scratch_shapes=[pltpu.VMEM((tm, d), BF16)],
        compiler_params=pltpu.CompilerParams(
            dimension_semantics=("arbitrary", "arbitrary"), vmem_limit_bytes=VMEM_LIMIT),
        name="inproj",
    )(x2, mod, cs, w_main, w_gr, w_g2, b_g2)


def _gla_kernel(q_ref, k_ref, v_ref, g_ref, lg_ref, nw_ref, o_ref, st_ref, *, chunk, sub_rows, q_scale):
    @pl.when(pl.program_id(2) == 0)
    def _():
        st_ref[...] = jnp.zeros_like(st_ref)

    n_chunks = sub_rows // chunk
    pos = lax.broadcasted_iota(jnp.int32, (sub_rows, lg_ref.shape[1]), 0) % chunk
    r = lax.broadcasted_iota(jnp.int32, (sub_rows, sub_rows), 0)
    c = lax.broadcasted_iota(jnp.int32, (sub_rows, sub_rows), 1)
    keep = jnp.logical_and(c <= r, (r // chunk) == (c // chunk))
    st = st_ref[...]

    for si in range(q_ref.shape[0] // sub_rows):
        rs = slice(si * sub_rows, (si + 1) * sub_rows)

        b = lg_ref[rs, :]
        step = 1
        while step < chunk:
            b = b + jnp.where(pos >= step, pltpu.roll(b, step, 0), 0.0)
            step *= 2

        k = k_ref[rs, :].astype(F32)
        q_t = (q_ref[rs, :].astype(F32) * q_scale * jnp.exp(b)).astype(BF16)
        k_t = (k * jnp.exp(-b)).astype(BF16)
        v = v_ref[rs, :]

        attn = lax.dot_general(q_t, k_t, NT_DIMS, preferred_element_type=F32)
        o = jnp.dot(jnp.where(keep, attn, 0.0).astype(BF16), v, preferred_element_type=F32)

        o_inter = []
        for ci in range(n_chunks):
            sl = slice(ci * chunk, (ci + 1) * chunk)
            b_c = b[sl, :]
            b_last = b_c[chunk - 1:chunk, :]
            k_d = (k[sl, :] * jnp.exp(b_last - b_c)).astype(BF16)
            kv = lax.dot_general(v[sl, :], k_d, TN_DIMS, preferred_element_type=F32)
            o_inter.append(lax.dot_general(q_t[sl, :], st.astype(BF16), NT_DIMS, preferred_element_type=F32))
            st = st * jnp.exp(b_last) + kv
        o = o + jnp.concatenate(o_inter, axis=0)

        ms = jnp.mean(o * o, axis=-1, keepdims=True)
        y = o * lax.rsqrt(ms + RMS_EPS) * nw_ref[...] * _silu(g_ref[rs, :].astype(F32))
        o_ref[rs, :] = y.astype(o_ref.dtype)

    st_ref[...] = st


def _gla(proj, log_g, norm_w, *, bsz, seq, rows, dk, dv, q_col, k_col, v_col, g_col):
    t = proj.shape[0]
    nl = seq // rows
    kern = functools.partial(_gla_kernel, chunk=GLA_CHUNK, sub_rows=min(rows, GLA_SUB_ROWS), q_scale=dk ** -0.5)
    row = lambda b, h, l: b * nl + l
    return pl.pallas_call(
        kern,
        grid=(bsz, GLA_HEADS, nl),
        in_specs=[pl.BlockSpec((rows, dk), lambda b, h, l: (row(b, h, l), q_col // dk + h)),
                  pl.BlockSpec((rows, dk), lambda b, h, l: (row(b, h, l), k_col // dk + h)),
                  pl.BlockSpec((rows, dv), lambda b, h, l: (row(b, h, l), v_col // dv + h)),
                  pl.BlockSpec((rows, dv), lambda b, h, l: (row(b, h, l), g_col // dv + h)),
                  pl.BlockSpec((rows, dk), lambda b, h, l: (row(b, h, l), h)),
                  pl.BlockSpec((1, dv), lambda b, h, l: (0, 0))],
        out_specs=pl.BlockSpec((rows, dv), lambda b, h, l: (row(b, h, l), h)),
        out_shape=jax.ShapeDtypeStruct((t, GLA_HEADS * dv), BF16),
        scratch_shapes=[pltpu.VMEM((dv, dk), F32)],
        compiler_params=pltpu.CompilerParams(
            dimension_semantics=("arbitrary", "arbitrary", "arbitrary"), vmem_limit_bytes=VMEM_LIMIT),
        name="gla",
    )(proj, proj, proj, proj, log_g, norm_w)


def _diff_kernel(lam_ref, q_ref, qn_ref, k_ref, v_ref, nw_ref, o_ref,
                 qs_ref, vt_ref, s0_ref, sa_ref, sb_ref, m_ref, l_ref, acc_ref, *, bq, bk, dh, lam_init):
    qi = pl.program_id(2)
    q0 = qi * bq
    seq = k_ref.shape[0]

    def mask_queries(q):
        lane = lax.broadcasted_iota(jnp.int32, q.shape, 1)
        zero = jnp.zeros_like(q)
        map0 = (lane % dh) < dh // 2
        qs_ref[:bq, :] = jnp.where(map0, q, zero)
        qs_ref[bq:, :] = jnp.where(map0, zero, q)

    def scores(kj, dst):
        k0 = pl.multiple_of(kj * bk, bk)
        dst[...] = lax.dot_general(k_ref[pl.ds(k0, bk), :], qs_ref[...], NT_DIMS, preferred_element_type=F32)

    def update(src, kj, masked):
        k0 = pl.multiple_of(kj * bk, bk)
        s = src[...]
        if masked:
            key = k0 + lax.broadcasted_iota(jnp.int32, s.shape, 0)
            col = lax.broadcasted_iota(jnp.int32, s.shape, 1)
            qpos = q0 + jnp.where(col >= bq, col - bq, col)
            s = jnp.where(key <= qpos, s, -jnp.inf)
        m_prev = m_ref[...]
        m_new = jnp.maximum(m_prev, jnp.max(s, axis=0, keepdims=True))
        alpha = jnp.exp2(m_prev - m_new)
        p = jnp.exp2(s - m_new)
        l_ref[...] = alpha * l_ref[...] + jnp.sum(p, axis=0, keepdims=True)
        acc_ref[...] = alpha * acc_ref[...] + jnp.dot(
            vt_ref[:, pl.ds(k0, bk)], p.astype(BF16), preferred_element_type=F32)
        m_ref[...] = m_new

    @pl.when(qi == 0)
    def _():
        for c in range(seq // bk):
            vt_ref[:, c * bk:(c + 1) * bk] = v_ref[c * bk:(c + 1) * bk, :].astype(F32).T.astype(BF16)
        mask_queries(q_ref[...])
        scores(0, s0_ref)

    m_ref[...] = jnp.full_like(m_ref, -jnp.inf)
    l_ref[...] = jnp.zeros_like(l_ref)
    acc_ref[...] = jnp.zeros_like(acc_ref)

    n_full = (q0 + 1) // bk

    @pl.when(n_full == 0)
    def _():
        update(s0_ref, 0, True)

    @pl.when(n_full > 0)
    def _():
        scores(1, sa_ref)
        update(s0_ref, 0, False)
        rest = n_full - 1

        def pair_body(jj, carry):
            j = 1 + 2 * jj
            scores(j + 1, sb_ref)
            update(sa_ref, j, False)
            scores(j + 2, sa_ref)
            update(sb_ref, j + 1, False)
            return carry

        lax.fori_loop(0, rest // 2, pair_body, 0)
        odd = (rest % 2) == 1

        @pl.when(odd)
        def _():
            scores(n_full, sb_ref)
            update(sa_ref, n_full - 1, False)
            update(sb_ref, n_full, True)

        @pl.when(jnp.logical_not(odd))
        def _():
            update(sa_ref, n_full, True)

    mask_queries(qn_ref[...])
    scores(0, s0_ref)

    lv = lam_ref[...]
    s1 = jnp.sum(lv[0:1, :] * lv[1:2, :], axis=-1, keepdims=True)
    s2 = jnp.sum(lv[2:3, :] * lv[3:4, :], axis=-1, keepdims=True)
    lam = jnp.exp(s1) - jnp.exp(s2) + lam_init
    o_all = acc_ref[...] / l_ref[...]
    o_t = o_all[:, :bq] - lam * o_all[:, bq:]
    ms = jnp.mean(o_t * o_t, axis=0, keepdims=True)
    y = (o_t * lax.rsqrt(ms + RMS_EPS)).T * nw_ref[...] * (1.0 - lam_init)
    o_ref[...] = y.astype(o_ref.dtype)


def _diff(lam_vecs, proj, norm_w, *, bsz, seq, bq, bk, dh, q_col, k_col, v_col, lam_init):
    assert bk % bq == 0 and seq % bk == 0
    t = proj.shape[0]
    w = 2 * dh
    nq = seq // bq
    kern = functools.partial(_diff_kernel, bq=bq, bk=bk, dh=dh, lam_init=lam_init)
    return pl.pallas_call(
        kern,
        grid=(bsz, DIFF_HEADS, nq),
        in_specs=[pl.BlockSpec(lam_vecs.shape, lambda b, h, i: (0, 0)),
                  pl.BlockSpec((bq, w), lambda b, h, i: (b * nq + i, q_col // w + h)),
                  pl.BlockSpec((bq, w), lambda b, h, i: (b * nq + jnp.minimum(i + 1, nq - 1), q_col // w + h)),
                  pl.BlockSpec((seq, w), lambda b, h, i: (b, k_col // w + h)),
                  pl.BlockSpec((seq, w), lambda b, h, i: (b, v_col // w + h)),
                  pl.BlockSpec((1, w), lambda b, h, i: (0, 0))],
        out_specs=pl.BlockSpec((bq, w), lambda b, h, i: (b * nq + i, h)),
        out_shape=jax.ShapeDtypeStruct((t, DIFF_HEADS * w), BF16),
        scratch_shapes=[pltpu.VMEM((2 * bq, w), BF16),
                        pltpu.VMEM((w, seq), BF16),
                        pltpu.VMEM((bk, 2 * bq), F32),
                        pltpu.VMEM((bk, 2 * bq), F32),
                        pltpu.VMEM((bk, 2 * bq), F32),
                        pltpu.VMEM((1, 2 * bq), F32),
                        pltpu.VMEM((1, 2 * bq), F32),
                        pltpu.VMEM((w, 2 * bq), F32)],
        compiler_params=pltpu.CompilerParams(
            dimension_semantics=("arbitrary", "arbitrary", "arbitrary"), vmem_limit_bytes=VMEM_LIMIT),
        name="diff_attn",
    )(lam_vecs, proj, proj, proj, proj, norm_w)


ROW_SUB = 8


def _store_row_tiles(ref, val):
    rows = val.shape[0]
    for s in range(ROW_SUB):
        ref[pl.ds(s, rows, stride=ROW_SUB), :] = val[:, s * LANES:(s + 1) * LANES]


def _load_row_tiles(ref, rows):
    return jnp.concatenate([ref[pl.ds(s, rows, stride=ROW_SUB), :] for s in range(ROW_SUB)], axis=1)


def _tile_copy(src_hbm, dst_vmem, sem, src_row, dst_row):
    return pltpu.make_async_copy(src_hbm.at[pl.ds(src_row * ROW_SUB, ROW_SUB), :],
                                 dst_vmem.at[pl.ds(dst_row * ROW_SUB, ROW_SUB), :], sem)


def _wait_tiles(src_hbm, dst_vmem, sem):
    pltpu.make_async_copy(src_hbm.at[pl.ds(0, dst_vmem.shape[0]), :], dst_vmem, sem).wait()


def _outproj_kernel(og_ref, od_ref, ga_ref, gb_ref, x_ref, mod_ref, wo_ref, lnw_ref, lnb_ref, wr_ref, br_ref,
                    x1_ref, u2_ref, rt_ref, cnt_ref, tri_ref, run_ref, *, alpha):
    @pl.when(pl.program_id(0) == 0)
    def _():
        r = lax.broadcasted_iota(jnp.int32, tri_ref.shape, 0)
        c = lax.broadcasted_iota(jnp.int32, tri_ref.shape, 1)
        tri_ref[...] = (c < r).astype(BF16)
        run_ref[...] = jnp.zeros_like(run_ref)

    merged = jax.nn.sigmoid(ga_ref[...]) * og_ref[...] + jax.nn.sigmoid(gb_ref[...]) * od_ref[...]
    y = jnp.dot(merged, wo_ref[...], preferred_element_type=F32)
    gate1 = mod_ref[0, 2:3, :]
    shift2 = mod_ref[0, 3:4, :]
    scale2 = mod_ref[0, 4:5, :]
    x1 = _layer_norm(alpha * x_ref[...] + gate1 * y) * lnw_ref[...] + lnb_ref[...]
    x1_ref[...] = x1
    u2 = _layer_norm(x1) * (1.0 + scale2) + shift2
    _store_row_tiles(u2_ref, u2)

    logits = jnp.dot(u2.astype(BF16), wr_ref[...], preferred_element_type=F32) + br_ref[...]
    lane = lax.broadcasted_iota(jnp.int32, logits.shape, 1)
    neg = jnp.float32(-jnp.inf)
    big = jnp.int32(LANES)
    lg = jnp.where(lane < N_GROUPS, logits, neg)
    g_max = jnp.max(lg, axis=-1, keepdims=True)
    w_grp = 1.0 / jnp.sum(jnp.exp(lg - g_max), axis=-1, keepdims=True)
    g_idx = jnp.min(jnp.where(lg == g_max, lane, big), axis=-1, keepdims=True)
    lo = N_GROUPS + EXPERTS_PER_GROUP * g_idx
    le = jnp.where(jnp.logical_and(lane >= lo, lane < lo + EXPERTS_PER_GROUP), logits, neg)
    v1 = jnp.max(le, axis=-1, keepdims=True)
    i1 = jnp.min(jnp.where(le == v1, lane, big), axis=-1, keepdims=True)
    le2 = jnp.where(lane == i1, neg, le)
    v2 = jnp.max(le2, axis=-1, keepdims=True)
    i2 = jnp.min(jnp.where(le2 == v2, lane, big), axis=-1, keepdims=True)
    e2 = jnp.exp(v2 - v1)
    den = 1.0 + e2
    c1 = w_grp / den
    c2 = w_grp * e2 / den

    sel1 = lane == i1 - N_GROUPS
    sel2 = lane == i2 - N_GROUPS
    onehot = jnp.logical_or(sel1, sel2).astype(BF16)
    before = jnp.dot(tri_ref[...], onehot, preferred_element_type=F32) + run_ref[...]
    pos1 = jnp.sum(jnp.where(sel1, before, 0.0), axis=-1, keepdims=True)
    pos2 = jnp.sum(jnp.where(sel2, before, 0.0), axis=-1, keepdims=True)
    run_ref[...] += jnp.sum(onehot.astype(F32), axis=0, keepdims=True)
    cnt_ref[...] = jnp.broadcast_to(run_ref[...], cnt_ref.shape)

    cols = ((i1 - N_GROUPS).astype(F32), (i2 - N_GROUPS).astype(F32), c1, c2, pos1, pos2)
    rt = jnp.zeros_like(logits)
    for li, col in enumerate(cols):
        rt = jnp.where(lane == li, col, rt)
    rt_ref[...] = rt


def _outproj(o_gla, o_diff, proj, x2, mod, w_out, ln_w, ln_b, w_r, b_r, *, seq, tm, ga_col, alpha):
    t, d = x2.shape
    assert d == ROW_SUB * LANES
    kern = functools.partial(_outproj_kernel, alpha=alpha)
    row_spec = pl.BlockSpec((tm, d), lambda i: (i, 0))
    vec_spec = pl.BlockSpec((1, d), lambda i: (0, 0))
    return pl.pallas_call(
        kern,
        grid=(t // tm,),
        in_specs=[row_spec, row_spec,
                  pl.BlockSpec((tm, d), lambda i: (i, ga_col // d)),
                  pl.BlockSpec((tm, d), lambda i: (i, ga_col // d + 1)),
                  row_spec,
                  pl.BlockSpec((1,) + mod.shape[1:], lambda i: ((i * tm) // seq, 0, 0)),
                  pl.BlockSpec((d, d), lambda i: (0, 0)),
                  vec_spec, vec_spec,
                  pl.BlockSpec((d, LANES), lambda i: (0, 0)),
                  pl.BlockSpec((1, LANES), lambda i: (0, 0))],
        out_specs=[row_spec,
                   pl.BlockSpec((tm * ROW_SUB, LANES), lambda i: (i, 0)),
                   pl.BlockSpec((tm, LANES), lambda i: (i, 0)),
                   pl.BlockSpec((ROW_SUB, LANES), lambda i: (0, 0))],
        out_shape=[jax.ShapeDtypeStruct((t, d), F32),
                   jax.ShapeDtypeStruct((t * ROW_SUB, LANES), F32),
                   jax.ShapeDtypeStruct((t, LANES), F32),
                   jax.ShapeDtypeStruct((ROW_SUB, LANES), F32)],
        scratch_shapes=[pltpu.VMEM((tm, tm), BF16),
                        pltpu.VMEM((1, LANES), F32)],
        compiler_params=pltpu.CompilerParams(
            dimension_semantics=("arbitrary",), vmem_limit_bytes=VMEM_LIMIT),
        name="outproj",
    )(o_gla, o_diff, proj, proj, x2, mod, w_out, ln_w, ln_b, w_r, b_r)


def _dispatch_kernel(fill_ref, d0_ref, d1_ref, u_ref, xs_hbm, zbuf, sem_fill, sem, *, tm, blk):
    i = pl.program_id(0)

    @pl.when(i == 0)
    def _():
        zbuf[...] = jnp.zeros_like(zbuf)
        n_fill = fill_ref.shape[0]

        def fill_copy(j):
            return pltpu.make_async_copy(zbuf, xs_hbm.at[pl.ds(fill_ref[j] * ROW_SUB, blk * ROW_SUB), :], sem_fill)

        def start_fill(j, carry):
            @pl.when(fill_ref[j] >= 0)
            def _():
                fill_copy(j).start()
            return carry

        def wait_fill(j, carry):
            @pl.when(fill_ref[j] >= 0)
            def _():
                fill_copy(j).wait()
            return carry

        lax.fori_loop(0, n_fill, start_fill, 0)
        lax.fori_loop(0, n_fill, wait_fill, 0)

    def row_copy(dest_ref, r):
        return pltpu.make_async_copy(u_ref.at[pl.ds(r * ROW_SUB, ROW_SUB), :],
                                     xs_hbm.at[pl.ds(dest_ref[0, 0, r] * ROW_SUB, ROW_SUB), :], sem)

    def body(r, carry):
        row_copy(d0_ref, r).start(priority=0)
        row_copy(d1_ref, r).start(priority=1)
        return carry

    lax.fori_loop(0, tm, body, 0, unroll=4)
    for _ in range(2):
        pltpu.make_async_copy(u_ref, xs_hbm.at[pl.ds(0, tm * ROW_SUB), :], sem).wait()


def _dispatch(fill_start, dest0, dest1, u2_tiles, *, p_rows, tm, blk):
    t = u2_tiles.shape[0] // ROW_SUB
    kern = functools.partial(_dispatch_kernel, tm=tm, blk=blk)
    idx_spec = pl.BlockSpec((1, 1, tm), lambda i, fs: (i, 0, 0), memory_space=pltpu.SMEM)
    grid_spec = pltpu.PrefetchScalarGridSpec(
        num_scalar_prefetch=1,
        grid=(t // tm,),
        in_specs=[idx_spec, idx_spec, pl.BlockSpec((tm * ROW_SUB, LANES), lambda i, fs: (i, 0))],
        out_specs=pl.BlockSpec(memory_space=pl.ANY),
        scratch_shapes=[pltpu.VMEM((blk * ROW_SUB, LANES), F32),
                        pltpu.SemaphoreType.DMA(()),
                        pltpu.SemaphoreType.DMA(())],
    )
    return pl.pallas_call(
        kern,
        grid_spec=grid_spec,
        out_shape=jax.ShapeDtypeStruct((p_rows * ROW_SUB, LANES), F32),
        compiler_params=pltpu.CompilerParams(dimension_semantics=("arbitrary",)),
        name="dispatch",
    )(fill_start, dest0, dest1, u2_tiles)


def _expert_kernel(be_ref, na_ref, x_ref, wg_ref, wu_ref, wd_ref, o_ref, wgb, wub, wdb, *, blk):
    i = pl.program_id(0)
    active = i < na_ref[0]

    @pl.when(active)
    def _():
        changed = jnp.logical_or(i == 0, be_ref[i] != be_ref[jnp.maximum(i - 1, 0)])

        @pl.when(changed)
        def _():
            wgb[...] = wg_ref[0].astype(BF16)
            wub[...] = wu_ref[0].astype(BF16)
            wdb[...] = wd_ref[0].astype(BF16)

        xb = _load_row_tiles(x_ref, blk).astype(BF16)
        hg = jnp.dot(xb, wgb[...], preferred_element_type=F32)
        hu = jnp.dot(xb, wub[...], preferred_element_type=F32)
        h = (_silu(hg) * hu).astype(BF16)
        _store_row_tiles(o_ref, jnp.dot(h, wdb[...], preferred_element_type=F32))

    @pl.when(jnp.logical_not(active))
    def _():
        o_ref[...] = jnp.zeros_like(o_ref)


def _experts(block_expert, n_active, xs_tiles, w_gate, w_up, w_down, *, blk):
    nblk = block_expert.shape[0]
    e, d, ff = w_gate.shape
    kern = functools.partial(_expert_kernel, blk=blk)
    grid_spec = pltpu.PrefetchScalarGridSpec(
        num_scalar_prefetch=2,
        grid=(nblk,),
        in_specs=[pl.BlockSpec((blk * ROW_SUB, LANES), lambda i, be, na: (jnp.minimum(i, na[0] - 1), 0)),
                  pl.BlockSpec((1, d, ff), lambda i, be, na: (be[i], 0, 0)),
                  pl.BlockSpec((1, d, ff), lambda i, be, na: (be[i], 0, 0)),
                  pl.BlockSpec((1, ff, d), lambda i, be, na: (be[i], 0, 0))],
        out_specs=pl.BlockSpec((blk * ROW_SUB, LANES), lambda i, be, na: (i, 0)),
        scratch_shapes=[pltpu.VMEM((d, ff), BF16),
                        pltpu.VMEM((d, ff), BF16),
                        pltpu.VMEM((ff, d), BF16)],
    )
    return pl.pallas_call(
        kern,
        grid_spec=grid_spec,
        out_shape=jax.ShapeDtypeStruct((nblk * blk * ROW_SUB, LANES), F32),
        compiler_params=pltpu.CompilerParams(
            dimension_semantics=("arbitrary",), vmem_limit_bytes=VMEM_LIMIT),
        name="experts",
    )(block_expert, n_active, xs_tiles, w_gate, w_up, w_down)


def _combine_kernel(d0_ref, d0n_ref, d1_ref, d1n_ref, rows_hbm, x1_ref, rt_ref, mod_ref, lnw_ref, lnb_ref,
                    o_ref, buf, sem, *, tm, alpha):
    i = pl.program_id(0)
    n = pl.num_programs(0)
    slot = i % 2

    def start_gather(i0_ref, i1_ref, s):
        def body(r, carry):
            _tile_copy(rows_hbm, buf.at[s, 0], sem.at[s], i0_ref[0, 0, r], r).start(priority=0)
            _tile_copy(rows_hbm, buf.at[s, 1], sem.at[s], i1_ref[0, 0, r], r).start(priority=1)
            return carry
        lax.fori_loop(0, tm, body, 0, unroll=4)

    def wait_gather(s):
        _wait_tiles(rows_hbm, buf.at[s, 0], sem.at[s])
        _wait_tiles(rows_hbm, buf.at[s, 1], sem.at[s])

    @pl.when(i == 0)
    def _():
        start_gather(d0_ref, d1_ref, 0)

    start_gather(d0n_ref, d1n_ref, 1 - slot)
    wait_gather(slot)
    rt = rt_ref[...]
    y = rt[:, 2:3] * _load_row_tiles(buf.at[slot, 0], tm) + rt[:, 3:4] * _load_row_tiles(buf.at[slot, 1], tm)
    gate2 = mod_ref[0, 5:6, :]
    z = alpha * x1_ref[...] + gate2 * y
    o_ref[...] = _layer_norm(z) * lnw_ref[...] + lnb_ref[...]

    @pl.when(i == n - 1)
    def _():
        wait_gather(1 - slot)


def _combine(dest0, dest1, rows_tiles, x1, route, mod, ln_w, ln_b, *, seq, tm, alpha):
    t, d = x1.shape
    nt = t // tm
    kern = functools.partial(_combine_kernel, tm=tm, alpha=alpha)
    cur = lambda i: (i, 0, 0)
    nxt = lambda i: (jnp.minimum(i + 1, nt - 1), 0, 0)
    idx_spec = lambda m: pl.BlockSpec((1, 1, tm), m, memory_space=pltpu.SMEM)
    row_spec = pl.BlockSpec((tm, d), lambda i: (i, 0))
    vec_spec = pl.BlockSpec((1, d), lambda i: (0, 0))
    return pl.pallas_call(
        kern,
        grid=(nt,),
        in_specs=[idx_spec(cur), idx_spec(nxt), idx_spec(cur), idx_spec(nxt),
                  pl.BlockSpec(memory_space=pl.ANY),
                  row_spec,
                  pl.BlockSpec((tm, LANES), lambda i: (i, 0)),
                  pl.BlockSpec((1,) + mod.shape[1:], lambda i: ((i * tm) // seq, 0, 0)),
                  vec_spec, vec_spec],
        out_specs=row_spec,
        out_shape=jax.ShapeDtypeStruct((t, d), F32),
        scratch_shapes=[pltpu.VMEM((2, 2, tm * ROW_SUB, LANES), F32),
                        pltpu.SemaphoreType.DMA((2,))],
        compiler_params=pltpu.CompilerParams(
            dimension_semantics=("arbitrary",), vmem_limit_bytes=VMEM_LIMIT),
        name="combine",
    )(dest0, dest0, dest1, dest1, rows_tiles, x1, route, mod, ln_w, ln_b)


def _dispatch_plan(route, counts, blk):
    t = route.shape[0]
    eid = route[:, :2].astype(jnp.int32)
    pos = route[:, 4:6].astype(jnp.int32)
    counts = counts[:N_EXPERTS].astype(jnp.int32)
    pcounts = ((counts + blk - 1) // blk) * blk
    pend = jnp.cumsum(pcounts)
    pstart = pend - pcounts
    experts = jnp.arange(N_EXPERTS, dtype=jnp.int32)
    dest = jnp.sum(jnp.where(eid[:, :, None] == experts, pstart, 0), axis=-1) + pos
    p_rows = ((2 * t + N_EXPERTS * (blk - 1) + blk - 1) // blk) * blk
    nblk = p_rows // blk
    block_start = jnp.arange(nblk, dtype=jnp.int32) * blk
    block_expert = jnp.minimum(jnp.sum((pend[None, :] <= block_start[:, None]).astype(jnp.int32), axis=1),
                               N_EXPERTS - 1)
    n_active = (pend[N_EXPERTS - 1:] // blk).astype(jnp.int32)
    tail = pend[N_EXPERTS - 1] + jnp.arange(N_EXPERTS, dtype=jnp.int32) * blk
    fill_start = jnp.concatenate([jnp.where(pcounts > 0, pend - blk, -1), jnp.where(tail < p_rows, tail, -1)])
    return dest[:, 0], dest[:, 1], fill_start, block_expert, n_active, p_rows


def _layer(x, c, positions, w_ada, b_ada, w_in, w_g2, b_g2, gla_nw, lq1, lk1, lq2, lk2, diff_nw, w_out,
           ln1_w, ln1_b, w_rg, b_rg, w_re, b_re, w_eg, w_eu, w_ed, ln2_w, ln2_b, *, lam_init,
           tm_in, tn_in, gla_rows, bq, bk, tm_out, moe_blk, tm_dsp, tm_cmb):
    bsz, seq, d = x.shape
    t = bsz * seq
    alpha = (2.0 * DEPTH) ** 0.25
    gla_dk = d // (2 * GLA_HEADS)
    gla_dv = d // GLA_HEADS
    dh = d // (2 * DIFF_HEADS)
    hk = GLA_HEADS * gla_dk
    hv = GLA_HEADS * gla_dv
    dq = DIFF_HEADS * 2 * dh
    gr_col = 2 * hk + 2 * hv
    q_col, k_col = 0, hk
    v_col, g_col = 2 * hk, 2 * hk + hv
    dq_col = gr_col
    dk_col = dq_col + dq
    dv_col = dk_col + dq
    mg_col = dv_col + dq
    w_main = jnp.concatenate([w_in[:, :gr_col], w_in[:, gr_col + GLA_GATE_RANK:]], axis=1).astype(BF16)
    half = dh // 2
    lane = jnp.arange(2 * dh)
    in_head = (lane // half % 2) * dh + (lane // dh) * half + lane % half
    cols = (jnp.arange(DIFF_HEADS)[:, None] * 2 * dh + in_head[None, :]).reshape(-1)
    w_main = jnp.concatenate([w_main[:, :dq_col], w_main[:, dq_col + cols], w_main[:, dk_col + cols],
                              w_main[:, dv_col:]], axis=1)
    w_gr = jnp.pad(w_in[:, gr_col:gr_col + GLA_GATE_RANK], ((0, 0), (0, LANES - GLA_GATE_RANK))).astype(BF16)
    w_g2p = jnp.pad(w_g2, ((0, LANES - GLA_GATE_RANK), (0, 0))).astype(BF16)
    x2 = x.reshape(t, d)

    ada = _ada(c, w_ada, b_ada.reshape(1, -1))
    mod = ada.reshape(bsz, 6, d)

    inv = ROPE_THETA ** (-jnp.arange(half, dtype=F32) / half)
    inv_row = jnp.tile(inv, LANES // half).reshape(1, LANES)
    cs = _rope_tables(positions.reshape(t, 1), inv_row, tm_in)

    proj, log_g = _inproj(x2, mod, cs, w_main, w_gr, w_g2p, b_g2.reshape(1, -1), seq=seq, tm=tm_in, tn=tn_in,
                          q_col=dq_col, k_col=dk_col, rope_cols=dq, q_scale=dh ** -0.5 * LOG2_E)

    o_gla = _gla(proj, log_g, gla_nw.reshape(1, -1), bsz=bsz, seq=seq, rows=gla_rows, dk=gla_dk, dv=gla_dv,
                 q_col=q_col, k_col=k_col, v_col=v_col, g_col=g_col)

    lam_vecs = jnp.pad(jnp.stack([lq1, lk1, lq2, lk2]), ((0, 4), (0, LANES - dh)))
    o_diff = _diff(lam_vecs, proj, diff_nw.reshape(1, -1), bsz=bsz, seq=seq, bq=bq, bk=bk, dh=dh,
                   q_col=dq_col, k_col=dk_col, v_col=dv_col, lam_init=lam_init)

    w_r = jnp.pad(jnp.concatenate([w_rg, w_re], axis=1), ((0, 0), (0, LANES - N_GROUPS - N_EXPERTS))).astype(BF16)
    b_r = jnp.pad(jnp.concatenate([b_rg, b_re]), (0, LANES - N_GROUPS - N_EXPERTS)).reshape(1, LANES)
    x1, u2_tiles, route, counts = _outproj(o_gla, o_diff, proj, x2, mod, w_out.astype(BF16), ln1_w.reshape(1, -1),
                                           ln1_b.reshape(1, -1), w_r, b_r, seq=seq, tm=tm_out, ga_col=mg_col,
                                           alpha=alpha)

    dest0, dest1, fill_start, block_expert, n_active, p_rows = _dispatch_plan(route, counts[0], moe_blk)
    xs_tiles = _dispatch(fill_start, dest0.reshape(-1, 1, tm_dsp), dest1.reshape(-1, 1, tm_dsp), u2_tiles,
                         p_rows=p_rows, tm=tm_dsp, blk=moe_blk)
    rows = _experts(block_expert, n_active, xs_tiles, w_eg, w_eu, w_ed, blk=moe_blk)
    out = _combine(dest0.reshape(-1, 1, tm_cmb), dest1.reshape(-1, 1, tm_cmb), rows, x1, route, mod,
                   ln2_w.reshape(1, -1), ln2_b.reshape(1, -1), seq=seq, tm=tm_cmb, alpha=alpha)
    return out.reshape(bsz, seq, d)


def kernel(x, c, positions, w_ada, b_ada, w_in, w_gla_gate2, b_gla_gate2, gla_norm_w, diff_lambda_q1,
           diff_lambda_k1, diff_lambda_q2, diff_lambda_k2, diff_norm_w, w_out, ln1_w, ln1_b, w_router_group,
           b_router_group, w_router_expert, b_router_expert, w_exp_gate, w_exp_up, w_exp_down, ln2_w, ln2_b):
    assert w_ada.shape[0] == DEPTH
    for l in range(DEPTH):
        lam_init = 0.8 - 0.6 * math.exp(-0.3 * l)
        x = _layer(x, c, positions, w_ada[l], b_ada[l], w_in[l], w_gla_gate2[l], b_gla_gate2[l], gla_norm_w[l],
                   diff_lambda_q1[l], diff_lambda_k1[l], diff_lambda_q2[l], diff_lambda_k2[l], diff_norm_w[l],
                   w_out[l], ln1_w[l], ln1_b[l], w_router_group[l], b_router_group[l], w_router_expert[l],
                   b_router_expert[l], w_exp_gate[l], w_exp_up[l], w_exp_down[l], ln2_w[l], ln2_b[l],
                   lam_init=lam_init, tm_in=1024, tn_in=2048, gla_rows=1024, bq=512, bk=512, tm_out=512,
                   moe_blk=256, tm_dsp=1024, tm_cmb=512)
    return x
```

```python
import functools
import math

import jax
import jax.numpy as jnp
import numpy as np
from jax import lax
from jax.experimental import pallas as pl
from jax.experimental.pallas import tpu as pltpu

F32 = jnp.float32
BF16 = jnp.bfloat16
HIGHEST = lax.Precision.HIGHEST

DEPTH = 1
GLA_HEADS = 4
GLA_GATE_RANK = 16
GLA_TAU = 16.0
GLA_CHUNK = 64
GLA_SUB_ROWS = 512
DIFF_HEADS = 8
ROPE_THETA = 10000.0
N_GROUPS = 4
EXPERTS_PER_GROUP = 8
N_EXPERTS = N_GROUPS * EXPERTS_PER_GROUP
LN_EPS = 1e-5
RMS_EPS = 1e-6
LOG2_E = math.log2(math.e)
LANES = 128
VMEM_LIMIT = 56 * 1024 * 1024

NT_DIMS = (((1,), (1,)), ((), ()))
TN_DIMS = (((0,), (0,)), ((), ()))


def _layer_norm(x):
    mu = jnp.mean(x, axis=-1, keepdims=True)
    xc = x - mu
    var = jnp.mean(xc * xc, axis=-1, keepdims=True)
    return xc * lax.rsqrt(var + LN_EPS)


def _silu(x):
    return x * jax.nn.sigmoid(x)


def _ada_kernel(c_ref, w_ref, b_ref, o_ref):
    s = _silu(c_ref[...])
    o_ref[...] = jnp.dot(s, w_ref[...], preferred_element_type=F32, precision=HIGHEST) + b_ref[...]


def _ada(c, w, b):
    bsz, d = c.shape
    n = w.shape[1]
    tn = d
    return pl.pallas_call(
        _ada_kernel,
        grid=(n // tn,),
        in_specs=[pl.BlockSpec((bsz, d), lambda j: (0, 0)),
                  pl.BlockSpec((d, tn), lambda j: (0, j)),
                  pl.BlockSpec((1, tn), lambda j: (0, j))],
        out_specs=pl.BlockSpec((bsz, tn), lambda j: (0, j)),
        out_shape=jax.ShapeDtypeStruct((bsz, n), F32),
        name="ada",
    )(c, w, b)


def _rope_kernel(pos_ref, inv_ref, cs_ref):
    ang = pos_ref[...].astype(F32) * inv_ref[...]
    lane = lax.broadcasted_iota(jnp.int32, ang.shape, 1)
    sin = jnp.sin(ang)
    cs_ref[:, :LANES] = jnp.cos(ang)
    cs_ref[:, LANES:] = jnp.where(lane < LANES // 2, -sin, sin)


def _rope_tables(pos_col, inv_row, tm):
    t = pos_col.shape[0]
    return pl.pallas_call(
        _rope_kernel,
        grid=(t // tm,),
        in_specs=[pl.BlockSpec((tm, 1), lambda i: (i, 0)),
                  pl.BlockSpec((1, LANES), lambda i: (0, 0))],
        out_specs=pl.BlockSpec((tm, 2 * LANES), lambda i: (i, 0)),
        out_shape=jax.ShapeDtypeStruct((t, 2 * LANES), F32),
        name="rope_tables",
    )(pos_col, inv_row)


def _inproj_kernel(x_ref, mod_ref, cs_ref, w_ref, wgr_ref, wg2_ref, bg2_ref, o_ref, lg_ref, u_ref,
                   *, rope_slabs, q_scale):
    j = pl.program_id(1)

    @pl.when(j == 0)
    def _():
        shift = mod_ref[0, 0:1, :]
        scale = mod_ref[0, 1:2, :]
        u = (_layer_norm(x_ref[...]) * (1.0 + scale) + shift).astype(BF16)
        u_ref[...] = u
        gr = jnp.dot(u, wgr_ref[...], preferred_element_type=F32)
        z = jnp.dot(gr.astype(BF16), wg2_ref[...], preferred_element_type=F32) + bg2_ref[...]
        log_sig = jnp.minimum(z, 0.0) - jnp.log(1.0 + jnp.exp(-jnp.abs(z)))
        lg_ref[...] = log_sig * (1.0 / GLA_TAU)

    acc = jnp.dot(u_ref[...], w_ref[...], preferred_element_type=F32)
    plain = j >= 0
    for tile, slabs in rope_slabs.items():
        plain = jnp.logical_and(plain, j != tile)

        @pl.when(j == tile)
        def _(slabs=slabs):
            cos = {False: cs_ref[:, :LANES]}
            sin = {False: cs_ref[:, LANES:]}
            if any(slabs.values()):
                cos[True] = cos[False] * q_scale
                sin[True] = sin[False] * q_scale
            for h in range(acc.shape[1] // LANES):
                t = acc[:, h * LANES:(h + 1) * LANES]
                if h in slabs:
                    partner = pltpu.roll(t, LANES // 2, 1)
                    t = t * cos[slabs[h]] + partner * sin[slabs[h]]
                o_ref[:, h * LANES:(h + 1) * LANES] = t.astype(o_ref.dtype)

    @pl.when(plain)
    def _():
        o_ref[...] = acc.astype(o_ref.dtype)


def _inproj(x2, mod, cs, w_main, w_gr, w_g2, b_g2, *, seq, tm, tn, q_col, k_col, rope_cols, q_scale):
    t, d = x2.shape
    n = w_main.shape[1]
    ng = w_g2.shape[1]
    rope_slabs = {}
    for col in range(0, n, LANES):
        for start, is_q in ((q_col, True), (k_col, False)):
            if start <= col < start + rope_cols:
                rope_slabs.setdefault(col // tn, {})[(col % tn) // LANES] = is_q
    kern = functools.partial(_inproj_kernel, rope_slabs=rope_slabs, q_scale=q_scale)
    return pl.pallas_call(
        kern,
        grid=(t // tm, n // tn),
        in_specs=[pl.BlockSpec((tm, d), lambda i, j: (i, 0)),
                  pl.BlockSpec((1,) + mod.shape[1:], lambda i, j: ((i * tm) // seq, 0, 0)),
                  pl.BlockSpec((tm, 2 * LANES), lambda i, j: (i, 0)),
                  pl.BlockSpec((d, tn), lambda i, j: (0, j)),
                  pl.BlockSpec((d, LANES), lambda i, j: (0, 0)),
                  pl.BlockSpec((LANES, ng), lambda i, j: (0, 0)),
                  pl.BlockSpec((1, ng), lambda i, j: (0, 0))],
        out_specs=[pl.BlockSpec((tm, tn), lambda i, j: (i, j)),
                   pl.BlockSpec((tm, ng), lambda i, j: (i, 0))],
        out_shape=[jax.ShapeDtypeStruct((t, n), BF16),
                   jax.ShapeDtypeStruct((t, ng), F32)],
        scratch_shapes=[pltpu.VMEM((tm, d), BF16)],
        compiler_params=pltpu.CompilerParams(
            dimension_semantics=("arbitrary", "arbitrary"), vmem_limit_bytes=VMEM_LIMIT),
        name="inproj",
    )(x2, mod, cs, w_main, w_gr, w_g2, b_g2)


def _gla_kernel(q_ref, k_ref, v_ref, g_ref, lg_ref, nw_ref, o_ref, st_ref, *, chunk, sub_rows, q_scale):
    @pl.when(pl.program_id(2) == 0)
    def _():
        st_ref[...] = jnp.zeros_like(st_ref)

    n_chunks = sub_rows // chunk
    pos = lax.broadcasted_iota(jnp.int32, (sub_rows, lg_ref.shape[1]), 0) % chunk
    r = lax.broadcasted_iota(jnp.int32, (sub_rows, sub_rows), 0)
    c = lax.broadcasted_iota(jnp.int32, (sub_rows, sub_rows), 1)
    keep = jnp.logical_and(c <= r, (r // chunk) == (c // chunk))
    st = st_ref[...]

    for si in range(q_ref.shape[0] // sub_rows):
        rs = slice(si * sub_rows, (si + 1) * sub_rows)

        b = lg_ref[rs, :]
        step = 1
        while step < chunk:
            b = b + jnp.where(pos >= step, pltpu.roll(b, step, 0), 0.0)
            step *= 2

        k = k_ref[rs, :].astype(F32)
        q_t = (q_ref[rs, :].astype(F32) * q_scale * jnp.exp(b)).astype(BF16)
        k_t = (k * jnp.exp(-b)).astype(BF16)
        v = v_ref[rs, :]

        attn = lax.dot_general(q_t, k_t, NT_DIMS, preferred_element_type=F32)
        o = jnp.dot(jnp.where(keep, attn, 0.0).astype(BF16), v, preferred_element_type=F32)

        o_inter = []
        for ci in range(n_chunks):
            sl = slice(ci * chunk, (ci + 1) * chunk)
            b_c = b[sl, :]
            b_last = b_c[chunk - 1:chunk, :]
            k_d = (k[sl, :] * jnp.exp(b_last - b_c)).astype(BF16)
            kv = lax.dot_general(v[sl, :], k_d, TN_DIMS, preferred_element_type=F32)
            o_inter.append(lax.dot_general(q_t[sl, :], st.astype(BF16), NT_DIMS, preferred_element_type=F32))
            st = st * jnp.exp(b_last) + kv
        o = o + jnp.concatenate(o_inter, axis=0)

        ms = jnp.mean(o * o, axis=-1, keepdims=True)
        y = o * lax.rsqrt(ms + RMS_EPS) * nw_ref[...] * _silu(g_ref[rs, :].astype(F32))
        o_ref[rs, :] = y.astype(o_ref.dtype)

    st_ref[...] = st


def _gla(proj, log_g, norm_w, *, bsz, seq, rows, dk, dv, q_col, k_col, v_col, g_col):
    t = proj.shape[0]
    nl = seq // rows
    kern = functools.partial(_gla_kernel, chunk=GLA_CHUNK, sub_rows=min(rows, GLA_SUB_ROWS), q_scale=dk ** -0.5)
    row = lambda b, h, l: b * nl + l
    return pl.pallas_call(
        kern,
        grid=(bsz, GLA_HEADS, nl),
        in_specs=[pl.BlockSpec((rows, dk), lambda b, h, l: (row(b, h, l), q_col // dk + h)),
                  pl.BlockSpec((rows, dk), lambda b, h, l: (row(b, h, l), k_col // dk + h)),
                  pl.BlockSpec((rows, dv), lambda b, h, l: (row(b, h, l), v_col // dv + h)),
                  pl.BlockSpec((rows, dv), lambda b, h, l: (row(b, h, l), g_col // dv + h)),
                  pl.BlockSpec((rows, dk), lambda b, h, l: (row(b, h, l), h)),
                  pl.BlockSpec((1, dv), lambda b, h, l: (0, 0))],
        out_specs=pl.BlockSpec((rows, dv), lambda b, h, l: (row(b, h, l), h)),
        out_shape=jax.ShapeDtypeStruct((t, GLA_HEADS * dv), BF16),
        scratch_shapes=[pltpu.VMEM((dv, dk), F32)],
        compiler_params=pltpu.CompilerParams(
            dimension_semantics=("arbitrary", "arbitrary", "arbitrary"), vmem_limit_bytes=VMEM_LIMIT),
        name="gla",
    )(proj, proj, proj, proj, log_g, norm_w)


def _diff_kernel(lam_ref, q_ref, qn_ref, k_ref, v_ref, nw_ref, o_ref,
                 qs_ref, vt_ref, s0_ref, sa_ref, sb_ref, m_ref, l_ref, acc_ref, *, bq, bk, dh, lam_init):
    qi = pl.program_id(2)
    q0 = qi * bq
    seq = k_ref.shape[0]

    def mask_queries(q):
        lane = lax.broadcasted_iota(jnp.int32, q.shape, 1)
        zero = jnp.zeros_like(q)
        map0 = (lane % dh) < dh // 2
        qs_ref[:bq, :] = jnp.where(map0, q, zero)
        qs_ref[bq:, :] = jnp.where(map0, zero, q)

    def scores(kj, dst):
        k0 = pl.multiple_of(kj * bk, bk)
        dst[...] = lax.dot_general(k_ref[pl.ds(k0, bk), :], qs_ref[...], NT_DIMS, preferred_element_type=F32)

    def update(src, kj, masked):
        k0 = pl.multiple_of(kj * bk, bk)
        s = src[...]
        if masked:
            key = k0 + lax.broadcasted_iota(jnp.int32, s.shape, 0)
            col = lax.broadcasted_iota(jnp.int32, s.shape, 1)
            qpos = q0 + jnp.where(col >= bq, col - bq, col)
            s = jnp.where(key <= qpos, s, -jnp.inf)
        m_prev = m_ref[...]
        m_new = jnp.maximum(m_prev, jnp.max(s, axis=0, keepdims=True))
        alpha = jnp.exp2(m_prev - m_new)
        p = jnp.exp2(s - m_new)
        l_ref[...] = alpha * l_ref[...] + jnp.sum(p, axis=0, keepdims=True)
        acc_ref[...] = alpha * acc_ref[...] + jnp.dot(
            vt_ref[:, pl.ds(k0, bk)], p.astype(BF16), preferred_element_type=F32)
        m_ref[...] = m_new

    @pl.when(qi == 0)
    def _():
        for c in range(seq // bk):
            vt_ref[:, c * bk:(c + 1) * bk] = v_ref[c * bk:(c + 1) * bk, :].astype(F32).T.astype(BF16)
        mask_queries(q_ref[...])
        scores(0, s0_ref)

    m_ref[...] = jnp.full_like(m_ref, -jnp.inf)
    l_ref[...] = jnp.zeros_like(l_ref)
    acc_ref[...] = jnp.zeros_like(acc_ref)

    n_full = (q0 + 1) // bk

    @pl.when(n_full == 0)
    def _():
        update(s0_ref, 0, True)

    @pl.when(n_full > 0)
    def _():
        scores(1, sa_ref)
        update(s0_ref, 0, False)
        rest = n_full - 1

        def pair_body(jj, carry):
            j = 1 + 2 * jj
            scores(j + 1, sb_ref)
            update(sa_ref, j, False)
            scores(j + 2, sa_ref)
            update(sb_ref, j + 1, False)
            return carry

        lax.fori_loop(0, rest // 2, pair_body, 0)
        odd = (rest % 2) == 1

        @pl.when(odd)
        def _():
            scores(n_full, sb_ref)
            update(sa_ref, n_full - 1, False)
            update(sb_ref, n_full, True)

        @pl.when(jnp.logical_not(odd))
        def _():
            update(sa_ref, n_full, True)

    mask_queries(qn_ref[...])
    scores(0, s0_ref)

    lv = lam_ref[...]
    s1 = jnp.sum(lv[0:1, :] * lv[1:2, :], axis=-1, keepdims=True)
    s2 = jnp.sum(lv[2:3, :] * lv[3:4, :], axis=-1, keepdims=True)
    lam = jnp.exp(s1) - jnp.exp(s2) + lam_init
    o_all = acc_ref[...] / l_ref[...]
    o_t = o_all[:, :bq] - lam * o_all[:, bq:]
    ms = jnp.mean(o_t * o_t, axis=0, keepdims=True)
    y = (o_t * lax.rsqrt(ms + RMS_EPS)).T * nw_ref[...] * (1.0 - lam_init)
    o_ref[...] = y.astype(o_ref.dtype)


def _diff(lam_vecs, proj, norm_w, *, bsz, seq, bq, bk, dh, q_col, k_col, v_col, lam_init):
    assert bk % bq == 0 and seq % bk == 0
    t = proj.shape[0]
    w = 2 * dh
    nq = seq // bq
    kern = functools.partial(_diff_kernel, bq=bq, bk=bk, dh=dh, lam_init=lam_init)
    return pl.pallas_call(
        kern,
        grid=(bsz, DIFF_HEADS, nq),
        in_specs=[pl.BlockSpec(lam_vecs.shape, lambda b, h, i: (0, 0)),
                  pl.BlockSpec((bq, w), lambda b, h, i: (b * nq + i, q_col // w + h)),
                  pl.BlockSpec((bq, w), lambda b, h, i: (b * nq + jnp.minimum(i + 1, nq - 1), q_col // w + h)),
                  pl.BlockSpec((seq, w), lambda b, h, i: (b, k_col // w + h)),
                  pl.BlockSpec((seq, w), lambda b, h, i: (b, v_col // w + h)),
                  pl.BlockSpec((1, w), lambda b, h, i: (0, 0))],
        out_specs=pl.BlockSpec((bq, w), lambda b, h, i: (b * nq + i, h)),
        out_shape=jax.ShapeDtypeStruct((t, DIFF_HEADS * w), BF16),
        scratch_shapes=[pltpu.VMEM((2 * bq, w), BF16),
                        pltpu.VMEM((w, seq), BF16),
                        pltpu.VMEM((bk, 2 * bq), F32),
                        pltpu.VMEM((bk, 2 * bq), F32),
                        pltpu.VMEM((bk, 2 * bq), F32),
                        pltpu.VMEM((1, 2 * bq), F32),
                        pltpu.VMEM((1, 2 * bq), F32),
                        pltpu.VMEM((w, 2 * bq), F32)],
        compiler_params=pltpu.CompilerParams(
            dimension_semantics=("arbitrary", "arbitrary", "arbitrary"), vmem_limit_bytes=VMEM_LIMIT),
        name="diff_attn",
    )(lam_vecs, proj, proj, proj, proj, norm_w)


ROW_SUB = 8


def _store_row_tiles(ref, val):
    rows = val.shape[0]
    for s in range(ROW_SUB):
        ref[pl.ds(s, rows, stride=ROW_SUB), :] = val[:, s * LANES:(s + 1) * LANES]


def _load_row_tiles(ref, rows):
    return jnp.concatenate([ref[pl.ds(s, rows, stride=ROW_SUB), :] for s in range(ROW_SUB)], axis=1)


def _tile_copy(src_hbm, dst_vmem, sem, src_row, dst_row):
    return pltpu.make_async_copy(src_hbm.at[pl.ds(src_row * ROW_SUB, ROW_SUB), :],
                                 dst_vmem.at[pl.ds(dst_row * ROW_SUB, ROW_SUB), :], sem)


def _wait_tiles(src_hbm, dst_vmem, sem):
    pltpu.make_async_copy(src_hbm.at[pl.ds(0, dst_vmem.shape[0]), :], dst_vmem, sem).wait()


def _outproj_kernel(og_ref, od_ref, ga_ref, gb_ref, x_ref, mod_ref, wo_ref, lnw_ref, lnb_ref, wr_ref, br_ref,
                    x1_ref, u2_ref, rt_ref, cnt_ref, tri_ref, run_ref, *, alpha):
    @pl.when(pl.program_id(0) == 0)
    def _():
        r = lax.broadcasted_iota(jnp.int32, tri_ref.shape, 0)
        c = lax.broadcasted_iota(jnp.int32, tri_ref.shape, 1)
        tri_ref[...] = (c < r).astype(BF16)
        run_ref[...] = jnp.zeros_like(run_ref)

    merged = jax.nn.sigmoid(ga_ref[...]) * og_ref[...] + jax.nn.sigmoid(gb_ref[...]) * od_ref[...]
    y = jnp.dot(merged, wo_ref[...], preferred_element_type=F32)
    gate1 = mod_ref[0, 2:3, :]
    shift2 = mod_ref[0, 3:4, :]
    scale2 = mod_ref[0, 4:5, :]
    x1 = _layer_norm(alpha * x_ref[...] + gate1 * y) * lnw_ref[...] + lnb_ref[...]
    x1_ref[...] = x1
    u2 = _layer_norm(x1) * (1.0 + scale2) + shift2
    _store_row_tiles(u2_ref, u2)

    logits = jnp.dot(u2.astype(BF16), wr_ref[...], preferred_element_type=F32) + br_ref[...]
    lane = lax.broadcasted_iota(jnp.int32, logits.shape, 1)
    neg = jnp.float32(-jnp.inf)
    big = jnp.int32(LANES)
    lg = jnp.where(lane < N_GROUPS, logits, neg)
    g_max = jnp.max(lg, axis=-1, keepdims=True)
    w_grp = 1.0 / jnp.sum(jnp.exp(lg - g_max), axis=-1, keepdims=True)
    g_idx = jnp.min(jnp.where(lg == g_max, lane, big), axis=-1, keepdims=True)
    lo = N_GROUPS + EXPERTS_PER_GROUP * g_idx
    le = jnp.where(jnp.logical_and(lane >= lo, lane < lo + EXPERTS_PER_GROUP), logits, neg)
    v1 = jnp.max(le, axis=-1, keepdims=True)
    i1 = jnp.min(jnp.where(le == v1, lane, big), axis=-1, keepdims=True)
    le2 = jnp.where(lane == i1, neg, le)
    v2 = jnp.max(le2, axis=-1, keepdims=True)
    i2 = jnp.min(jnp.where(le2 == v2, lane, big), axis=-1, keepdims=True)
    e2 = jnp.exp(v2 - v1)
    den = 1.0 + e2
    c1 = w_grp / den
    c2 = w_grp * e2 / den

    sel1 = lane == i1 - N_GROUPS
    sel2 = lane == i2 - N_GROUPS
    onehot = jnp.logical_or(sel1, sel2).astype(BF16)
    before = jnp.dot(tri_ref[...], onehot, preferred_element_type=F32) + run_ref[...]
    pos1 = jnp.sum(jnp.where(sel1, before, 0.0), axis=-1, keepdims=True)
    pos2 = jnp.sum(jnp.where(sel2, before, 0.0), axis=-1, keepdims=True)
    run_ref[...] += jnp.sum(onehot.astype(F32), axis=0, keepdims=True)
    cnt_ref[...] = jnp.broadcast_to(run_ref[...], cnt_ref.shape)

    cols = ((i1 - N_GROUPS).astype(F32), (i2 - N_GROUPS).astype(F32), c1, c2, pos1, pos2)
    rt = jnp.zeros_like(logits)
    for li, col in enumerate(cols):
        rt = jnp.where(lane == li, col, rt)
    rt_ref[...] = rt


def _outproj(o_gla, o_diff, proj, x2, mod, w_out, ln_w, ln_b, w_r, b_r, *, seq, tm, ga_col, alpha):
    t, d = x2.shape
    assert d == ROW_SUB * LANES
    kern = functools.partial(_outproj_kernel, alpha=alpha)
    row_spec = pl.BlockSpec((tm, d), lambda i: (i, 0))
    vec_spec = pl.BlockSpec((1, d), lambda i: (0, 0))
    return pl.pallas_call(
        kern,
        grid=(t // tm,),
        in_specs=[row_spec, row_spec,
                  pl.BlockSpec((tm, d), lambda i: (i, ga_col // d)),
                  pl.BlockSpec((tm, d), lambda i: (i, ga_col // d + 1)),
                  row_spec,
                  pl.BlockSpec((1,) + mod.shape[1:], lambda i: ((i * tm) // seq, 0, 0)),
                  pl.BlockSpec((d, d), lambda i: (0, 0)),
                  vec_spec, vec_spec,
                  pl.BlockSpec((d, LANES), lambda i: (0, 0)),
                  pl.BlockSpec((1, LANES), lambda i: (0, 0))],
        out_specs=[row_spec,
                   pl.BlockSpec((tm * ROW_SUB, LANES), lambda i: (i, 0)),
                   pl.BlockSpec((tm, LANES), lambda i: (i, 0)),
                   pl.BlockSpec((ROW_SUB, LANES), lambda i: (0, 0))],
        out_shape=[jax.ShapeDtypeStruct((t, d), F32),
                   jax.ShapeDtypeStruct((t * ROW_SUB, LANES), F32),
                   jax.ShapeDtypeStruct((t, LANES), F32),
                   jax.ShapeDtypeStruct((ROW_SUB, LANES), F32)],
        scratch_shapes=[pltpu.VMEM((tm, tm), BF16),
                        pltpu.VMEM((1, LANES), F32)],
        compiler_params=pltpu.CompilerParams(
            dimension_semantics=("arbitrary",), vmem_limit_bytes=VMEM_LIMIT),
        name="outproj",
    )(o_gla, o_diff, proj, proj, x2, mod, w_out, ln_w, ln_b, w_r, b_r)


def _dispatch_kernel(fill_ref, d0_ref, d1_ref, u_ref, xs_hbm, zbuf, sem_fill, sem, *, tm, blk):
    i = pl.program_id(0)

    @pl.when(i == 0)
    def _():
        zbuf[...] = jnp.zeros_like(zbuf)
        n_fill = fill_ref.shape[0]

        def fill_copy(j):
            return pltpu.make_async_copy(zbuf, xs_hbm.at[pl.ds(fill_ref[j] * ROW_SUB, blk * ROW_SUB), :], sem_fill)

        def start_fill(j, carry):
            @pl.when(fill_ref[j] >= 0)
            def _():
                fill_copy(j).start()
            return carry

        def wait_fill(j, carry):
            @pl.when(fill_ref[j] >= 0)
            def _():
                fill_copy(j).wait()
            return carry

        lax.fori_loop(0, n_fill, start_fill, 0)
        lax.fori_loop(0, n_fill, wait_fill, 0)

    def row_copy(dest_ref, r):
        return pltpu.make_async_copy(u_ref.at[pl.ds(r * ROW_SUB, ROW_SUB), :],
                                     xs_hbm.at[pl.ds(dest_ref[0, 0, r] * ROW_SUB, ROW_SUB), :], sem)

    def body(r, carry):
        row_copy(d0_ref, r).start(priority=0)
        row_copy(d1_ref, r).start(priority=1)
        return carry

    lax.fori_loop(0, tm, body, 0, unroll=4)
    for _ in range(2):
        pltpu.make_async_copy(u_ref, xs_hbm.at[pl.ds(0, tm * ROW_SUB), :], sem).wait()


def _dispatch(fill_start, dest0, dest1, u2_tiles, *, p_rows, tm, blk):
    t = u2_tiles.shape[0] // ROW_SUB
    kern = functools.partial(_dispatch_kernel, tm=tm, blk=blk)
    idx_spec = pl.BlockSpec((1, 1, tm), lambda i, fs: (i, 0, 0), memory_space=pltpu.SMEM)
    grid_spec = pltpu.PrefetchScalarGridSpec(
        num_scalar_prefetch=1,
        grid=(t // tm,),
        in_specs=[idx_spec, idx_spec, pl.BlockSpec((tm * ROW_SUB, LANES), lambda i, fs: (i, 0))],
        out_specs=pl.BlockSpec(memory_space=pl.ANY),
        scratch_shapes=[pltpu.VMEM((blk * ROW_SUB, LANES), F32),
                        pltpu.SemaphoreType.DMA(()),
                        pltpu.SemaphoreType.DMA(())],
    )
    return pl.pallas_call(
        kern,
        grid_spec=grid_spec,
        out_shape=jax.ShapeDtypeStruct((p_rows * ROW_SUB, LANES), F32),
        compiler_params=pltpu.CompilerParams(dimension_semantics=("arbitrary",)),
        name="dispatch",
    )(fill_start, dest0, dest1, u2_tiles)


def _expert_kernel(be_ref, na_ref, x_ref, wg_ref, wu_ref, wd_ref, o_ref, wgb, wub, wdb, *, blk):
    i = pl.program_id(0)
    active = i < na_ref[0]

    @pl.when(active)
    def _():
        changed = jnp.logical_or(i == 0, be_ref[i] != be_ref[jnp.maximum(i - 1, 0)])

        @pl.when(changed)
        def _():
            wgb[...] = wg_ref[0].astype(BF16)
            wub[...] = wu_ref[0].astype(BF16)
            wdb[...] = wd_ref[0].astype(BF16)

        xb = _load_row_tiles(x_ref, blk).astype(BF16)
        hg = jnp.dot(xb, wgb[...], preferred_element_type=F32)
        hu = jnp.dot(xb, wub[...], preferred_element_type=F32)
        h = (_silu(hg) * hu).astype(BF16)
        _store_row_tiles(o_ref, jnp.dot(h, wdb[...], preferred_element_type=F32))

    @pl.when(jnp.logical_not(active))
    def _():
        o_ref[...] = jnp.zeros_like(o_ref)


def _experts(block_expert, n_active, xs_tiles, w_gate, w_up, w_down, *, blk):
    nblk = block_expert.shape[0]
    e, d, ff = w_gate.shape
    kern = functools.partial(_expert_kernel, blk=blk)
    grid_spec = pltpu.PrefetchScalarGridSpec(
        num_scalar_prefetch=2,
        grid=(nblk,),
        in_specs=[pl.BlockSpec((blk * ROW_SUB, LANES), lambda i, be, na: (jnp.minimum(i, na[0] - 1), 0)),
                  pl.BlockSpec((1, d, ff), lambda i, be, na: (be[i], 0, 0)),
                  pl.BlockSpec((1, d, ff), lambda i, be, na: (be[i], 0, 0)),
                  pl.BlockSpec((1, ff, d), lambda i, be, na: (be[i], 0, 0))],
        out_specs=pl.BlockSpec((blk * ROW_SUB, LANES), lambda i, be, na: (i, 0)),
        scratch_shapes=[pltpu.VMEM((d, ff), BF16),
                        pltpu.VMEM((d, ff), BF16),
                        pltpu.VMEM((ff, d), BF16)],
    )
    return pl.pallas_call(
        kern,
        grid_spec=grid_spec,
        out_shape=jax.ShapeDtypeStruct((nblk * blk * ROW_SUB, LANES), F32),
        compiler_params=pltpu.CompilerParams(
            dimension_semantics=("arbitrary",), vmem_limit_bytes=VMEM_LIMIT),
        name="experts",
    )(block_expert, n_active, xs_tiles, w_gate, w_up, w_down)


def _combine_kernel(d0_ref, d0n_ref, d1_ref, d1n_ref, rows_hbm, x1_ref, rt_ref, mod_ref, lnw_ref, lnb_ref,
                    o_ref, buf, sem, *, tm, alpha):
    i = pl.program_id(0)
    n = pl.num_programs(0)
    slot = i % 2

    def start_gather(i0_ref, i1_ref, s):
        def body(r, carry):
            _tile_copy(rows_hbm, buf.at[s, 0], sem.at[s], i0_ref[0, 0, r], r).start(priority=0)
            _tile_copy(rows_hbm, buf.at[s, 1], sem.at[s], i1_ref[0, 0, r], r).start(priority=1)
            return carry
        lax.fori_loop(0, tm, body, 0, unroll=4)

    def wait_gather(s):
        _wait_tiles(rows_hbm, buf.at[s, 0], sem.at[s])
        _wait_tiles(rows_hbm, buf.at[s, 1], sem.at[s])

    @pl.when(i == 0)
    def _():
        start_gather(d0_ref, d1_ref, 0)

    start_gather(d0n_ref, d1n_ref, 1 - slot)
    wait_gather(slot)
    rt = rt_ref[...]
    y = rt[:, 2:3] * _load_row_tiles(buf.at[slot, 0], tm) + rt[:, 3:4] * _load_row_tiles(buf.at[slot, 1], tm)
    gate2 = mod_ref[0, 5:6, :]
    z = alpha * x1_ref[...] + gate2 * y
    o_ref[...] = _layer_norm(z) * lnw_ref[...] + lnb_ref[...]

    @pl.when(i == n - 1)
    def _():
        wait_gather(1 - slot)


def _combine(dest0, dest1, rows_tiles, x1, route, mod, ln_w, ln_b, *, seq, tm, alpha):
    t, d = x1.shape
    nt = t // tm
    kern = functools.partial(_combine_kernel, tm=tm, alpha=alpha)
    cur = lambda i: (i, 0, 0)
    nxt = lambda i: (jnp.minimum(i + 1, nt - 1), 0, 0)
    idx_spec = lambda m: pl.BlockSpec((1, 1, tm), m, memory_space=pltpu.SMEM)
    row_spec = pl.BlockSpec((tm, d), lambda i: (i, 0))
    vec_spec = pl.BlockSpec((1, d), lambda i: (0, 0))
    return pl.pallas_call(
        kern,
        grid=(nt,),
        in_specs=[idx_spec(cur), idx_spec(nxt), idx_spec(cur), idx_spec(nxt),
                  pl.BlockSpec(memory_space=pl.ANY),
                  row_spec,
                  pl.BlockSpec((tm, LANES), lambda i: (i, 0)),
                  pl.BlockSpec((1,) + mod.shape[1:], lambda i: ((i * tm) // seq, 0, 0)),
                  vec_spec, vec_spec],
        out_specs=row_spec,
        out_shape=jax.ShapeDtypeStruct((t, d), F32),
        scratch_shapes=[pltpu.VMEM((2, 2, tm * ROW_SUB, LANES), F32),
                        pltpu.SemaphoreType.DMA((2,))],
        compiler_params=pltpu.CompilerParams(
            dimension_semantics=("arbitrary",), vmem_limit_bytes=VMEM_LIMIT),
        name="combine",
    )(dest0, dest0, dest1, dest1, rows_tiles, x1, route, mod, ln_w, ln_b)


def _dispatch_plan(route, counts, blk):
    t = route.shape[0]
    eid = route[:, :2].astype(jnp.int32)
    pos = route[:, 4:6].astype(jnp.int32)
    counts = counts[:N_EXPERTS].astype(jnp.int32)
    pcounts = ((counts + blk - 1) // blk) * blk
    pend = jnp.cumsum(pcounts)
    pstart = pend - pcounts
    experts = jnp.arange(N_EXPERTS, dtype=jnp.int32)
    dest = jnp.sum(jnp.where(eid[:, :, None] == experts, pstart, 0), axis=-1) + pos
    p_rows = ((2 * t + N_EXPERTS * (blk - 1) + blk - 1) // blk) * blk
    nblk = p_rows // blk
    block_start = jnp.arange(nblk, dtype=jnp.int32) * blk
    block_expert = jnp.minimum(jnp.sum((pend[None, :] <= block_start[:, None]).astype(jnp.int32), axis=1),
                               N_EXPERTS - 1)
    n_active = (pend[N_EXPERTS - 1:] // blk).astype(jnp.int32)
    tail = pend[N_EXPERTS - 1] + jnp.arange(N_EXPERTS, dtype=jnp.int32) * blk
    fill_start = jnp.concatenate([jnp.where(pcounts > 0, pend - blk, -1), jnp.where(tail < p_rows, tail, -1)])
    return dest[:, 0], dest[:, 1], fill_start, block_expert, n_active, p_rows


def _layer(x, c, positions, w_ada, b_ada, w_in, w_g2, b_g2, gla_nw, lq1, lk1, lq2, lk2, diff_nw, w_out,
           ln1_w, ln1_b, w_rg, b_rg, w_re, b_re, w_eg, w_eu, w_ed, ln2_w, ln2_b, *, lam_init,
           tm_in, tn_in, gla_rows, bq, bk, tm_out, moe_blk, tm_dsp, tm_cmb):
    bsz, seq, d = x.shape
    t = bsz * seq
    alpha = (2.0 * DEPTH) ** 0.25
    gla_dk = d // (2 * GLA_HEADS)
    gla_dv = d // GLA_HEADS
    dh = d // (2 * DIFF_HEADS)
    hk = GLA_HEADS * gla_dk
    hv = GLA_HEADS * gla_dv
    dq = DIFF_HEADS * 2 * dh
    gr_col = 2 * hk + 2 * hv
    q_col, k_col = 0, hk
    v_col, g_col = 2 * hk, 2 * hk + hv
    dq_col = gr_col
    dk_col = dq_col + dq
    dv_col = dk_col + dq
    mg_col = dv_col + dq
    half = dh // 2
    lane = np.arange(2 * dh)
    in_head = (lane // half % 2) * dh + (lane // dh) * half + lane % half
    cols = (np.arange(DIFF_HEADS)[:, None] * 2 * dh + in_head[None, :]).reshape(-1)
    src = np.arange(w_in.shape[1] - GLA_GATE_RANK)
    src[dq_col:dq_col + dq] = dq_col + cols
    src[dk_col:dk_col + dq] = dk_col + cols
    src = np.where(src >= gr_col, src + GLA_GATE_RANK, src)
    w_main = w_in[:, src].astype(BF16)
    w_gr = jnp.pad(w_in[:, gr_col:gr_col + GLA_GATE_RANK], ((0, 0), (0, LANES - GLA_GATE_RANK))).astype(BF16)
    w_g2p = jnp.pad(w_g2, ((0, LANES - GLA_GATE_RANK), (0, 0))).astype(BF16)
    x2 = x.reshape(t, d)

    ada = _ada(c, w_ada, b_ada.reshape(1, -1))
    mod = ada.reshape(bsz, 6, d)

    inv = ROPE_THETA ** (-jnp.arange(half, dtype=F32) / half)
    inv_row = jnp.tile(inv, LANES // half).reshape(1, LANES)
    cs = _rope_tables(positions.reshape(t, 1), inv_row, tm_in)

    proj, log_g = _inproj(x2, mod, cs, w_main, w_gr, w_g2p, b_g2.reshape(1, -1), seq=seq, tm=tm_in, tn=tn_in,
                          q_col=dq_col, k_col=dk_col, rope_cols=dq, q_scale=dh ** -0.5 * LOG2_E)

    o_gla = _gla(proj, log_g, gla_nw.reshape(1, -1), bsz=bsz, seq=seq, rows=gla_rows, dk=gla_dk, dv=gla_dv,
                 q_col=q_col, k_col=k_col, v_col=v_col, g_col=g_col)

    lam_vecs = jnp.pad(jnp.stack([lq1, lk1, lq2, lk2]), ((0, 4), (0, LANES - dh)))
    o_diff = _diff(lam_vecs, proj, diff_nw.reshape(1, -1), bsz=bsz, seq=seq, bq=bq, bk=bk, dh=dh,
                   q_col=dq_col, k_col=dk_col, v_col=dv_col, lam_init=lam_init)

    w_r = jnp.pad(jnp.concatenate([w_rg, w_re], axis=1), ((0, 0), (0, LANES - N_GROUPS - N_EXPERTS))).astype(BF16)
    b_r = jnp.pad(jnp.concatenate([b_rg, b_re]), (0, LANES - N_GROUPS - N_EXPERTS)).reshape(1, LANES)
    x1, u2_tiles, route, counts = _outproj(o_gla, o_diff, proj, x2, mod, w_out.astype(BF16), ln1_w.reshape(1, -1),
                                           ln1_b.reshape(1, -1), w_r, b_r, seq=seq, tm=tm_out, ga_col=mg_col,
                                           alpha=alpha)

    dest0, dest1, fill_start, block_expert, n_active, p_rows = _dispatch_plan(route, counts[0], moe_blk)
    xs_tiles = _dispatch(fill_start, dest0.reshape(-1, 1, tm_dsp), dest1.reshape(-1, 1, tm_dsp), u2_tiles,
                         p_rows=p_rows, tm=tm_dsp, blk=moe_blk)
    rows = _experts(block_expert, n_active, xs_tiles, w_eg, w_eu, w_ed, blk=moe_blk)
    out = _combine(dest0.reshape(-1, 1, tm_cmb), dest1.reshape(-1, 1, tm_cmb), rows, x1, route, mod,
                   ln2_w.reshape(1, -1), ln2_b.reshape(1, -1), seq=seq, tm=tm_cmb, alpha=alpha)
    return out.reshape(bsz, seq, d)


def kernel(x, c, positions, w_ada, b_ada, w_in, w_gla_gate2, b_gla_gate2, gla_norm_w, diff_lambda_q1,
           diff_lambda_k1, diff_lambda_q2, diff_lambda_k2, diff_norm_w, w_out, ln1_w, ln1_b, w_router_group,
           b_router_group, w_router_expert, b_router_expert, w_exp_gate, w_exp_up, w_exp_down, ln2_w, ln2_b):
    assert w_ada.shape[0] == DEPTH
    for l in range(DEPTH):
        lam_init = 0.8 - 0.6 * math.exp(-0.3 * l)
        x = _layer(x, c, positions, w_ada[l], b_ada[l], w_in[l], w_gla_gate2[l], b_gla_gate2[l], gla_norm_w[l],
                   diff_lambda_q1[l], diff_lambda_k1[l], diff_lambda_q2[l], diff_lambda_k2[l], diff_norm_w[l],
                   w_out[l], ln1_w[l], ln1_b[l], w_router_group[l], b_router_group[l], w_router_expert[l],
                   b_router_expert[l], w_exp_gate[l], w_exp_up[l], w_exp_down[l], ln2_w[l], ln2_b[l],
                   lam_init=lam_init, tm_in=1024, tn_in=2048, gla_rows=2048, bq=512, bk=512, tm_out=512,
                   moe_blk=256, tm_dsp=1024, tm_cmb=512)
    return x
```

```python
import functools
import math

import jax
import jax.numpy as jnp
import numpy as np
from jax import lax
from jax.experimental import pallas as pl
from jax.experimental.pallas import tpu as pltpu

F32 = jnp.float32
BF16 = jnp.bfloat16
HIGHEST = lax.Precision.HIGHEST

DEPTH = 1
GLA_HEADS = 4
GLA_GATE_RANK = 16
GLA_TAU = 16.0
GLA_CHUNK = 64
GLA_SUB_ROWS = 512
DIFF_HEADS = 8
ROPE_THETA = 10000.0
N_GROUPS = 4
EXPERTS_PER_GROUP = 8
N_EXPERTS = N_GROUPS * EXPERTS_PER_GROUP
LN_EPS = 1e-5
RMS_EPS = 1e-6
LOG2_E = math.log2(math.e)
LANES = 128
VMEM_LIMIT = 56 * 1024 * 1024

NT_DIMS = (((1,), (1,)), ((), ()))
TN_DIMS = (((0,), (0,)), ((), ()))


def _layer_norm(x):
    mu = jnp.mean(x, axis=-1, keepdims=True)
    xc = x - mu
    var = jnp.mean(xc * xc, axis=-1, keepdims=True)
    return xc * lax.rsqrt(var + LN_EPS)


def _silu(x):
    return x * jax.nn.sigmoid(x)


def _ada_kernel(c_ref, w_ref, b_ref, o_ref):
    s = _silu(c_ref[...])
    o_ref[...] = jnp.dot(s, w_ref[...], preferred_element_type=F32, precision=HIGHEST) + b_ref[...]


def _ada(c, w, b):
    bsz, d = c.shape
    n = w.shape[1]
    tn = d
    return pl.pallas_call(
        _ada_kernel,
        grid=(n // tn,),
        in_specs=[pl.BlockSpec((bsz, d), lambda j: (0, 0)),
                  pl.BlockSpec((d, tn), lambda j: (0, j)),
                  pl.BlockSpec((1, tn), lambda j: (0, j))],
        out_specs=pl.BlockSpec((bsz, tn), lambda j: (0, j)),
        out_shape=jax.ShapeDtypeStruct((bsz, n), F32),
        name="ada",
    )(c, w, b)


ROPE_PACK = 4


def _rope_kernel(pos_ref, inv_ref, cos_ref, sin_ref):
    rows = pos_ref.shape[0]
    group = LANES // ROPE_PACK
    lane = lax.broadcasted_iota(jnp.int32, (rows, LANES), 1)
    p = pos_ref[...].astype(F32)
    pos = p[:, ROPE_PACK - 1:ROPE_PACK]
    for k in range(ROPE_PACK - 2, -1, -1):
        pos = jnp.where(lane // group == k, p[:, k:k + 1], pos)
    ang = pos * inv_ref[...]
    cos = jnp.cos(ang)
    sin = jnp.sin(ang)

    def spread(x, k):
        y = jnp.where(lane // group == k, x, 0.0)
        out = y
        for g in range(1, ROPE_PACK):
            out = out + pltpu.roll(y, g * group, 1)
        return out

    for k in range(ROPE_PACK):
        cos_ref[pl.ds(k, rows, stride=ROPE_PACK), :] = spread(cos, k)
        s_k = spread(sin, k)
        sin_ref[pl.ds(k, rows, stride=ROPE_PACK), :] = jnp.where(lane < LANES // 2, -s_k, s_k)


def _rope_tables(pos_packed, inv_row, tm):
    t = pos_packed.shape[0] * ROPE_PACK
    out = jax.ShapeDtypeStruct((t, LANES), F32)
    return pl.pallas_call(
        _rope_kernel,
        grid=(t // tm,),
        in_specs=[pl.BlockSpec((tm // ROPE_PACK, ROPE_PACK), lambda i: (i, 0)),
                  pl.BlockSpec((1, LANES), lambda i: (0, 0))],
        out_specs=[pl.BlockSpec((tm, LANES), lambda i: (i, 0)), pl.BlockSpec((tm, LANES), lambda i: (i, 0))],
        out_shape=[out, out],
        name="rope_tables",
    )(pos_packed, inv_row)


def _inproj_kernel(x_ref, mod_ref, cos_ref, sin_ref, w_ref, wgr_ref, wg2_ref, bg2_ref, o_ref, lg_ref, u_ref,
                   *, rope_slabs, q_scale):
    j = pl.program_id(1)

    @pl.when(j == 0)
    def _():
        shift = mod_ref[0, 0:1, :]
        scale = mod_ref[0, 1:2, :]
        u = (_layer_norm(x_ref[...]) * (1.0 + scale) + shift).astype(BF16)
        u_ref[...] = u
        gr = jnp.dot(u, wgr_ref[...], preferred_element_type=F32)
        z = jnp.dot(gr.astype(BF16), wg2_ref[...], preferred_element_type=F32) + bg2_ref[...]
        log_sig = jnp.minimum(z, 0.0) - jnp.log(1.0 + jnp.exp(-jnp.abs(z)))
        lg_ref[...] = log_sig * (1.0 / GLA_TAU)

    acc = jnp.dot(u_ref[...], w_ref[...], preferred_element_type=F32)
    plain = j >= 0
    for tile, slabs in rope_slabs.items():
        plain = jnp.logical_and(plain, j != tile)

        @pl.when(j == tile)
        def _(slabs=slabs):
            cos = {False: cos_ref[...]}
            sin = {False: sin_ref[...]}
            if any(slabs.values()):
                cos[True] = cos[False] * q_scale
                sin[True] = sin[False] * q_scale
            for h in range(acc.shape[1] // LANES):
                t = acc[:, h * LANES:(h + 1) * LANES]
                if h in slabs:
                    partner = pltpu.roll(t, LANES // 2, 1)
                    t = t * cos[slabs[h]] + partner * sin[slabs[h]]
                o_ref[:, h * LANES:(h + 1) * LANES] = t.astype(o_ref.dtype)

    @pl.when(plain)
    def _():
        o_ref[...] = acc.astype(o_ref.dtype)


def _inproj(x2, mod, cos, sin, w_main, w_gr, w_g2, b_g2, *, seq, tm, tn, q_col, k_col, rope_cols, q_scale):
    t, d = x2.shape
    n = w_main.shape[1]
    ng = w_g2.shape[1]
    rope_slabs = {}
    for col in range(0, n, LANES):
        for start, is_q in ((q_col, True), (k_col, False)):
            if start <= col < start + rope_cols:
                rope_slabs.setdefault(col // tn, {})[(col % tn) // LANES] = is_q
    kern = functools.partial(_inproj_kernel, rope_slabs=rope_slabs, q_scale=q_scale)
    return pl.pallas_call(
        kern,
        grid=(t // tm, n // tn),
        in_specs=[pl.BlockSpec((tm, d), lambda i, j: (i, 0)),
                  pl.BlockSpec((1,) + mod.shape[1:], lambda i, j: ((i * tm) // seq, 0, 0)),
                  pl.BlockSpec((tm, LANES), lambda i, j: (i, 0)),
                  pl.BlockSpec((tm, LANES), lambda i, j: (i, 0)),
                  pl.BlockSpec((d, tn), lambda i, j: (0, j)),
                  pl.BlockSpec((d, LANES), lambda i, j: (0, 0)),
                  pl.BlockSpec((LANES, ng), lambda i, j: (0, 0)),
                  pl.BlockSpec((1, ng), lambda i, j: (0, 0))],
        out_specs=[pl.BlockSpec((tm, tn), lambda i, j: (i, j)),
                   pl.BlockSpec((tm, ng), lambda i, j: (i, 0))],
        out_shape=[jax.ShapeDtypeStruct((t, n), BF16),
                   jax.ShapeDtypeStruct((t, ng), F32)],
        scratch_shapes=[pltpu.VMEM((tm, d), BF16)],
        compiler_params=pltpu.CompilerParams(
            dimension_semantics=("arbitrary", "arbitrary"), vmem_limit_bytes=VMEM_LIMIT),
        name="inproj",
    )(x2, mod, cos, sin, w_main, w_gr, w_g2, b_g2)


def _gla_kernel(q_ref, k_ref, v_ref, g_ref, lg_ref, nw_ref, o_ref, st_ref, *, chunk, sub_rows, q_scale):
    @pl.when(pl.program_id(2) == 0)
    def _():
        st_ref[...] = jnp.zeros_like(st_ref)

    n_chunks = sub_rows // chunk
    pos = lax.broadcasted_iota(jnp.int32, (sub_rows, lg_ref.shape[1]), 0) % chunk
    r = lax.broadcasted_iota(jnp.int32, (sub_rows, sub_rows), 0)
    c = lax.broadcasted_iota(jnp.int32, (sub_rows, sub_rows), 1)
    keep = jnp.logical_and(c <= r, (r // chunk) == (c // chunk))
    st = st_ref[...]

    for si in range(q_ref.shape[0] // sub_rows):
        rs = slice(si * sub_rows, (si + 1) * sub_rows)

        b = lg_ref[rs, :]
        step = 1
        while step < chunk:
            b = b + jnp.where(pos >= step, pltpu.roll(b, step, 0), 0.0)
            step *= 2

        k = k_ref[rs, :].astype(F32)
        q_t = (q_ref[rs, :].astype(F32) * q_scale * jnp.exp(b)).astype(BF16)
        k_t = (k * jnp.exp(-b)).astype(BF16)
        v = v_ref[rs, :]

        attn = lax.dot_general(q_t, k_t, NT_DIMS, preferred_element_type=F32)
        o = jnp.dot(jnp.where(keep, attn, 0.0).astype(BF16), v, preferred_element_type=F32)

        o_inter = []
        for ci in range(n_chunks):
            sl = slice(ci * chunk, (ci + 1) * chunk)
            b_c = b[sl, :]
            b_last = b_c[chunk - 1:chunk, :]
            k_d = (k[sl, :] * jnp.exp(b_last - b_c)).astype(BF16)
            kv = lax.dot_general(v[sl, :], k_d, TN_DIMS, preferred_element_type=F32)
            o_inter.append(lax.dot_general(q_t[sl, :], st.astype(BF16), NT_DIMS, preferred_element_type=F32))
            st = st * jnp.exp(b_last) + kv
        o = o + jnp.concatenate(o_inter, axis=0)

        ms = jnp.mean(o * o, axis=-1, keepdims=True)
        y = o * lax.rsqrt(ms + RMS_EPS) * nw_ref[...] * _silu(g_ref[rs, :].astype(F32))
        o_ref[rs, :] = y.astype(o_ref.dtype)

    st_ref[...] = st


def _gla(proj, log_g, norm_w, *, bsz, seq, rows, dk, dv, q_col, k_col, v_col, g_col):
    t = proj.shape[0]
    nl = seq // rows
    kern = functools.partial(_gla_kernel, chunk=GLA_CHUNK, sub_rows=min(rows, GLA_SUB_ROWS), q_scale=dk ** -0.5)
    row = lambda b, h, l: b * nl + l
    return pl.pallas_call(
        kern,
        grid=(bsz, GLA_HEADS, nl),
        in_specs=[pl.BlockSpec((rows, dk), lambda b, h, l: (row(b, h, l), q_col // dk + h)),
                  pl.BlockSpec((rows, dk), lambda b, h, l: (row(b, h, l), k_col // dk + h)),
                  pl.BlockSpec((rows, dv), lambda b, h, l: (row(b, h, l), v_col // dv + h)),
                  pl.BlockSpec((rows, dv), lambda b, h, l: (row(b, h, l), g_col // dv + h)),
                  pl.BlockSpec((rows, dk), lambda b, h, l: (row(b, h, l), h)),
                  pl.BlockSpec((1, dv), lambda b, h, l: (0, 0))],
        out_specs=pl.BlockSpec((rows, dv), lambda b, h, l: (row(b, h, l), h)),
        out_shape=jax.ShapeDtypeStruct((t, GLA_HEADS * dv), BF16),
        scratch_shapes=[pltpu.VMEM((dv, dk), F32)],
        compiler_params=pltpu.CompilerParams(
            dimension_semantics=("arbitrary", "arbitrary", "arbitrary"), vmem_limit_bytes=VMEM_LIMIT),
        name="gla",
    )(proj, proj, proj, proj, log_g, norm_w)


def _diff_kernel(lam_ref, q_ref, qn_ref, k_ref, v_ref, nw_ref, o_ref,
                 qs_ref, vt_ref, s0_ref, sa_ref, sb_ref, m_ref, l_ref, acc_ref, *, bq, bk, dh, lam_init):
    qi = pl.program_id(2)
    q0 = qi * bq
    seq = k_ref.shape[0]

    def mask_queries(q):
        lane = lax.broadcasted_iota(jnp.int32, q.shape, 1)
        zero = jnp.zeros_like(q)
        map0 = (lane % dh) < dh // 2
        qs_ref[:bq, :] = jnp.where(map0, q, zero)
        qs_ref[bq:, :] = jnp.where(map0, zero, q)

    def scores(kj, dst):
        k0 = pl.multiple_of(kj * bk, bk)
        dst[...] = lax.dot_general(k_ref[pl.ds(k0, bk), :], qs_ref[...], NT_DIMS, preferred_element_type=F32)

    def update(src, kj, masked):
        k0 = pl.multiple_of(kj * bk, bk)
        s = src[...]
        if masked:
            key = k0 + lax.broadcasted_iota(jnp.int32, s.shape, 0)
            col = lax.broadcasted_iota(jnp.int32, s.shape, 1)
            qpos = q0 + jnp.where(col >= bq, col - bq, col)
            s = jnp.where(key <= qpos, s, -jnp.inf)
        m_prev = m_ref[...]
        m_new = jnp.maximum(m_prev, jnp.max(s, axis=0, keepdims=True))
        alpha = jnp.exp2(m_prev - m_new)
        p = jnp.exp2(s - m_new)
        l_ref[...] = alpha * l_ref[...] + jnp.sum(p, axis=0, keepdims=True)
        acc_ref[...] = alpha * acc_ref[...] + jnp.dot(
            vt_ref[:, pl.ds(k0, bk)], p.astype(BF16), preferred_element_type=F32)
        m_ref[...] = m_new

    @pl.when(qi == 0)
    def _():
        for c in range(seq // bk):
            vt_ref[:, c * bk:(c + 1) * bk] = v_ref[c * bk:(c + 1) * bk, :].astype(F32).T.astype(BF16)
        mask_queries(q_ref[...])
        scores(0, s0_ref)

    m_ref[...] = jnp.full_like(m_ref, -jnp.inf)
    l_ref[...] = jnp.zeros_like(l_ref)
    acc_ref[...] = jnp.zeros_like(acc_ref)

    n_full = (q0 + 1) // bk

    @pl.when(n_full == 0)
    def _():
        update(s0_ref, 0, True)

    @pl.when(n_full > 0)
    def _():
        scores(1, sa_ref)
        update(s0_ref, 0, False)
        rest = n_full - 1

        def pair_body(jj, carry):
            j = 1 + 2 * jj
            scores(j + 1, sb_ref)
            update(sa_ref, j, False)
            scores(j + 2, sa_ref)
            update(sb_ref, j + 1, False)
            return carry

        lax.fori_loop(0, rest // 2, pair_body, 0)
        odd = (rest % 2) == 1

        @pl.when(odd)
        def _():
            scores(n_full, sb_ref)
            update(sa_ref, n_full - 1, False)
            update(sb_ref, n_full, True)

        @pl.when(jnp.logical_not(odd))
        def _():
            update(sa_ref, n_full, True)

    mask_queries(qn_ref[...])
    scores(0, s0_ref)

    lv = lam_ref[...]
    s1 = jnp.sum(lv[0:1, :] * lv[1:2, :], axis=-1, keepdims=True)
    s2 = jnp.sum(lv[2:3, :] * lv[3:4, :], axis=-1, keepdims=True)
    lam = jnp.exp(s1) - jnp.exp(s2) + lam_init
    o_all = acc_ref[...] / l_ref[...]
    o_t = o_all[:, :bq] - lam * o_all[:, bq:]
    ms = jnp.mean(o_t * o_t, axis=0, keepdims=True)
    y = (o_t * lax.rsqrt(ms + RMS_EPS)).T * nw_ref[...] * (1.0 - lam_init)
    o_ref[...] = y.astype(o_ref.dtype)


def _diff(lam_vecs, proj, norm_w, *, bsz, seq, bq, bk, dh, q_col, k_col, v_col, lam_init):
    assert bk % bq == 0 and seq % bk == 0
    t = proj.shape[0]
    w = 2 * dh
    nq = seq // bq
    kern = functools.partial(_diff_kernel, bq=bq, bk=bk, dh=dh, lam_init=lam_init)
    return pl.pallas_call(
        kern,
        grid=(bsz, DIFF_HEADS, nq),
        in_specs=[pl.BlockSpec(lam_vecs.shape, lambda b, h, i: (0, 0)),
                  pl.BlockSpec((bq, w), lambda b, h, i: (b * nq + i, q_col // w + h)),
                  pl.BlockSpec((bq, w), lambda b, h, i: (b * nq + jnp.minimum(i + 1, nq - 1), q_col // w + h)),
                  pl.BlockSpec((seq, w), lambda b, h, i: (b, k_col // w + h)),
                  pl.BlockSpec((seq, w), lambda b, h, i: (b, v_col // w + h)),
                  pl.BlockSpec((1, w), lambda b, h, i: (0, 0))],
        out_specs=pl.BlockSpec((bq, w), lambda b, h, i: (b * nq + i, h)),
        out_shape=jax.ShapeDtypeStruct((t, DIFF_HEADS * w), BF16),
        scratch_shapes=[pltpu.VMEM((2 * bq, w), BF16),
                        pltpu.VMEM((w, seq), BF16),
                        pltpu.VMEM((bk, 2 * bq), F32),
                        pltpu.VMEM((bk, 2 * bq), F32),
                        pltpu.VMEM((bk, 2 * bq), F32),
                        pltpu.VMEM((1, 2 * bq), F32),
                        pltpu.VMEM((1, 2 * bq), F32),
                        pltpu.VMEM((w, 2 * bq), F32)],
        compiler_params=pltpu.CompilerParams(
            dimension_semantics=("arbitrary", "arbitrary", "arbitrary"), vmem_limit_bytes=VMEM_LIMIT),
        name="diff_attn",
    )(lam_vecs, proj, proj, proj, proj, norm_w)


ROW_SUB = 8


def _store_row_tiles(ref, val):
    rows = val.shape[0]
    for s in range(ROW_SUB):
        ref[pl.ds(s, rows, stride=ROW_SUB), :] = val[:, s * LANES:(s + 1) * LANES]


def _load_row_tiles(ref, rows):
    return jnp.concatenate([ref[pl.ds(s, rows, stride=ROW_SUB), :] for s in range(ROW_SUB)], axis=1)


def _tile_copy(src_hbm, dst_vmem, sem, src_row, dst_row):
    return pltpu.make_async_copy(src_hbm.at[pl.ds(src_row * ROW_SUB, ROW_SUB), :],
                                 dst_vmem.at[pl.ds(dst_row * ROW_SUB, ROW_SUB), :], sem)


def _wait_tiles(src_hbm, dst_vmem, sem):
    pltpu.make_async_copy(src_hbm.at[pl.ds(0, dst_vmem.shape[0]), :], dst_vmem, sem).wait()


def _outproj_kernel(og_ref, od_ref, ga_ref, gb_ref, x_ref, mod_ref, wo_ref, lnw_ref, lnb_ref, wr_ref, br_ref,
                    x1_ref, u2_ref, rt_ref, cnt_ref, tri_ref, run_ref, *, alpha):
    @pl.when(pl.program_id(0) == 0)
    def _():
        r = lax.broadcasted_iota(jnp.int32, tri_ref.shape, 0)
        c = lax.broadcasted_iota(jnp.int32, tri_ref.shape, 1)
        tri_ref[...] = (c < r).astype(BF16)
        run_ref[...] = jnp.zeros_like(run_ref)

    merged = jax.nn.sigmoid(ga_ref[...]) * og_ref[...] + jax.nn.sigmoid(gb_ref[...]) * od_ref[...]
    y = jnp.dot(merged, wo_ref[...], preferred_element_type=F32)
    gate1 = mod_ref[0, 2:3, :]
    shift2 = mod_ref[0, 3:4, :]
    scale2 = mod_ref[0, 4:5, :]
    x1 = _layer_norm(alpha * x_ref[...] + gate1 * y) * lnw_ref[...] + lnb_ref[...]
    x1_ref[...] = x1
    u2 = _layer_norm(x1) * (1.0 + scale2) + shift2
    _store_row_tiles(u2_ref, u2)

    logits = jnp.dot(u2.astype(BF16), wr_ref[...], preferred_element_type=F32) + br_ref[...]
    lane = lax.broadcasted_iota(jnp.int32, logits.shape, 1)
    neg = jnp.float32(-jnp.inf)
    big = jnp.int32(LANES)
    lg = jnp.where(lane < N_GROUPS, logits, neg)
    g_max = jnp.max(lg, axis=-1, keepdims=True)
    w_grp = 1.0 / jnp.sum(jnp.exp(lg - g_max), axis=-1, keepdims=True)
    g_idx = jnp.min(jnp.where(lg == g_max, lane, big), axis=-1, keepdims=True)
    lo = N_GROUPS + EXPERTS_PER_GROUP * g_idx
    le = jnp.where(jnp.logical_and(lane >= lo, lane < lo + EXPERTS_PER_GROUP), logits, neg)
    v1 = jnp.max(le, axis=-1, keepdims=True)
    i1 = jnp.min(jnp.where(le == v1, lane, big), axis=-1, keepdims=True)
    le2 = jnp.where(lane == i1, neg, le)
    v2 = jnp.max(le2, axis=-1, keepdims=True)
    i2 = jnp.min(jnp.where(le2 == v2, lane, big), axis=-1, keepdims=True)
    e2 = jnp.exp(v2 - v1)
    den = 1.0 + e2
    c1 = w_grp / den
    c2 = w_grp * e2 / den

    sel1 = lane == i1 - N_GROUPS
    sel2 = lane == i2 - N_GROUPS
    onehot = jnp.logical_or(sel1, sel2).astype(BF16)
    before = jnp.dot(tri_ref[...], onehot, preferred_element_type=F32) + run_ref[...]
    pos1 = jnp.sum(jnp.where(sel1, before, 0.0), axis=-1, keepdims=True)
    pos2 = jnp.sum(jnp.where(sel2, before, 0.0), axis=-1, keepdims=True)
    run_ref[...] += jnp.sum(onehot.astype(F32), axis=0, keepdims=True)
    cnt_ref[...] = jnp.broadcast_to(run_ref[...], cnt_ref.shape)

    cols = ((i1 - N_GROUPS).astype(F32), (i2 - N_GROUPS).astype(F32), c1, c2, pos1, pos2)
    rt = jnp.zeros_like(logits)
    for li, col in enumerate(cols):
        rt = jnp.where(lane == li, col, rt)
    rt_ref[...] = rt


def _outproj(o_gla, o_diff, proj, x2, mod, w_out, ln_w, ln_b, w_r, b_r, *, seq, tm, ga_col, alpha):
    t, d = x2.shape
    assert d == ROW_SUB * LANES
    kern = functools.partial(_outproj_kernel, alpha=alpha)
    row_spec = pl.BlockSpec((tm, d), lambda i: (i, 0))
    vec_spec = pl.BlockSpec((1, d), lambda i: (0, 0))
    return pl.pallas_call(
        kern,
        grid=(t // tm,),
        in_specs=[row_spec, row_spec,
                  pl.BlockSpec((tm, d), lambda i: (i, ga_col // d)),
                  pl.BlockSpec((tm, d), lambda i: (i, ga_col // d + 1)),
                  row_spec,
                  pl.BlockSpec((1,) + mod.shape[1:], lambda i: ((i * tm) // seq, 0, 0)),
                  pl.BlockSpec((d, d), lambda i: (0, 0)),
                  vec_spec, vec_spec,
                  pl.BlockSpec((d, LANES), lambda i: (0, 0)),
                  pl.BlockSpec((1, LANES), lambda i: (0, 0))],
        out_specs=[row_spec,
                   pl.BlockSpec((tm * ROW_SUB, LANES), lambda i: (i, 0)),
                   pl.BlockSpec((tm, LANES), lambda i: (i, 0)),
                   pl.BlockSpec((ROW_SUB, LANES), lambda i: (0, 0))],
        out_shape=[jax.ShapeDtypeStruct((t, d), F32),
                   jax.ShapeDtypeStruct((t * ROW_SUB, LANES), F32),
                   jax.ShapeDtypeStruct((t, LANES), F32),
                   jax.ShapeDtypeStruct((ROW_SUB, LANES), F32)],
        scratch_shapes=[pltpu.VMEM((tm, tm), BF16),
                        pltpu.VMEM((1, LANES), F32)],
        compiler_params=pltpu.CompilerParams(
            dimension_semantics=("arbitrary",), vmem_limit_bytes=VMEM_LIMIT),
        name="outproj",
    )(o_gla, o_diff, proj, proj, x2, mod, w_out, ln_w, ln_b, w_r, b_r)


def _dispatch_kernel(fill_ref, d0_ref, d1_ref, u_ref, xs_hbm, zbuf, sem_fill, sem, *, tm, blk):
    i = pl.program_id(0)

    @pl.when(i == 0)
    def _():
        zbuf[...] = jnp.zeros_like(zbuf)
        n_fill = fill_ref.shape[0]

        def fill_copy(j):
            return pltpu.make_async_copy(zbuf, xs_hbm.at[pl.ds(fill_ref[j] * ROW_SUB, blk * ROW_SUB), :], sem_fill)

        def start_fill(j, carry):
            @pl.when(fill_ref[j] >= 0)
            def _():
                fill_copy(j).start()
            return carry

        def wait_fill(j, carry):
            @pl.when(fill_ref[j] >= 0)
            def _():
                fill_copy(j).wait()
            return carry

        lax.fori_loop(0, n_fill, start_fill, 0)
        lax.fori_loop(0, n_fill, wait_fill, 0)

    def row_copy(dest_ref, r):
        return pltpu.make_async_copy(u_ref.at[pl.ds(r * ROW_SUB, ROW_SUB), :],
                                     xs_hbm.at[pl.ds(dest_ref[0, 0, r] * ROW_SUB, ROW_SUB), :], sem)

    def body(r, carry):
        row_copy(d0_ref, r).start(priority=0)
        row_copy(d1_ref, r).start(priority=1)
        return carry

    lax.fori_loop(0, tm, body, 0, unroll=4)
    for _ in range(2):
        pltpu.make_async_copy(u_ref, xs_hbm.at[pl.ds(0, tm * ROW_SUB), :], sem).wait()


def _dispatch(fill_start, dest0, dest1, u2_tiles, *, p_rows, tm, blk):
    t = u2_tiles.shape[0] // ROW_SUB
    kern = functools.partial(_dispatch_kernel, tm=tm, blk=blk)
    idx_spec = pl.BlockSpec((1, 1, tm), lambda i, fs: (i, 0, 0), memory_space=pltpu.SMEM)
    grid_spec = pltpu.PrefetchScalarGridSpec(
        num_scalar_prefetch=1,
        grid=(t // tm,),
        in_specs=[idx_spec, idx_spec, pl.BlockSpec((tm * ROW_SUB, LANES), lambda i, fs: (i, 0))],
        out_specs=pl.BlockSpec(memory_space=pl.ANY),
        scratch_shapes=[pltpu.VMEM((blk * ROW_SUB, LANES), F32),
                        pltpu.SemaphoreType.DMA(()),
                        pltpu.SemaphoreType.DMA(())],
    )
    return pl.pallas_call(
        kern,
        grid_spec=grid_spec,
        out_shape=jax.ShapeDtypeStruct((p_rows * ROW_SUB, LANES), F32),
        compiler_params=pltpu.CompilerParams(dimension_semantics=("arbitrary",)),
        name="dispatch",
    )(fill_start, dest0, dest1, u2_tiles)


def _expert_kernel(be_ref, na_ref, x_ref, wg_ref, wu_ref, wd_ref, o_ref, wgb, wub, wdb, *, blk):
    i = pl.program_id(0)
    active = i < na_ref[0]

    @pl.when(active)
    def _():
        changed = jnp.logical_or(i == 0, be_ref[i] != be_ref[jnp.maximum(i - 1, 0)])

        @pl.when(changed)
        def _():
            wgb[...] = wg_ref[0].astype(BF16)
            wub[...] = wu_ref[0].astype(BF16)
            wdb[...] = wd_ref[0].astype(BF16)

        xb = _load_row_tiles(x_ref, blk).astype(BF16)
        hg = jnp.dot(xb, wgb[...], preferred_element_type=F32)
        hu = jnp.dot(xb, wub[...], preferred_element_type=F32)
        h = (_silu(hg) * hu).astype(BF16)
        _store_row_tiles(o_ref, jnp.dot(h, wdb[...], preferred_element_type=F32))

    @pl.when(jnp.logical_not(active))
    def _():
        o_ref[...] = jnp.zeros_like(o_ref)


def _experts(block_expert, n_active, xs_tiles, w_gate, w_up, w_down, *, blk):
    nblk = block_expert.shape[0]
    e, d, ff = w_gate.shape
    kern = functools.partial(_expert_kernel, blk=blk)
    grid_spec = pltpu.PrefetchScalarGridSpec(
        num_scalar_prefetch=2,
        grid=(nblk,),
        in_specs=[pl.BlockSpec((blk * ROW_SUB, LANES), lambda i, be, na: (jnp.minimum(i, na[0] - 1), 0)),
                  pl.BlockSpec((1, d, ff), lambda i, be, na: (be[i], 0, 0)),
                  pl.BlockSpec((1, d, ff), lambda i, be, na: (be[i], 0, 0)),
                  pl.BlockSpec((1, ff, d), lambda i, be, na: (be[i], 0, 0))],
        out_specs=pl.BlockSpec((blk * ROW_SUB, LANES), lambda i, be, na: (i, 0)),
        scratch_shapes=[pltpu.VMEM((d, ff), BF16),
                        pltpu.VMEM((d, ff), BF16),
                        pltpu.VMEM((ff, d), BF16)],
    )
    return pl.pallas_call(
        kern,
        grid_spec=grid_spec,
        out_shape=jax.ShapeDtypeStruct((nblk * blk * ROW_SUB, LANES), F32),
        compiler_params=pltpu.CompilerParams(
            dimension_semantics=("arbitrary",), vmem_limit_bytes=VMEM_LIMIT),
        name="experts",
    )(block_expert, n_active, xs_tiles, w_gate, w_up, w_down)


def _combine_kernel(d0_ref, d0n_ref, d1_ref, d1n_ref, rows_hbm, x1_ref, rt_ref, mod_ref, lnw_ref, lnb_ref,
                    o_ref, buf, sem, *, tm, alpha):
    i = pl.program_id(0)
    n = pl.num_programs(0)
    slot = i % 2

    def start_gather(i0_ref, i1_ref, s):
        def body(r, carry):
            _tile_copy(rows_hbm, buf.at[s, 0], sem.at[s], i0_ref[0, 0, r], r).start(priority=0)
            _tile_copy(rows_hbm, buf.at[s, 1], sem.at[s], i1_ref[0, 0, r], r).start(priority=1)
            return carry
        lax.fori_loop(0, tm, body, 0, unroll=4)

    def wait_gather(s):
        _wait_tiles(rows_hbm, buf.at[s, 0], sem.at[s])
        _wait_tiles(rows_hbm, buf.at[s, 1], sem.at[s])

    @pl.when(i == 0)
    def _():
        start_gather(d0_ref, d1_ref, 0)

    start_gather(d0n_ref, d1n_ref, 1 - slot)
    wait_gather(slot)
    rt = rt_ref[...]
    y = rt[:, 2:3] * _load_row_tiles(buf.at[slot, 0], tm) + rt[:, 3:4] * _load_row_tiles(buf.at[slot, 1], tm)
    gate2 = mod_ref[0, 5:6, :]
    z = alpha * x1_ref[...] + gate2 * y
    o_ref[...] = _layer_norm(z) * lnw_ref[...] + lnb_ref[...]

    @pl.when(i == n - 1)
    def _():
        wait_gather(1 - slot)


def _combine(dest0, dest1, rows_tiles, x1, route, mod, ln_w, ln_b, *, seq, tm, alpha):
    t, d = x1.shape
    nt = t // tm
    kern = functools.partial(_combine_kernel, tm=tm, alpha=alpha)
    cur = lambda i: (i, 0, 0)
    nxt = lambda i: (jnp.minimum(i + 1, nt - 1), 0, 0)
    idx_spec = lambda m: pl.BlockSpec((1, 1, tm), m, memory_space=pltpu.SMEM)
    row_spec = pl.BlockSpec((tm, d), lambda i: (i, 0))
    vec_spec = pl.BlockSpec((1, d), lambda i: (0, 0))
    return pl.pallas_call(
        kern,
        grid=(nt,),
        in_specs=[idx_spec(cur), idx_spec(nxt), idx_spec(cur), idx_spec(nxt),
                  pl.BlockSpec(memory_space=pl.ANY),
                  row_spec,
                  pl.BlockSpec((tm, LANES), lambda i: (i, 0)),
                  pl.BlockSpec((1,) + mod.shape[1:], lambda i: ((i * tm) // seq, 0, 0)),
                  vec_spec, vec_spec],
        out_specs=row_spec,
        out_shape=jax.ShapeDtypeStruct((t, d), F32),
        scratch_shapes=[pltpu.VMEM((2, 2, tm * ROW_SUB, LANES), F32),
                        pltpu.SemaphoreType.DMA((2,))],
        compiler_params=pltpu.CompilerParams(
            dimension_semantics=("arbitrary",), vmem_limit_bytes=VMEM_LIMIT),
        name="combine",
    )(dest0, dest0, dest1, dest1, rows_tiles, x1, route, mod, ln_w, ln_b)


def _dispatch_plan(route, counts, blk):
    t = route.shape[0]
    eid = route[:, :2].astype(jnp.int32)
    pos = route[:, 4:6].astype(jnp.int32)
    counts = counts[:N_EXPERTS].astype(jnp.int32)
    pcounts = ((counts + blk - 1) // blk) * blk
    pend = jnp.cumsum(pcounts)
    pstart = pend - pcounts
    experts = jnp.arange(N_EXPERTS, dtype=jnp.int32)
    dest = jnp.sum(jnp.where(eid[:, :, None] == experts, pstart, 0), axis=-1) + pos
    p_rows = ((2 * t + N_EXPERTS * (blk - 1) + blk - 1) // blk) * blk
    nblk = p_rows // blk
    block_start = jnp.arange(nblk, dtype=jnp.int32) * blk
    block_expert = jnp.minimum(jnp.sum((pend[None, :] <= block_start[:, None]).astype(jnp.int32), axis=1),
                               N_EXPERTS - 1)
    n_active = (pend[N_EXPERTS - 1:] // blk).astype(jnp.int32)
    tail = pend[N_EXPERTS - 1] + jnp.arange(N_EXPERTS, dtype=jnp.int32) * blk
    fill_start = jnp.concatenate([jnp.where(pcounts > 0, pend - blk, -1), jnp.where(tail < p_rows, tail, -1)])
    return dest[:, 0], dest[:, 1], fill_start, block_expert, n_active, p_rows


def _layer(x, c, positions, w_ada, b_ada, w_in, w_g2, b_g2, gla_nw, lq1, lk1, lq2, lk2, diff_nw, w_out,
           ln1_w, ln1_b, w_rg, b_rg, w_re, b_re, w_eg, w_eu, w_ed, ln2_w, ln2_b, *, lam_init,
           tm_in, tn_in, gla_rows, bq, bk, tm_out, moe_blk, tm_dsp, tm_cmb):
    bsz, seq, d = x.shape
    t = bsz * seq
    alpha = (2.0 * DEPTH) ** 0.25
    gla_dk = d // (2 * GLA_HEADS)
    gla_dv = d // GLA_HEADS
    dh = d // (2 * DIFF_HEADS)
    hk = GLA_HEADS * gla_dk
    hv = GLA_HEADS * gla_dv
    dq = DIFF_HEADS * 2 * dh
    gr_col = 2 * hk + 2 * hv
    q_col, k_col = 0, hk
    v_col, g_col = 2 * hk, 2 * hk + hv
    dq_col = gr_col
    dk_col = dq_col + dq
    dv_col = dk_col + dq
    mg_col = dv_col + dq
    w_main = jnp.concatenate([w_in[:, :gr_col], w_in[:, gr_col + GLA_GATE_RANK:]], axis=1).astype(BF16)
    half = dh // 2
    lane = np.arange(2 * dh)
    in_head = (lane // half % 2) * dh + (lane // dh) * half + lane % half
    cols = (np.arange(DIFF_HEADS)[:, None] * 2 * dh + in_head[None, :]).reshape(-1)
    w_main = jnp.concatenate([w_main[:, :dq_col], w_main[:, dq_col + cols], w_main[:, dk_col + cols],
                              w_main[:, dv_col:]], axis=1)
    w_gr = jnp.pad(w_in[:, gr_col:gr_col + GLA_GATE_RANK], ((0, 0), (0, LANES - GLA_GATE_RANK))).astype(BF16)
    w_g2p = jnp.pad(w_g2, ((0, LANES - GLA_GATE_RANK), (0, 0))).astype(BF16)
    x2 = x.reshape(t, d)

    ada = _ada(c, w_ada, b_ada.reshape(1, -1))
    mod = ada.reshape(bsz, 6, d)

    inv = ROPE_THETA ** (-jnp.arange(half, dtype=F32) / half)
    inv_row = jnp.tile(inv, LANES // half).reshape(1, LANES)
    cos, sin = _rope_tables(positions.reshape(t // ROPE_PACK, ROPE_PACK), inv_row, tm_in)

    proj, log_g = _inproj(x2, mod, cos, sin, w_main, w_gr, w_g2p, b_g2.reshape(1, -1), seq=seq, tm=tm_in, tn=tn_in,
                          q_col=dq_col, k_col=dk_col, rope_cols=dq, q_scale=dh ** -0.5 * LOG2_E)

    o_gla = _gla(proj, log_g, gla_nw.reshape(1, -1), bsz=bsz, seq=seq, rows=gla_rows, dk=gla_dk, dv=gla_dv,
                 q_col=q_col, k_col=k_col, v_col=v_col, g_col=g_col)

    lam_vecs = jnp.pad(jnp.stack([lq1, lk1, lq2, lk2]), ((0, 4), (0, LANES - dh)))
    o_diff = _diff(lam_vecs, proj, diff_nw.reshape(1, -1), bsz=bsz, seq=seq, bq=bq, bk=bk, dh=dh,
                   q_col=dq_col, k_col=dk_col, v_col=dv_col, lam_init=lam_init)

    w_r = jnp.pad(jnp.concatenate([w_rg, w_re], axis=1), ((0, 0), (0, LANES - N_GROUPS - N_EXPERTS))).astype(BF16)
    b_r = jnp.pad(jnp.concatenate([b_rg, b_re]), (0, LANES - N_GROUPS - N_EXPERTS)).reshape(1, LANES)
    x1, u2_tiles, route, counts = _outproj(o_gla, o_diff, proj, x2, mod, w_out.astype(BF16), ln1_w.reshape(1, -1),
                                           ln1_b.reshape(1, -1), w_r, b_r, seq=seq, tm=tm_out, ga_col=mg_col,
                                           alpha=alpha)

    dest0, dest1, fill_start, block_expert, n_active, p_rows = _dispatch_plan(route, counts[0], moe_blk)
    xs_tiles = _dispatch(fill_start, dest0.reshape(-1, 1, tm_dsp), dest1.reshape(-1, 1, tm_dsp), u2_tiles,
                         p_rows=p_rows, tm=tm_dsp, blk=moe_blk)
    rows = _experts(block_expert, n_active, xs_tiles, w_eg, w_eu, w_ed, blk=moe_blk)
    out = _combine(dest0.reshape(-1, 1, tm_cmb), dest1.reshape(-1, 1, tm_cmb), rows, x1, route, mod,
                   ln2_w.reshape(1, -1), ln2_b.reshape(1, -1), seq=seq, tm=tm_cmb, alpha=alpha)
    return out.reshape(bsz, seq, d)


def kernel(x, c, positions, w_ada, b_ada, w_in, w_gla_gate2, b_gla_gate2, gla_norm_w, diff_lambda_q1,
           diff_lambda_k1, diff_lambda_q2, diff_lambda_k2, diff_norm_w, w_out, ln1_w, ln1_b, w_router_group,
           b_router_group, w_router_expert, b_router_expert, w_exp_gate, w_exp_up, w_exp_down, ln2_w, ln2_b):
    assert w_ada.shape[0] == DEPTH
    for l in range(DEPTH):
        lam_init = 0.8 - 0.6 * math.exp(-0.3 * l)
        x = _layer(x, c, positions, w_ada[l], b_ada[l], w_in[l], w_gla_gate2[l], b_gla_gate2[l], gla_norm_w[l],
                   diff_lambda_q1[l], diff_lambda_k1[l], diff_lambda_q2[l], diff_lambda_k2[l], diff_norm_w[l],
                   w_out[l], ln1_w[l], ln1_b[l], w_router_group[l], b_router_group[l], w_router_expert[l],
                   b_router_expert[l], w_exp_gate[l], w_exp_up[l], w_exp_down[l], ln2_w[l], ln2_b[l],
                   lam_init=lam_init, tm_in=1024, tn_in=2048, gla_rows=2048, bq=512, bk=512, tm_out=512,
                   moe_blk=256, tm_dsp=1024, tm_cmb=512)
    return x
```

```python
import functools
import math

import jax
import jax.numpy as jnp
import numpy as np
from jax import lax
from jax.experimental import pallas as pl
from jax.experimental.pallas import tpu as pltpu

F32 = jnp.float32
BF16 = jnp.bfloat16
HIGHEST = lax.Precision.HIGHEST

DEPTH = 1
GLA_HEADS = 4
GLA_GATE_RANK = 16
GLA_TAU = 16.0
GLA_CHUNK = 64
GLA_SUB_ROWS = 256
DIFF_HEADS = 8
ROPE_THETA = 10000.0
N_GROUPS = 4
EXPERTS_PER_GROUP = 8
N_EXPERTS = N_GROUPS * EXPERTS_PER_GROUP
LN_EPS = 1e-5
RMS_EPS = 1e-6
LOG2_E = math.log2(math.e)
LANES = 128
VMEM_LIMIT = 56 * 1024 * 1024

NT_DIMS = (((1,), (1,)), ((), ()))
TN_DIMS = (((0,), (0,)), ((), ()))


def _layer_norm(x):
    mu = jnp.mean(x, axis=-1, keepdims=True)
    xc = x - mu
    var = jnp.mean(xc * xc, axis=-1, keepdims=True)
    return xc * lax.rsqrt(var + LN_EPS)


def _silu(x):
    return x * jax.nn.sigmoid(x)


def _ada_kernel(c_ref, w_ref, b_ref, o_ref):
    s = _silu(c_ref[...])
    o_ref[...] = jnp.dot(s, w_ref[...], preferred_element_type=F32, precision=HIGHEST) + b_ref[...]


def _ada(c, w, b):
    bsz, d = c.shape
    n = w.shape[1]
    tn = d
    return pl.pallas_call(
        _ada_kernel,
        grid=(n // tn,),
        in_specs=[pl.BlockSpec((bsz, d), lambda j: (0, 0)),
                  pl.BlockSpec((d, tn), lambda j: (0, j)),
                  pl.BlockSpec((1, tn), lambda j: (0, j))],
        out_specs=pl.BlockSpec((bsz, tn), lambda j: (0, j)),
        out_shape=jax.ShapeDtypeStruct((bsz, n), F32),
        name="ada",
    )(c, w, b)


def _rope_kernel(pos_ref, inv_ref, cs_ref):
    ang = pos_ref[...].astype(F32) * inv_ref[...]
    lane = lax.broadcasted_iota(jnp.int32, ang.shape, 1)
    sin = jnp.sin(ang)
    cs_ref[:, :LANES] = jnp.cos(ang)
    cs_ref[:, LANES:] = jnp.where(lane < LANES // 2, -sin, sin)


def _rope_tables(pos_col, inv_row, tm):
    t = pos_col.shape[0]
    return pl.pallas_call(
        _rope_kernel,
        grid=(t // tm,),
        in_specs=[pl.BlockSpec((tm, 1), lambda i: (i, 0)),
                  pl.BlockSpec((1, LANES), lambda i: (0, 0))],
        out_specs=pl.BlockSpec((tm, 2 * LANES), lambda i: (i, 0)),
        out_shape=jax.ShapeDtypeStruct((t, 2 * LANES), F32),
        name="rope_tables",
    )(pos_col, inv_row)


def _inproj_kernel(x_ref, mod_ref, cs_ref, w_ref, wgr_ref, wg2_ref, bg2_ref, o_ref, lg_ref, u_ref,
                   *, rope_slabs, q_scale):
    j = pl.program_id(1)

    @pl.when(j == 0)
    def _():
        shift = mod_ref[0, 0:1, :]
        scale = mod_ref[0, 1:2, :]
        u = (_layer_norm(x_ref[...]) * (1.0 + scale) + shift).astype(BF16)
        u_ref[...] = u
        gr = jnp.dot(u, wgr_ref[...], preferred_element_type=F32)
        z = jnp.dot(gr.astype(BF16), wg2_ref[...], preferred_element_type=F32) + bg2_ref[...]
        log_sig = jnp.minimum(z, 0.0) - jnp.log(1.0 + jnp.exp(-jnp.abs(z)))
        lg_ref[...] = log_sig * (1.0 / GLA_TAU)

    acc = jnp.dot(u_ref[...], w_ref[...], preferred_element_type=F32)
    plain = j >= 0
    for tile, slabs in rope_slabs.items():
        plain = jnp.logical_and(plain, j != tile)

        @pl.when(j == tile)
        def _(slabs=slabs):
            cos = {False: cs_ref[:, :LANES]}
            sin = {False: cs_ref[:, LANES:]}
            if any(slabs.values()):
                cos[True] = cos[False] * q_scale
                sin[True] = sin[False] * q_scale
            for h in range(acc.shape[1] // LANES):
                t = acc[:, h * LANES:(h + 1) * LANES]
                if h in slabs:
                    partner = pltpu.roll(t, LANES // 2, 1)
                    t = t * cos[slabs[h]] + partner * sin[slabs[h]]
                o_ref[:, h * LANES:(h + 1) * LANES] = t.astype(o_ref.dtype)

    @pl.when(plain)
    def _():
        o_ref[...] = acc.astype(o_ref.dtype)


def _inproj(x2, mod, cs, w_main, w_gr, w_g2, b_g2, *, seq, tm, tn, q_col, k_col, rope_cols, q_scale):
    t, d = x2.shape
    n = w_main.shape[1]
    ng = w_g2.shape[1]
    rope_slabs = {}
    for col in range(0, n, LANES):
        for start, is_q in ((q_col, True), (k_col, False)):
            if start <= col < start + rope_cols:
                rope_slabs.setdefault(col // tn, {})[(col % tn) // LANES] = is_q
    kern = functools.partial(_inproj_kernel, rope_slabs=rope_slabs, q_scale=q_scale)
    return pl.pallas_call(
        kern,
        grid=(t // tm, n // tn),
        in_specs=[pl.BlockSpec((tm, d), lambda i, j: (i, 0)),
                  pl.BlockSpec((1,) + mod.shape[1:], lambda i, j: ((i * tm) // seq, 0, 0)),
                  pl.BlockSpec((tm, 2 * LANES), lambda i, j: (i, 0)),
                  pl.BlockSpec((d, tn), lambda i, j: (0, j)),
                  pl.BlockSpec((d, LANES), lambda i, j: (0, 0)),
                  pl.BlockSpec((LANES, ng), lambda i, j: (0, 0)),
                  pl.BlockSpec((1, ng), lambda i, j: (0, 0))],
        out_specs=[pl.BlockSpec((tm, tn), lambda i, j: (i, j)),
                   pl.BlockSpec((tm, ng), lambda i, j: (i, 0))],
        out_shape=[jax.ShapeDtypeStruct((t, n), BF16),
                   jax.ShapeDtypeStruct((t, ng), F32)],
        scratch_shapes=[pltpu.VMEM((tm, d), BF16)],
        compiler_params=pltpu.CompilerParams(
            dimension_semantics=("arbitrary", "arbitrary"), vmem_limit_bytes=VMEM_LIMIT),
        name="inproj",
    )(x2, mod, cs, w_main, w_gr, w_g2, b_g2)


def _gla_kernel(q_ref, k_ref, v_ref, g_ref, lg_ref, nw_ref, o_ref, st_ref, *, chunk, sub_rows, q_scale):
    @pl.when(pl.program_id(2) == 0)
    def _():
        st_ref[...] = jnp.zeros_like(st_ref)

    n_chunks = sub_rows // chunk
    pos = lax.broadcasted_iota(jnp.int32, (sub_rows, lg_ref.shape[1]), 0) % chunk
    r = lax.broadcasted_iota(jnp.int32, (sub_rows, sub_rows), 0)
    c = lax.broadcasted_iota(jnp.int32, (sub_rows, sub_rows), 1)
    keep = jnp.logical_and(c <= r, (r // chunk) == (c // chunk))
    st = st_ref[...]

    for si in range(q_ref.shape[0] // sub_rows):
        rs = slice(si * sub_rows, (si + 1) * sub_rows)

        b = lg_ref[rs, :]
        step = 1
        while step < chunk:
            b = b + jnp.where(pos >= step, pltpu.roll(b, step, 0), 0.0)
            step *= 2

        k = k_ref[rs, :].astype(F32)
        q_t = (q_ref[rs, :].astype(F32) * q_scale * jnp.exp(b)).astype(BF16)
        k_t = (k * jnp.exp(-b)).astype(BF16)
        v = v_ref[rs, :]

        attn = lax.dot_general(q_t, k_t, NT_DIMS, preferred_element_type=F32)
        o = jnp.dot(jnp.where(keep, attn, 0.0).astype(BF16), v, preferred_element_type=F32)

        o_inter = []
        for ci in range(n_chunks):
            sl = slice(ci * chunk, (ci + 1) * chunk)
            b_c = b[sl, :]
            b_last = b_c[chunk - 1:chunk, :]
            k_d = (k[sl, :] * jnp.exp(b_last - b_c)).astype(BF16)
            kv = lax.dot_general(v[sl, :], k_d, TN_DIMS, preferred_element_type=F32)
            o_inter.append(lax.dot_general(q_t[sl, :], st.astype(BF16), NT_DIMS, preferred_element_type=F32))
            st = st * jnp.exp(b_last) + kv
        o = o + jnp.concatenate(o_inter, axis=0)

        ms = jnp.mean(o * o, axis=-1, keepdims=True)
        y = o * lax.rsqrt(ms + RMS_EPS) * nw_ref[...] * _silu(g_ref[rs, :].astype(F32))
        o_ref[rs, :] = y.astype(o_ref.dtype)

    st_ref[...] = st


def _gla(proj, log_g, norm_w, *, bsz, seq, rows, dk, dv, q_col, k_col, v_col, g_col):
    t = proj.shape[0]
    nl = seq // rows
    kern = functools.partial(_gla_kernel, chunk=GLA_CHUNK, sub_rows=min(rows, GLA_SUB_ROWS), q_scale=dk ** -0.5)
    row = lambda b, h, l: b * nl + l
    return pl.pallas_call(
        kern,
        grid=(bsz, GLA_HEADS, nl),
        in_specs=[pl.BlockSpec((rows, dk), lambda b, h, l: (row(b, h, l), q_col // dk + h)),
                  pl.BlockSpec((rows, dk), lambda b, h, l: (row(b, h, l), k_col // dk + h)),
                  pl.BlockSpec((rows, dv), lambda b, h, l: (row(b, h, l), v_col // dv + h)),
                  pl.BlockSpec((rows, dv), lambda b, h, l: (row(b, h, l), g_col // dv + h)),
                  pl.BlockSpec((rows, dk), lambda b, h, l: (row(b, h, l), h)),
                  pl.BlockSpec((1, dv), lambda b, h, l: (0, 0))],
        out_specs=pl.BlockSpec((rows, dv), lambda b, h, l: (row(b, h, l), h)),
        out_shape=jax.ShapeDtypeStruct((t, GLA_HEADS * dv), BF16),
        scratch_shapes=[pltpu.VMEM((dv, dk), F32)],
        compiler_params=pltpu.CompilerParams(
            dimension_semantics=("arbitrary", "arbitrary", "arbitrary"), vmem_limit_bytes=VMEM_LIMIT),
        name="gla",
    )(proj, proj, proj, proj, log_g, norm_w)


def _diff_kernel(lam_ref, q_ref, qn_ref, k_ref, v_ref, nw_ref, o_ref,
                 qs_ref, vt_ref, s0_ref, sa_ref, sb_ref, m_ref, l_ref, acc_ref, *, bq, bk, dh, lam_init):
    qi = pl.program_id(2)
    q0 = qi * bq
    seq = k_ref.shape[0]

    def mask_queries(q):
        lane = lax.broadcasted_iota(jnp.int32, q.shape, 1)
        zero = jnp.zeros_like(q)
        map0 = (lane % dh) < dh // 2
        qs_ref[:bq, :] = jnp.where(map0, q, zero)
        qs_ref[bq:, :] = jnp.where(map0, zero, q)

    def scores(kj, dst):
        k0 = pl.multiple_of(kj * bk, bk)
        dst[...] = lax.dot_general(k_ref[pl.ds(k0, bk), :], qs_ref[...], NT_DIMS, preferred_element_type=F32)

    def update(src, kj, masked):
        k0 = pl.multiple_of(kj * bk, bk)
        s = src[...]
        if masked:
            key = k0 + lax.broadcasted_iota(jnp.int32, s.shape, 0)
            col = lax.broadcasted_iota(jnp.int32, s.shape, 1)
            qpos = q0 + jnp.where(col >= bq, col - bq, col)
            s = jnp.where(key <= qpos, s, -jnp.inf)
        m_prev = m_ref[...]
        m_new = jnp.maximum(m_prev, jnp.max(s, axis=0, keepdims=True))
        alpha = jnp.exp2(m_prev - m_new)
        p = jnp.exp2(s - m_new)
        l_ref[...] = alpha * l_ref[...] + jnp.sum(p, axis=0, keepdims=True)
        acc_ref[...] = alpha * acc_ref[...] + jnp.dot(
            vt_ref[:, pl.ds(k0, bk)], p.astype(BF16), preferred_element_type=F32)
        m_ref[...] = m_new

    @pl.when(qi == 0)
    def _():
        for c in range(seq // bk):
            vt_ref[:, c * bk:(c + 1) * bk] = v_ref[c * bk:(c + 1) * bk, :].astype(F32).T.astype(BF16)
        mask_queries(q_ref[...])
        scores(0, s0_ref)

    m_ref[...] = jnp.full_like(m_ref, -jnp.inf)
    l_ref[...] = jnp.zeros_like(l_ref)
    acc_ref[...] = jnp.zeros_like(acc_ref)

    n_full = (q0 + 1) // bk

    @pl.when(n_full == 0)
    def _():
        update(s0_ref, 0, True)

    @pl.when(n_full > 0)
    def _():
        scores(1, sa_ref)
        update(s0_ref, 0, False)
        rest = n_full - 1

        def pair_body(jj, carry):
            j = 1 + 2 * jj
            scores(j + 1, sb_ref)
            update(sa_ref, j, False)
            scores(j + 2, sa_ref)
            update(sb_ref, j + 1, False)
            return carry

        lax.fori_loop(0, rest // 2, pair_body, 0)
        odd = (rest % 2) == 1

        @pl.when(odd)
        def _():
            scores(n_full, sb_ref)
            update(sa_ref, n_full - 1, False)
            update(sb_ref, n_full, True)

        @pl.when(jnp.logical_not(odd))
        def _():
            update(sa_ref, n_full, True)

    mask_queries(qn_ref[...])
    scores(0, s0_ref)

    lv = lam_ref[...]
    s1 = jnp.sum(lv[0:1, :] * lv[1:2, :], axis=-1, keepdims=True)
    s2 = jnp.sum(lv[2:3, :] * lv[3:4, :], axis=-1, keepdims=True)
    lam = jnp.exp(s1) - jnp.exp(s2) + lam_init
    o_all = acc_ref[...] / l_ref[...]
    o_t = o_all[:, :bq] - lam * o_all[:, bq:]
    ms = jnp.mean(o_t * o_t, axis=0, keepdims=True)
    y = (o_t * lax.rsqrt(ms + RMS_EPS)).T * nw_ref[...] * (1.0 - lam_init)
    o_ref[...] = y.astype(o_ref.dtype)


def _diff(lam_vecs, proj, norm_w, *, bsz, seq, bq, bk, dh, q_col, k_col, v_col, lam_init):
    assert bk % bq == 0 and seq % bk == 0
    t = proj.shape[0]
    w = 2 * dh
    nq = seq // bq
    kern = functools.partial(_diff_kernel, bq=bq, bk=bk, dh=dh, lam_init=lam_init)
    return pl.pallas_call(
        kern,
        grid=(bsz, DIFF_HEADS, nq),
        in_specs=[pl.BlockSpec(lam_vecs.shape, lambda b, h, i: (0, 0)),
                  pl.BlockSpec((bq, w), lambda b, h, i: (b * nq + i, q_col // w + h)),
                  pl.BlockSpec((bq, w), lambda b, h, i: (b * nq + jnp.minimum(i + 1, nq - 1), q_col // w + h)),
                  pl.BlockSpec((seq, w), lambda b, h, i: (b, k_col // w + h)),
                  pl.BlockSpec((seq, w), lambda b, h, i: (b, v_col // w + h)),
                  pl.BlockSpec((1, w), lambda b, h, i: (0, 0))],
        out_specs=pl.BlockSpec((bq, w), lambda b, h, i: (b * nq + i, h)),
        out_shape=jax.ShapeDtypeStruct((t, DIFF_HEADS * w), BF16),
        scratch_shapes=[pltpu.VMEM((2 * bq, w), BF16),
                        pltpu.VMEM((w, seq), BF16),
                        pltpu.VMEM((bk, 2 * bq), F32),
                        pltpu.VMEM((bk, 2 * bq), F32),
                        pltpu.VMEM((bk, 2 * bq), F32),
                        pltpu.VMEM((1, 2 * bq), F32),
                        pltpu.VMEM((1, 2 * bq), F32),
                        pltpu.VMEM((w, 2 * bq), F32)],
        compiler_params=pltpu.CompilerParams(
            dimension_semantics=("arbitrary", "arbitrary", "arbitrary"), vmem_limit_bytes=VMEM_LIMIT),
        name="diff_attn",
    )(lam_vecs, proj, proj, proj, proj, norm_w)


ROW_SUB = 8


def _store_row_tiles(ref, val):
    rows = val.shape[0]
    for s in range(ROW_SUB):
        ref[pl.ds(s, rows, stride=ROW_SUB), :] = val[:, s * LANES:(s + 1) * LANES]


def _load_row_tiles(ref, rows):
    return jnp.concatenate([ref[pl.ds(s, rows, stride=ROW_SUB), :] for s in range(ROW_SUB)], axis=1)


def _tile_copy(src_hbm, dst_vmem, sem, src_row, dst_row):
    return pltpu.make_async_copy(src_hbm.at[pl.ds(src_row * ROW_SUB, ROW_SUB), :],
                                 dst_vmem.at[pl.ds(dst_row * ROW_SUB, ROW_SUB), :], sem)


def _wait_tiles(src_hbm, dst_vmem, sem):
    pltpu.make_async_copy(src_hbm.at[pl.ds(0, dst_vmem.shape[0]), :], dst_vmem, sem).wait()


def _outproj_kernel(og_ref, od_ref, ga_ref, gb_ref, x_ref, mod_ref, wo_ref, lnw_ref, lnb_ref, wr_ref, br_ref,
                    x1_ref, u2_ref, rt_ref, cnt_ref, tri_ref, run_ref, *, alpha):
    @pl.when(pl.program_id(0) == 0)
    def _():
        r = lax.broadcasted_iota(jnp.int32, tri_ref.shape, 0)
        c = lax.broadcasted_iota(jnp.int32, tri_ref.shape, 1)
        tri_ref[...] = (c < r).astype(BF16)
        run_ref[...] = jnp.zeros_like(run_ref)

    merged = jax.nn.sigmoid(ga_ref[...]) * og_ref[...] + jax.nn.sigmoid(gb_ref[...]) * od_ref[...]
    y = jnp.dot(merged, wo_ref[...], preferred_element_type=F32)
    gate1 = mod_ref[0, 2:3, :]
    shift2 = mod_ref[0, 3:4, :]
    scale2 = mod_ref[0, 4:5, :]
    x1 = _layer_norm(alpha * x_ref[...] + gate1 * y) * lnw_ref[...] + lnb_ref[...]
    x1_ref[...] = x1
    u2 = _layer_norm(x1) * (1.0 + scale2) + shift2
    _store_row_tiles(u2_ref, u2)

    logits = jnp.dot(u2.astype(BF16), wr_ref[...], preferred_element_type=F32) + br_ref[...]
    lane = lax.broadcasted_iota(jnp.int32, logits.shape, 1)
    neg = jnp.float32(-jnp.inf)
    big = jnp.int32(LANES)
    lg = jnp.where(lane < N_GROUPS, logits, neg)
    g_max = jnp.max(lg, axis=-1, keepdims=True)
    w_grp = 1.0 / jnp.sum(jnp.exp(lg - g_max), axis=-1, keepdims=True)
    g_idx = jnp.min(jnp.where(lg == g_max, lane, big), axis=-1, keepdims=True)
    lo = N_GROUPS + EXPERTS_PER_GROUP * g_idx
    le = jnp.where(jnp.logical_and(lane >= lo, lane < lo + EXPERTS_PER_GROUP), logits, neg)
    v1 = jnp.max(le, axis=-1, keepdims=True)
    i1 = jnp.min(jnp.where(le == v1, lane, big), axis=-1, keepdims=True)
    le2 = jnp.where(lane == i1, neg, le)
    v2 = jnp.max(le2, axis=-1, keepdims=True)
    i2 = jnp.min(jnp.where(le2 == v2, lane, big), axis=-1, keepdims=True)
    e2 = jnp.exp(v2 - v1)
    den = 1.0 + e2
    c1 = w_grp / den
    c2 = w_grp * e2 / den

    sel1 = lane == i1 - N_GROUPS
    sel2 = lane == i2 - N_GROUPS
    onehot = jnp.logical_or(sel1, sel2).astype(BF16)
    before = jnp.dot(tri_ref[...], onehot, preferred_element_type=F32) + run_ref[...]
    pos1 = jnp.sum(jnp.where(sel1, before, 0.0), axis=-1, keepdims=True)
    pos2 = jnp.sum(jnp.where(sel2, before, 0.0), axis=-1, keepdims=True)
    run_ref[...] += jnp.sum(onehot.astype(F32), axis=0, keepdims=True)
    cnt_ref[...] = jnp.broadcast_to(run_ref[...], cnt_ref.shape)

    cols = ((i1 - N_GROUPS).astype(F32), (i2 - N_GROUPS).astype(F32), c1, c2, pos1, pos2)
    rt = jnp.zeros_like(logits)
    for li, col in enumerate(cols):
        rt = jnp.where(lane == li, col, rt)
    rt_ref[...] = rt


def _outproj(o_gla, o_diff, proj, x2, mod, w_out, ln_w, ln_b, w_r, b_r, *, seq, tm, ga_col, alpha):
    t, d = x2.shape
    assert d == ROW_SUB * LANES
    kern = functools.partial(_outproj_kernel, alpha=alpha)
    row_spec = pl.BlockSpec((tm, d), lambda i: (i, 0))
    vec_spec = pl.BlockSpec((1, d), lambda i: (0, 0))
    return pl.pallas_call(
        kern,
        grid=(t // tm,),
        in_specs=[row_spec, row_spec,
                  pl.BlockSpec((tm, d), lambda i: (i, ga_col // d)),
                  pl.BlockSpec((tm, d), lambda i: (i, ga_col // d + 1)),
                  row_spec,
                  pl.BlockSpec((1,) + mod.shape[1:], lambda i: ((i * tm) // seq, 0, 0)),
                  pl.BlockSpec((d, d), lambda i: (0, 0)),
                  vec_spec, vec_spec,
                  pl.BlockSpec((d, LANES), lambda i: (0, 0)),
                  pl.BlockSpec((1, LANES), lambda i: (0, 0))],
        out_specs=[row_spec,
                   pl.BlockSpec((tm * ROW_SUB, LANES), lambda i: (i, 0)),
                   pl.BlockSpec((tm, LANES), lambda i: (i, 0)),
                   pl.BlockSpec((ROW_SUB, LANES), lambda i: (0, 0))],
        out_shape=[jax.ShapeDtypeStruct((t, d), F32),
                   jax.ShapeDtypeStruct((t * ROW_SUB, LANES), F32),
                   jax.ShapeDtypeStruct((t, LANES), F32),
                   jax.ShapeDtypeStruct((ROW_SUB, LANES), F32)],
        scratch_shapes=[pltpu.VMEM((tm, tm), BF16),
                        pltpu.VMEM((1, LANES), F32)],
        compiler_params=pltpu.CompilerParams(
            dimension_semantics=("arbitrary",), vmem_limit_bytes=VMEM_LIMIT),
        name="outproj",
    )(o_gla, o_diff, proj, proj, x2, mod, w_out, ln_w, ln_b, w_r, b_r)


def _dispatch_kernel(fill_ref, d0_ref, d1_ref, u_ref, xs_hbm, zbuf, sem_fill, sem, *, tm, blk):
    i = pl.program_id(0)

    @pl.when(i == 0)
    def _():
        zbuf[...] = jnp.zeros_like(zbuf)
        n_fill = fill_ref.shape[0]

        def fill_copy(j):
            return pltpu.make_async_copy(zbuf, xs_hbm.at[pl.ds(fill_ref[j] * ROW_SUB, blk * ROW_SUB), :], sem_fill)

        def start_fill(j, carry):
            @pl.when(fill_ref[j] >= 0)
            def _():
                fill_copy(j).start()
            return carry

        def wait_fill(j, carry):
            @pl.when(fill_ref[j] >= 0)
            def _():
                fill_copy(j).wait()
            return carry

        lax.fori_loop(0, n_fill, start_fill, 0)
        lax.fori_loop(0, n_fill, wait_fill, 0)

    def row_copy(dest_ref, r):
        return pltpu.make_async_copy(u_ref.at[pl.ds(r * ROW_SUB, ROW_SUB), :],
                                     xs_hbm.at[pl.ds(dest_ref[0, 0, r] * ROW_SUB, ROW_SUB), :], sem)

    def body(r, carry):
        row_copy(d0_ref, r).start(priority=0)
        row_copy(d1_ref, r).start(priority=1)
        return carry

    lax.fori_loop(0, tm, body, 0, unroll=4)
    for _ in range(2):
        pltpu.make_async_copy(u_ref, xs_hbm.at[pl.ds(0, tm * ROW_SUB), :], sem).wait()


def _dispatch(fill_start, dest0, dest1, u2_tiles, *, p_rows, tm, blk):
    t = u2_tiles.shape[0] // ROW_SUB
    kern = functools.partial(_dispatch_kernel, tm=tm, blk=blk)
    idx_spec = pl.BlockSpec((1, 1, tm), lambda i, fs: (i, 0, 0), memory_space=pltpu.SMEM)
    grid_spec = pltpu.PrefetchScalarGridSpec(
        num_scalar_prefetch=1,
        grid=(t // tm,),
        in_specs=[idx_spec, idx_spec, pl.BlockSpec((tm * ROW_SUB, LANES), lambda i, fs: (i, 0))],
        out_specs=pl.BlockSpec(memory_space=pl.ANY),
        scratch_shapes=[pltpu.VMEM((blk * ROW_SUB, LANES), F32),
                        pltpu.SemaphoreType.DMA(()),
                        pltpu.SemaphoreType.DMA(())],
    )
    return pl.pallas_call(
        kern,
        grid_spec=grid_spec,
        out_shape=jax.ShapeDtypeStruct((p_rows * ROW_SUB, LANES), F32),
        compiler_params=pltpu.CompilerParams(dimension_semantics=("arbitrary",)),
        name="dispatch",
    )(fill_start, dest0, dest1, u2_tiles)


def _expert_kernel(be_ref, na_ref, x_ref, wg_ref, wu_ref, wd_ref, o_ref, wgb, wub, wdb, *, blk):
    i = pl.program_id(0)
    active = i < na_ref[0]

    @pl.when(active)
    def _():
        changed = jnp.logical_or(i == 0, be_ref[i] != be_ref[jnp.maximum(i - 1, 0)])

        @pl.when(changed)
        def _():
            wgb[...] = wg_ref[0].astype(BF16)
            wub[...] = wu_ref[0].astype(BF16)
            wdb[...] = wd_ref[0].astype(BF16)

        xb = _load_row_tiles(x_ref, blk).astype(BF16)
        hg = jnp.dot(xb, wgb[...], preferred_element_type=F32)
        hu = jnp.dot(xb, wub[...], preferred_element_type=F32)
        h = (_silu(hg) * hu).astype(BF16)
        _store_row_tiles(o_ref, jnp.dot(h, wdb[...], preferred_element_type=F32))

    @pl.when(jnp.logical_not(active))
    def _():
        o_ref[...] = jnp.zeros_like(o_ref)


def _experts(block_expert, n_active, xs_tiles, w_gate, w_up, w_down, *, blk):
    nblk = block_expert.shape[0]
    e, d, ff = w_gate.shape
    kern = functools.partial(_expert_kernel, blk=blk)
    grid_spec = pltpu.PrefetchScalarGridSpec(
        num_scalar_prefetch=2,
        grid=(nblk,),
        in_specs=[pl.BlockSpec((blk * ROW_SUB, LANES), lambda i, be, na: (jnp.minimum(i, na[0] - 1), 0)),
                  pl.BlockSpec((1, d, ff), lambda i, be, na: (be[i], 0, 0)),
                  pl.BlockSpec((1, d, ff), lambda i, be, na: (be[i], 0, 0)),
                  pl.BlockSpec((1, ff, d), lambda i, be, na: (be[i], 0, 0))],
        out_specs=pl.BlockSpec((blk * ROW_SUB, LANES), lambda i, be, na: (i, 0)),
        scratch_shapes=[pltpu.VMEM((d, ff), BF16),
                        pltpu.VMEM((d, ff), BF16),
                        pltpu.VMEM((ff, d), BF16)],
    )
    return pl.pallas_call(
        kern,
        grid_spec=grid_spec,
        out_shape=jax.ShapeDtypeStruct((nblk * blk * ROW_SUB, LANES), F32),
        compiler_params=pltpu.CompilerParams(
            dimension_semantics=("arbitrary",), vmem_limit_bytes=VMEM_LIMIT),
        name="experts",
    )(block_expert, n_active, xs_tiles, w_gate, w_up, w_down)


def _combine_kernel(d0_ref, d0n_ref, d1_ref, d1n_ref, rows_hbm, x1_ref, rt_ref, mod_ref, lnw_ref, lnb_ref,
                    o_ref, buf, sem, *, tm, alpha):
    i = pl.program_id(0)
    n = pl.num_programs(0)
    slot = i % 2

    def start_gather(i0_ref, i1_ref, s):
        def body(r, carry):
            _tile_copy(rows_hbm, buf.at[s, 0], sem.at[s], i0_ref[0, 0, r], r).start(priority=0)
            _tile_copy(rows_hbm, buf.at[s, 1], sem.at[s], i1_ref[0, 0, r], r).start(priority=1)
            return carry
        lax.fori_loop(0, tm, body, 0, unroll=4)

    def wait_gather(s):
        _wait_tiles(rows_hbm, buf.at[s, 0], sem.at[s])
        _wait_tiles(rows_hbm, buf.at[s, 1], sem.at[s])

    @pl.when(i == 0)
    def _():
        start_gather(d0_ref, d1_ref, 0)

    start_gather(d0n_ref, d1n_ref, 1 - slot)
    wait_gather(slot)
    rt = rt_ref[...]
    y = rt[:, 2:3] * _load_row_tiles(buf.at[slot, 0], tm) + rt[:, 3:4] * _load_row_tiles(buf.at[slot, 1], tm)
    gate2 = mod_ref[0, 5:6, :]
    z = alpha * x1_ref[...] + gate2 * y
    o_ref[...] = _layer_norm(z) * lnw_ref[...] + lnb_ref[...]

    @pl.when(i == n - 1)
    def _():
        wait_gather(1 - slot)


def _combine(dest0, dest1, rows_tiles, x1, route, mod, ln_w, ln_b, *, seq, tm, alpha):
    t, d = x1.shape
    nt = t // tm
    kern = functools.partial(_combine_kernel, tm=tm, alpha=alpha)
    cur = lambda i: (i, 0, 0)
    nxt = lambda i: (jnp.minimum(i + 1, nt - 1), 0, 0)
    idx_spec = lambda m: pl.BlockSpec((1, 1, tm), m, memory_space=pltpu.SMEM)
    row_spec = pl.BlockSpec((tm, d), lambda i: (i, 0))
    vec_spec = pl.BlockSpec((1, d), lambda i: (0, 0))
    return pl.pallas_call(
        kern,
        grid=(nt,),
        in_specs=[idx_spec(cur), idx_spec(nxt), idx_spec(cur), idx_spec(nxt),
                  pl.BlockSpec(memory_space=pl.ANY),
                  row_spec,
                  pl.BlockSpec((tm, LANES), lambda i: (i, 0)),
                  pl.BlockSpec((1,) + mod.shape[1:], lambda i: ((i * tm) // seq, 0, 0)),
                  vec_spec, vec_spec],
        out_specs=row_spec,
        out_shape=jax.ShapeDtypeStruct((t, d), F32),
        scratch_shapes=[pltpu.VMEM((2, 2, tm * ROW_SUB, LANES), F32),
                        pltpu.SemaphoreType.DMA((2,))],
        compiler_params=pltpu.CompilerParams(
            dimension_semantics=("arbitrary",), vmem_limit_bytes=VMEM_LIMIT),
        name="combine",
    )(dest0, dest0, dest1, dest1, rows_tiles, x1, route, mod, ln_w, ln_b)


def _dispatch_plan(route, counts, blk):
    t = route.shape[0]
    eid = route[:, :2].astype(jnp.int32)
    pos = route[:, 4:6].astype(jnp.int32)
    counts = counts[:N_EXPERTS].astype(jnp.int32)
    pcounts = ((counts + blk - 1) // blk) * blk
    pend = jnp.cumsum(pcounts)
    pstart = pend - pcounts
    experts = jnp.arange(N_EXPERTS, dtype=jnp.int32)
    dest = jnp.sum(jnp.where(eid[:, :, None] == experts, pstart, 0), axis=-1) + pos
    p_rows = ((2 * t + N_EXPERTS * (blk - 1) + blk - 1) // blk) * blk
    nblk = p_rows // blk
    block_start = jnp.arange(nblk, dtype=jnp.int32) * blk
    block_expert = jnp.minimum(jnp.sum((pend[None, :] <= block_start[:, None]).astype(jnp.int32), axis=1),
                               N_EXPERTS - 1)
    n_active = (pend[N_EXPERTS - 1:] // blk).astype(jnp.int32)
    tail = pend[N_EXPERTS - 1] + jnp.arange(N_EXPERTS, dtype=jnp.int32) * blk
    fill_start = jnp.concatenate([jnp.where(pcounts > 0, pend - blk, -1), jnp.where(tail < p_rows, tail, -1)])
    return dest[:, 0], dest[:, 1], fill_start, block_expert, n_active, p_rows


def _layer(x, c, positions, w_ada, b_ada, w_in, w_g2, b_g2, gla_nw, lq1, lk1, lq2, lk2, diff_nw, w_out,
           ln1_w, ln1_b, w_rg, b_rg, w_re, b_re, w_eg, w_eu, w_ed, ln2_w, ln2_b, *, lam_init,
           tm_in, tn_in, gla_rows, bq, bk, tm_out, moe_blk, tm_dsp, tm_cmb):
    bsz, seq, d = x.shape
    t = bsz * seq
    alpha = (2.0 * DEPTH) ** 0.25
    gla_dk = d // (2 * GLA_HEADS)
    gla_dv = d // GLA_HEADS
    dh = d // (2 * DIFF_HEADS)
    hk = GLA_HEADS * gla_dk
    hv = GLA_HEADS * gla_dv
    dq = DIFF_HEADS * 2 * dh
    gr_col = 2 * hk + 2 * hv
    q_col, k_col = 0, hk
    v_col, g_col = 2 * hk, 2 * hk + hv
    dq_col = gr_col
    dk_col = dq_col + dq
    dv_col = dk_col + dq
    mg_col = dv_col + dq
    half = dh // 2
    lane = np.arange(2 * dh)
    in_head = (lane // half % 2) * dh + (lane // dh) * half + lane % half
    cols = (np.arange(DIFF_HEADS)[:, None] * 2 * dh + in_head[None, :]).reshape(-1)
    src = np.arange(w_in.shape[1] - GLA_GATE_RANK)
    src[dq_col:dq_col + dq] = dq_col + cols
    src[dk_col:dk_col + dq] = dk_col + cols
    src = np.where(src >= gr_col, src + GLA_GATE_RANK, src)
    w_main = w_in[:, src].astype(BF16)
    w_gr = jnp.pad(w_in[:, gr_col:gr_col + GLA_GATE_RANK], ((0, 0), (0, LANES - GLA_GATE_RANK))).astype(BF16)
    w_g2p = jnp.pad(w_g2, ((0, LANES - GLA_GATE_RANK), (0, 0))).astype(BF16)
    x2 = x.reshape(t, d)

    ada = _ada(c, w_ada, b_ada.reshape(1, -1))
    mod = ada.reshape(bsz, 6, d)

    inv = ROPE_THETA ** (-jnp.arange(half, dtype=F32) / half)
    inv_row = jnp.tile(inv, LANES // half).reshape(1, LANES)
    cs = _rope_tables(positions.reshape(t, 1), inv_row, tm_in)

    proj, log_g = _inproj(x2, mod, cs, w_main, w_gr, w_g2p, b_g2.reshape(1, -1), seq=seq, tm=tm_in, tn=tn_in,
                          q_col=dq_col, k_col=dk_col, rope_cols=dq, q_scale=dh ** -0.5 * LOG2_E)

    o_gla = _gla(proj, log_g, gla_nw.reshape(1, -1), bsz=bsz, seq=seq, rows=gla_rows, dk=gla_dk, dv=gla_dv,
                 q_col=q_col, k_col=k_col, v_col=v_col, g_col=g_col)

    lam_vecs = jnp.pad(jnp.stack([lq1, lk1, lq2, lk2]), ((0, 4), (0, LANES - dh)))
    o_diff = _diff(lam_vecs, proj, diff_nw.reshape(1, -1), bsz=bsz, seq=seq, bq=bq, bk=bk, dh=dh,
                   q_col=dq_col, k_col=dk_col, v_col=dv_col, lam_init=lam_init)

    w_r = jnp.pad(jnp.concatenate([w_rg, w_re], axis=1), ((0, 0), (0, LANES - N_GROUPS - N_EXPERTS))).astype(BF16)
    b_r = jnp.pad(jnp.concatenate([b_rg, b_re]), (0, LANES - N_GROUPS - N_EXPERTS)).reshape(1, LANES)
    x1, u2_tiles, route, counts = _outproj(o_gla, o_diff, proj, x2, mod, w_out.astype(BF16), ln1_w.reshape(1, -1),
                                           ln1_b.reshape(1, -1), w_r, b_r, seq=seq, tm=tm_out, ga_col=mg_col,
                                           alpha=alpha)

    dest0, dest1, fill_start, block_expert, n_active, p_rows = _dispatch_plan(route, counts[0], moe_blk)
    xs_tiles = _dispatch(fill_start, dest0.reshape(-1, 1, tm_dsp), dest1.reshape(-1, 1, tm_dsp), u2_tiles,
                         p_rows=p_rows, tm=tm_dsp, blk=moe_blk)
    rows = _experts(block_expert, n_active, xs_tiles, w_eg, w_eu, w_ed, blk=moe_blk)
    out = _combine(dest0.reshape(-1, 1, tm_cmb), dest1.reshape(-1, 1, tm_cmb), rows, x1, route, mod,
                   ln2_w.reshape(1, -1), ln2_b.reshape(1, -1), seq=seq, tm=tm_cmb, alpha=alpha)
    return out.reshape(bsz, seq, d)


def kernel(x, c, positions, w_ada, b_ada, w_in, w_gla_gate2, b_gla_gate2, gla_norm_w, diff_lambda_q1,
           diff_lambda_k1, diff_lambda_q2, diff_lambda_k2, diff_norm_w, w_out, ln1_w, ln1_b, w_router_group,
           b_router_group, w_router_expert, b_router_expert, w_exp_gate, w_exp_up, w_exp_down, ln2_w, ln2_b):
    assert w_ada.shape[0] == DEPTH
    for l in range(DEPTH):
        lam_init = 0.8 - 0.6 * math.exp(-0.3 * l)
        x = _layer(x, c, positions, w_ada[l], b_ada[l], w_in[l], w_gla_gate2[l], b_gla_gate2[l], gla_norm_w[l],
                   diff_lambda_q1[l], diff_lambda_k1[l], diff_lambda_q2[l], diff_lambda_k2[l], diff_norm_w[l],
                   w_out[l], ln1_w[l], ln1_b[l], w_router_group[l], b_router_group[l], w_router_expert[l],
                   b_router_expert[l], w_exp_gate[l], w_exp_up[l], w_exp_down[l], ln2_w[l], ln2_b[l],
                   lam_init=lam_init, tm_in=1024, tn_in=2048, gla_rows=2048, bq=512, bk=512, tm_out=512,
                   moe_blk=256, tm_dsp=1024, tm_cmb=512)
    return x
```

```python
import functools
import math

import jax
import jax.numpy as jnp
import numpy as np
from jax import lax
from jax.experimental import pallas as pl
from jax.experimental.pallas import tpu as pltpu

F32 = jnp.float32
BF16 = jnp.bfloat16
HIGHEST = lax.Precision.HIGHEST

DEPTH = 1
GLA_HEADS = 4
GLA_GATE_RANK = 16
GLA_TAU = 16.0
GLA_CHUNK = 64
GLA_SUB_ROWS = 256
DIFF_HEADS = 8
ROPE_THETA = 10000.0
N_GROUPS = 4
EXPERTS_PER_GROUP = 8
N_EXPERTS = N_GROUPS * EXPERTS_PER_GROUP
LN_EPS = 1e-5
RMS_EPS = 1e-6
LOG2_E = math.log2(math.e)
LANES = 128
VMEM_LIMIT = 56 * 1024 * 1024

NT_DIMS = (((1,), (1,)), ((), ()))
TN_DIMS = (((0,), (0,)), ((), ()))


def _layer_norm(x):
    mu = jnp.mean(x, axis=-1, keepdims=True)
    xc = x - mu
    var = jnp.mean(xc * xc, axis=-1, keepdims=True)
    return xc * lax.rsqrt(var + LN_EPS)


def _silu(x):
    return x * jax.nn.sigmoid(x)


def _ada_kernel(c_ref, w_ref, b_ref, o_ref):
    s = _silu(c_ref[...])
    o_ref[...] = jnp.dot(s, w_ref[...], preferred_element_type=F32, precision=HIGHEST) + b_ref[...]


def _ada(c, w, b):
    bsz, d = c.shape
    n = w.shape[1]
    tn = d
    return pl.pallas_call(
        _ada_kernel,
        grid=(n // tn,),
        in_specs=[pl.BlockSpec((bsz, d), lambda j: (0, 0)),
                  pl.BlockSpec((d, tn), lambda j: (0, j)),
                  pl.BlockSpec((1, tn), lambda j: (0, j))],
        out_specs=pl.BlockSpec((bsz, tn), lambda j: (0, j)),
        out_shape=jax.ShapeDtypeStruct((bsz, n), F32),
        name="ada",
    )(c, w, b)


ROPE_PACK = 4


def _rope_kernel(pos_ref, inv_ref, cos_ref, sin_ref):
    rows = pos_ref.shape[0]
    group = LANES // ROPE_PACK
    lane = lax.broadcasted_iota(jnp.int32, (rows, LANES), 1)
    p = pos_ref[...].astype(F32)
    pos = p[:, ROPE_PACK - 1:ROPE_PACK]
    for k in range(ROPE_PACK - 2, -1, -1):
        pos = jnp.where(lane // group == k, p[:, k:k + 1], pos)
    ang = pos * inv_ref[...]
    cos = jnp.cos(ang)
    sin = jnp.sin(ang)

    def spread(x, k):
        y = jnp.where(lane // group == k, x, 0.0)
        out = y
        for g in range(1, ROPE_PACK):
            out = out + pltpu.roll(y, g * group, 1)
        return out

    for k in range(ROPE_PACK):
        cos_ref[pl.ds(k, rows, stride=ROPE_PACK), :] = spread(cos, k)
        s_k = spread(sin, k)
        sin_ref[pl.ds(k, rows, stride=ROPE_PACK), :] = jnp.where(lane < LANES // 2, -s_k, s_k)


def _rope_tables(pos_packed, inv_row, tm):
    t = pos_packed.shape[0] * ROPE_PACK
    out = jax.ShapeDtypeStruct((t, LANES), F32)
    return pl.pallas_call(
        _rope_kernel,
        grid=(t // tm,),
        in_specs=[pl.BlockSpec((tm // ROPE_PACK, ROPE_PACK), lambda i: (i, 0)),
                  pl.BlockSpec((1, LANES), lambda i: (0, 0))],
        out_specs=[pl.BlockSpec((tm, LANES), lambda i: (i, 0)), pl.BlockSpec((tm, LANES), lambda i: (i, 0))],
        out_shape=[out, out],
        name="rope_tables",
    )(pos_packed, inv_row)


def _inproj_kernel(x_ref, mod_ref, cos_ref, sin_ref, w_ref, wgr_ref, wg2_ref, bg2_ref, o_ref, lg_ref, u_ref,
                   *, rope_slabs, q_scale):
    j = pl.program_id(1)

    @pl.when(j == 0)
    def _():
        shift = mod_ref[0, 0:1, :]
        scale = mod_ref[0, 1:2, :]
        u = (_layer_norm(x_ref[...]) * (1.0 + scale) + shift).astype(BF16)
        u_ref[...] = u
        gr = jnp.dot(u, wgr_ref[...], preferred_element_type=F32)
        z = jnp.dot(gr.astype(BF16), wg2_ref[...], preferred_element_type=F32) + bg2_ref[...]
        log_sig = jnp.minimum(z, 0.0) - jnp.log(1.0 + jnp.exp(-jnp.abs(z)))
        lg_ref[...] = log_sig * (1.0 / GLA_TAU)

    acc = jnp.dot(u_ref[...], w_ref[...], preferred_element_type=F32)
    plain = j >= 0
    for tile, slabs in rope_slabs.items():
        plain = jnp.logical_and(plain, j != tile)

        @pl.when(j == tile)
        def _(slabs=slabs):
            cos = {False: cos_ref[...]}
            sin = {False: sin_ref[...]}
            if any(slabs.values()):
                cos[True] = cos[False] * q_scale
                sin[True] = sin[False] * q_scale
            for h in range(acc.shape[1] // LANES):
                t = acc[:, h * LANES:(h + 1) * LANES]
                if h in slabs:
                    partner = pltpu.roll(t, LANES // 2, 1)
                    t = t * cos[slabs[h]] + partner * sin[slabs[h]]
                o_ref[:, h * LANES:(h + 1) * LANES] = t.astype(o_ref.dtype)

    @pl.when(plain)
    def _():
        o_ref[...] = acc.astype(o_ref.dtype)


def _inproj(x2, mod, cos, sin, w_main, w_gr, w_g2, b_g2, *, seq, tm, tn, q_col, k_col, rope_cols, q_scale):
    t, d = x2.shape
    n = w_main.shape[1]
    ng = w_g2.shape[1]
    rope_slabs = {}
    for col in range(0, n, LANES):
        for start, is_q in ((q_col, True), (k_col, False)):
            if start <= col < start + rope_cols:
                rope_slabs.setdefault(col // tn, {})[(col % tn) // LANES] = is_q
    kern = functools.partial(_inproj_kernel, rope_slabs=rope_slabs, q_scale=q_scale)
    return pl.pallas_call(
        kern,
        grid=(t // tm, n // tn),
        in_specs=[pl.BlockSpec((tm, d), lambda i, j: (i, 0)),
                  pl.BlockSpec((1,) + mod.shape[1:], lambda i, j: ((i * tm) // seq, 0, 0)),
                  pl.BlockSpec((tm, LANES), lambda i, j: (i, 0)),
                  pl.BlockSpec((tm, LANES), lambda i, j: (i, 0)),
                  pl.BlockSpec((d, tn), lambda i, j: (0, j)),
                  pl.BlockSpec((d, LANES), lambda i, j: (0, 0)),
                  pl.BlockSpec((LANES, ng), lambda i, j: (0, 0)),
                  pl.BlockSpec((1, ng), lambda i, j: (0, 0))],
        out_specs=[pl.BlockSpec((tm, tn), lambda i, j: (i, j)),
                   pl.BlockSpec((tm, ng), lambda i, j: (i, 0))],
        out_shape=[jax.ShapeDtypeStruct((t, n), BF16),
                   jax.ShapeDtypeStruct((t, ng), F32)],
        scratch_shapes=[pltpu.VMEM((tm, d), BF16)],
        compiler_params=pltpu.CompilerParams(
            dimension_semantics=("arbitrary", "arbitrary"), vmem_limit_bytes=VMEM_LIMIT),
        name="inproj",
    )(x2, mod, cos, sin, w_main, w_gr, w_g2, b_g2)


def _gla_kernel(q_ref, k_ref, v_ref, g_ref, lg_ref, nw_ref, o_ref, st_ref, *, chunk, sub_rows, q_scale):
    @pl.when(pl.program_id(2) == 0)
    def _():
        st_ref[...] = jnp.zeros_like(st_ref)

    n_chunks = sub_rows // chunk
    pos = lax.broadcasted_iota(jnp.int32, (sub_rows, lg_ref.shape[1]), 0) % chunk
    r = lax.broadcasted_iota(jnp.int32, (sub_rows, sub_rows), 0)
    c = lax.broadcasted_iota(jnp.int32, (sub_rows, sub_rows), 1)
    keep = jnp.logical_and(c <= r, (r // chunk) == (c // chunk))
    st = st_ref[...]

    for si in range(q_ref.shape[0] // sub_rows):
        rs = slice(si * sub_rows, (si + 1) * sub_rows)

        b = lg_ref[rs, :]
        step = 1
        while step < chunk:
            b = b + jnp.where(pos >= step, pltpu.roll(b, step, 0), 0.0)
            step *= 2

        k = k_ref[rs, :].astype(F32)
        q_t = (q_ref[rs, :].astype(F32) * q_scale * jnp.exp(b)).astype(BF16)
        k_t = (k * jnp.exp(-b)).astype(BF16)
        v = v_ref[rs, :]

        attn = lax.dot_general(q_t, k_t, NT_DIMS, preferred_element_type=F32)
        o = jnp.dot(jnp.where(keep, attn, 0.0).astype(BF16), v, preferred_element_type=F32)

        o_inter = []
        for ci in range(n_chunks):
            sl = slice(ci * chunk, (ci + 1) * chunk)
            b_c = b[sl, :]
            b_last = b_c[chunk - 1:chunk, :]
            k_d = (k[sl, :] * jnp.exp(b_last - b_c)).astype(BF16)
            kv = lax.dot_general(v[sl, :], k_d, TN_DIMS, preferred_element_type=F32)
            o_inter.append(lax.dot_general(q_t[sl, :], st.astype(BF16), NT_DIMS, preferred_element_type=F32))
            st = st * jnp.exp(b_last) + kv
        o = o + jnp.concatenate(o_inter, axis=0)

        ms = jnp.mean(o * o, axis=-1, keepdims=True)
        y = o * lax.rsqrt(ms + RMS_EPS) * nw_ref[...] * _silu(g_ref[rs, :].astype(F32))
        o_ref[rs, :] = y.astype(o_ref.dtype)

    st_ref[...] = st


def _gla(proj, log_g, norm_w, *, bsz, seq, rows, dk, dv, q_col, k_col, v_col, g_col):
    t = proj.shape[0]
    nl = seq // rows
    kern = functools.partial(_gla_kernel, chunk=GLA_CHUNK, sub_rows=min(rows, GLA_SUB_ROWS), q_scale=dk ** -0.5)
    row = lambda b, h, l: b * nl + l
    return pl.pallas_call(
        kern,
        grid=(bsz, GLA_HEADS, nl),
        in_specs=[pl.BlockSpec((rows, dk), lambda b, h, l: (row(b, h, l), q_col // dk + h)),
                  pl.BlockSpec((rows, dk), lambda b, h, l: (row(b, h, l), k_col // dk + h)),
                  pl.BlockSpec((rows, dv), lambda b, h, l: (row(b, h, l), v_col // dv + h)),
                  pl.BlockSpec((rows, dv), lambda b, h, l: (row(b, h, l), g_col // dv + h)),
                  pl.BlockSpec((rows, dk), lambda b, h, l: (row(b, h, l), h)),
                  pl.BlockSpec((1, dv), lambda b, h, l: (0, 0))],
        out_specs=pl.BlockSpec((rows, dv), lambda b, h, l: (row(b, h, l), h)),
        out_shape=jax.ShapeDtypeStruct((t, GLA_HEADS * dv), BF16),
        scratch_shapes=[pltpu.VMEM((dv, dk), F32)],
        compiler_params=pltpu.CompilerParams(
            dimension_semantics=("arbitrary", "arbitrary", "arbitrary"), vmem_limit_bytes=VMEM_LIMIT),
        name="gla",
    )(proj, proj, proj, proj, log_g, norm_w)


def _diff_kernel(lam_ref, q_ref, qn_ref, k_ref, v_ref, nw_ref, o_ref,
                 qs_ref, vt_ref, s0_ref, sa_ref, sb_ref, m_ref, l_ref, acc_ref, *, bq, bk, dh, lam_init):
    qi = pl.program_id(2)
    q0 = qi * bq
    seq = k_ref.shape[0]

    def mask_queries(q):
        lane = lax.broadcasted_iota(jnp.int32, q.shape, 1)
        zero = jnp.zeros_like(q)
        map0 = (lane % dh) < dh // 2
        qs_ref[:bq, :] = jnp.where(map0, q, zero)
        qs_ref[bq:, :] = jnp.where(map0, zero, q)

    def scores(kj, dst):
        k0 = pl.multiple_of(kj * bk, bk)
        dst[...] = lax.dot_general(k_ref[pl.ds(k0, bk), :], qs_ref[...], NT_DIMS, preferred_element_type=F32)

    def update(src, kj, masked):
        k0 = pl.multiple_of(kj * bk, bk)
        s = src[...]
        if masked:
            key = k0 + lax.broadcasted_iota(jnp.int32, s.shape, 0)
            col = lax.broadcasted_iota(jnp.int32, s.shape, 1)
            qpos = q0 + jnp.where(col >= bq, col - bq, col)
            s = jnp.where(key <= qpos, s, -jnp.inf)
        m_prev = m_ref[...]
        m_new = jnp.maximum(m_prev, jnp.max(s, axis=0, keepdims=True))
        alpha = jnp.exp2(m_prev - m_new)
        p = jnp.exp2(s - m_new)
        l_ref[...] = alpha * l_ref[...] + jnp.sum(p, axis=0, keepdims=True)
        acc_ref[...] = alpha * acc_ref[...] + jnp.dot(
            vt_ref[:, pl.ds(k0, bk)], p.astype(BF16), preferred_element_type=F32)
        m_ref[...] = m_new

    @pl.when(qi == 0)
    def _():
        for c in range(seq // bk):
            vt_ref[:, c * bk:(c + 1) * bk] = v_ref[c * bk:(c + 1) * bk, :].astype(F32).T.astype(BF16)
        mask_queries(q_ref[...])
        scores(0, s0_ref)

    m_ref[...] = jnp.full_like(m_ref, -jnp.inf)
    l_ref[...] = jnp.zeros_like(l_ref)
    acc_ref[...] = jnp.zeros_like(acc_ref)

    n_full = (q0 + 1) // bk

    @pl.when(n_full == 0)
    def _():
        update(s0_ref, 0, True)

    @pl.when(n_full > 0)
    def _():
        scores(1, sa_ref)
        update(s0_ref, 0, False)
        rest = n_full - 1

        def pair_body(jj, carry):
            j = 1 + 2 * jj
            scores(j + 1, sb_ref)
            update(sa_ref, j, False)
            scores(j + 2, sa_ref)
            update(sb_ref, j + 1, False)
            return carry

        lax.fori_loop(0, rest // 2, pair_body, 0)
        odd = (rest % 2) == 1

        @pl.when(odd)
        def _():
            scores(n_full, sb_ref)
            update(sa_ref, n_full - 1, False)
            update(sb_ref, n_full, True)

        @pl.when(jnp.logical_not(odd))
        def _():
            update(sa_ref, n_full, True)

    mask_queries(qn_ref[...])
    scores(0, s0_ref)

    lv = lam_ref[...]
    s1 = jnp.sum(lv[0:1, :] * lv[1:2, :], axis=-1, keepdims=True)
    s2 = jnp.sum(lv[2:3, :] * lv[3:4, :], axis=-1, keepdims=True)
    lam = jnp.exp(s1) - jnp.exp(s2) + lam_init
    o_all = acc_ref[...] / l_ref[...]
    o_t = o_all[:, :bq] - lam * o_all[:, bq:]
    ms = jnp.mean(o_t * o_t, axis=0, keepdims=True)
    y = (o_t * lax.rsqrt(ms + RMS_EPS)).T * nw_ref[...] * (1.0 - lam_init)
    o_ref[...] = y.astype(o_ref.dtype)


def _diff(lam_vecs, proj, norm_w, *, bsz, seq, bq, bk, dh, q_col, k_col, v_col, lam_init):
    assert bk % bq == 0 and seq % bk == 0
    t = proj.shape[0]
    w = 2 * dh
    nq = seq // bq
    kern = functools.partial(_diff_kernel, bq=bq, bk=bk, dh=dh, lam_init=lam_init)
    return pl.pallas_call(
        kern,
        grid=(bsz, DIFF_HEADS, nq),
        in_specs=[pl.BlockSpec(lam_vecs.shape, lambda b, h, i: (0, 0)),
                  pl.BlockSpec((bq, w), lambda b, h, i: (b * nq + i, q_col // w + h)),
                  pl.BlockSpec((bq, w), lambda b, h, i: (b * nq + jnp.minimum(i + 1, nq - 1), q_col // w + h)),
                  pl.BlockSpec((seq, w), lambda b, h, i: (b, k_col // w + h)),
                  pl.BlockSpec((seq, w), lambda b, h, i: (b, v_col // w + h)),
                  pl.BlockSpec((1, w), lambda b, h, i: (0, 0))],
        out_specs=pl.BlockSpec((bq, w), lambda b, h, i: (b * nq + i, h)),
        out_shape=jax.ShapeDtypeStruct((t, DIFF_HEADS * w), BF16),
        scratch_shapes=[pltpu.VMEM((2 * bq, w), BF16),
                        pltpu.VMEM((w, seq), BF16),
                        pltpu.VMEM((bk, 2 * bq), F32),
                        pltpu.VMEM((bk, 2 * bq), F32),
                        pltpu.VMEM((bk, 2 * bq), F32),
                        pltpu.VMEM((1, 2 * bq), F32),
                        pltpu.VMEM((1, 2 * bq), F32),
                        pltpu.VMEM((w, 2 * bq), F32)],
        compiler_params=pltpu.CompilerParams(
            dimension_semantics=("arbitrary", "arbitrary", "arbitrary"), vmem_limit_bytes=VMEM_LIMIT),
        name="diff_attn",
    )(lam_vecs, proj, proj, proj, proj, norm_w)


ROW_SUB = 8


def _store_row_tiles(ref, val):
    rows = val.shape[0]
    for s in range(ROW_SUB):
        ref[pl.ds(s, rows, stride=ROW_SUB), :] = val[:, s * LANES:(s + 1) * LANES]


def _load_row_tiles(ref, rows):
    return jnp.concatenate([ref[pl.ds(s, rows, stride=ROW_SUB), :] for s in range(ROW_SUB)], axis=1)


def _tile_copy(src_hbm, dst_vmem, sem, src_row, dst_row):
    return pltpu.make_async_copy(src_hbm.at[pl.ds(src_row * ROW_SUB, ROW_SUB), :],
                                 dst_vmem.at[pl.ds(dst_row * ROW_SUB, ROW_SUB), :], sem)


def _wait_tiles(src_hbm, dst_vmem, sem):
    pltpu.make_async_copy(src_hbm.at[pl.ds(0, dst_vmem.shape[0]), :], dst_vmem, sem).wait()


def _outproj_kernel(og_ref, od_ref, ga_ref, gb_ref, x_ref, mod_ref, wo_ref, lnw_ref, lnb_ref, wr_ref, br_ref,
                    x1_ref, u2_ref, rt_ref, cnt_ref, tri_ref, run_ref, *, alpha):
    @pl.when(pl.program_id(0) == 0)
    def _():
        r = lax.broadcasted_iota(jnp.int32, tri_ref.shape, 0)
        c = lax.broadcasted_iota(jnp.int32, tri_ref.shape, 1)
        tri_ref[...] = (c < r).astype(BF16)
        run_ref[...] = jnp.zeros_like(run_ref)

    merged = jax.nn.sigmoid(ga_ref[...]) * og_ref[...] + jax.nn.sigmoid(gb_ref[...]) * od_ref[...]
    y = jnp.dot(merged, wo_ref[...], preferred_element_type=F32)
    gate1 = mod_ref[0, 2:3, :]
    shift2 = mod_ref[0, 3:4, :]
    scale2 = mod_ref[0, 4:5, :]
    x1 = _layer_norm(alpha * x_ref[...] + gate1 * y) * lnw_ref[...] + lnb_ref[...]
    x1_ref[...] = x1
    u2 = _layer_norm(x1) * (1.0 + scale2) + shift2
    _store_row_tiles(u2_ref, u2)

    logits = jnp.dot(u2.astype(BF16), wr_ref[...], preferred_element_type=F32) + br_ref[...]
    lane = lax.broadcasted_iota(jnp.int32, logits.shape, 1)
    neg = jnp.float32(-jnp.inf)
    big = jnp.int32(LANES)
    lg = jnp.where(lane < N_GROUPS, logits, neg)
    g_max = jnp.max(lg, axis=-1, keepdims=True)
    w_grp = 1.0 / jnp.sum(jnp.exp(lg - g_max), axis=-1, keepdims=True)
    g_idx = jnp.min(jnp.where(lg == g_max, lane, big), axis=-1, keepdims=True)
    lo = N_GROUPS + EXPERTS_PER_GROUP * g_idx
    le = jnp.where(jnp.logical_and(lane >= lo, lane < lo + EXPERTS_PER_GROUP), logits, neg)
    v1 = jnp.max(le, axis=-1, keepdims=True)
    i1 = jnp.min(jnp.where(le == v1, lane, big), axis=-1, keepdims=True)
    le2 = jnp.where(lane == i1, neg, le)
    v2 = jnp.max(le2, axis=-1, keepdims=True)
    i2 = jnp.min(jnp.where(le2 == v2, lane, big), axis=-1, keepdims=True)
    e2 = jnp.exp(v2 - v1)
    den = 1.0 + e2
    c1 = w_grp / den
    c2 = w_grp * e2 / den

    sel1 = lane == i1 - N_GROUPS
    sel2 = lane == i2 - N_GROUPS
    onehot = jnp.logical_or(sel1, sel2).astype(BF16)
    before = jnp.dot(tri_ref[...], onehot, preferred_element_type=F32) + run_ref[...]
    pos1 = jnp.sum(jnp.where(sel1, before, 0.0), axis=-1, keepdims=True)
    pos2 = jnp.sum(jnp.where(sel2, before, 0.0), axis=-1, keepdims=True)
    run_ref[...] += jnp.sum(onehot.astype(F32), axis=0, keepdims=True)
    cnt_ref[...] = jnp.broadcast_to(run_ref[...], cnt_ref.shape)

    cols = ((i1 - N_GROUPS).astype(F32), (i2 - N_GROUPS).astype(F32), c1, c2, pos1, pos2)
    rt = jnp.zeros_like(logits)
    for li, col in enumerate(cols):
        rt = jnp.where(lane == li, col, rt)
    rt_ref[...] = rt


def _outproj(o_gla, o_diff, proj, x2, mod, w_out, ln_w, ln_b, w_r, b_r, *, seq, tm, ga_col, alpha):
    t, d = x2.shape
    assert d == ROW_SUB * LANES
    kern = functools.partial(_outproj_kernel, alpha=alpha)
    row_spec = pl.BlockSpec((tm, d), lambda i: (i, 0))
    vec_spec = pl.BlockSpec((1, d), lambda i: (0, 0))
    return pl.pallas_call(
        kern,
        grid=(t // tm,),
        in_specs=[row_spec, row_spec,
                  pl.BlockSpec((tm, d), lambda i: (i, ga_col // d)),
                  pl.BlockSpec((tm, d), lambda i: (i, ga_col // d + 1)),
                  row_spec,
                  pl.BlockSpec((1,) + mod.shape[1:], lambda i: ((i * tm) // seq, 0, 0)),
                  pl.BlockSpec((d, d), lambda i: (0, 0)),
                  vec_spec, vec_spec,
                  pl.BlockSpec((d, LANES), lambda i: (0, 0)),
                  pl.BlockSpec((1, LANES), lambda i: (0, 0))],
        out_specs=[row_spec,
                   pl.BlockSpec((tm * ROW_SUB, LANES), lambda i: (i, 0)),
                   pl.BlockSpec((tm, LANES), lambda i: (i, 0)),
                   pl.BlockSpec((ROW_SUB, LANES), lambda i: (0, 0))],
        out_shape=[jax.ShapeDtypeStruct((t, d), F32),
                   jax.ShapeDtypeStruct((t * ROW_SUB, LANES), F32),
                   jax.ShapeDtypeStruct((t, LANES), F32),
                   jax.ShapeDtypeStruct((ROW_SUB, LANES), F32)],
        scratch_shapes=[pltpu.VMEM((tm, tm), BF16),
                        pltpu.VMEM((1, LANES), F32)],
        compiler_params=pltpu.CompilerParams(
            dimension_semantics=("arbitrary",), vmem_limit_bytes=VMEM_LIMIT),
        name="outproj",
    )(o_gla, o_diff, proj, proj, x2, mod, w_out, ln_w, ln_b, w_r, b_r)


def _dispatch_kernel(fill_ref, d0_ref, d1_ref, u_ref, xs_hbm, zbuf, sem_fill, sem, *, tm, blk):
    i = pl.program_id(0)

    @pl.when(i == 0)
    def _():
        zbuf[...] = jnp.zeros_like(zbuf)
        n_fill = fill_ref.shape[0]

        def fill_copy(j):
            return pltpu.make_async_copy(zbuf, xs_hbm.at[pl.ds(fill_ref[j] * ROW_SUB, blk * ROW_SUB), :], sem_fill)

        def start_fill(j, carry):
            @pl.when(fill_ref[j] >= 0)
            def _():
                fill_copy(j).start()
            return carry

        def wait_fill(j, carry):
            @pl.when(fill_ref[j] >= 0)
            def _():
                fill_copy(j).wait()
            return carry

        lax.fori_loop(0, n_fill, start_fill, 0)
        lax.fori_loop(0, n_fill, wait_fill, 0)

    def row_copy(dest_ref, r):
        return pltpu.make_async_copy(u_ref.at[pl.ds(r * ROW_SUB, ROW_SUB), :],
                                     xs_hbm.at[pl.ds(dest_ref[0, 0, r] * ROW_SUB, ROW_SUB), :], sem)

    def body(r, carry):
        row_copy(d0_ref, r).start(priority=0)
        row_copy(d1_ref, r).start(priority=1)
        return carry

    lax.fori_loop(0, tm, body, 0, unroll=4)
    for _ in range(2):
        pltpu.make_async_copy(u_ref, xs_hbm.at[pl.ds(0, tm * ROW_SUB), :], sem).wait()


def _dispatch(fill_start, dest0, dest1, u2_tiles, *, p_rows, tm, blk):
    t = u2_tiles.shape[0] // ROW_SUB
    kern = functools.partial(_dispatch_kernel, tm=tm, blk=blk)
    idx_spec = pl.BlockSpec((1, 1, tm), lambda i, fs: (i, 0, 0), memory_space=pltpu.SMEM)
    grid_spec = pltpu.PrefetchScalarGridSpec(
        num_scalar_prefetch=1,
        grid=(t // tm,),
        in_specs=[idx_spec, idx_spec, pl.BlockSpec((tm * ROW_SUB, LANES), lambda i, fs: (i, 0))],
        out_specs=pl.BlockSpec(memory_space=pl.ANY),
        scratch_shapes=[pltpu.VMEM((blk * ROW_SUB, LANES), F32),
                        pltpu.SemaphoreType.DMA(()),
                        pltpu.SemaphoreType.DMA(())],
    )
    return pl.pallas_call(
        kern,
        grid_spec=grid_spec,
        out_shape=jax.ShapeDtypeStruct((p_rows * ROW_SUB, LANES), F32),
        compiler_params=pltpu.CompilerParams(dimension_semantics=("arbitrary",)),
        name="dispatch",
    )(fill_start, dest0, dest1, u2_tiles)


def _expert_kernel(be_ref, na_ref, x_ref, wg_ref, wu_ref, wd_ref, o_ref, wgb, wub, wdb, *, blk):
    i = pl.program_id(0)
    active = i < na_ref[0]

    @pl.when(active)
    def _():
        changed = jnp.logical_or(i == 0, be_ref[i] != be_ref[jnp.maximum(i - 1, 0)])

        @pl.when(changed)
        def _():
            wgb[...] = wg_ref[0].astype(BF16)
            wub[...] = wu_ref[0].astype(BF16)
            wdb[...] = wd_ref[0].astype(BF16)

        xb = _load_row_tiles(x_ref, blk).astype(BF16)
        hg = jnp.dot(xb, wgb[...], preferred_element_type=F32)
        hu = jnp.dot(xb, wub[...], preferred_element_type=F32)
        h = (_silu(hg) * hu).astype(BF16)
        _store_row_tiles(o_ref, jnp.dot(h, wdb[...], preferred_element_type=F32))

    @pl.when(jnp.logical_not(active))
    def _():
        o_ref[...] = jnp.zeros_like(o_ref)


def _experts(block_expert, n_active, xs_tiles, w_gate, w_up, w_down, *, blk):
    nblk = block_expert.shape[0]
    e, d, ff = w_gate.shape
    kern = functools.partial(_expert_kernel, blk=blk)
    grid_spec = pltpu.PrefetchScalarGridSpec(
        num_scalar_prefetch=2,
        grid=(nblk,),
        in_specs=[pl.BlockSpec((blk * ROW_SUB, LANES), lambda i, be, na: (jnp.minimum(i, na[0] - 1), 0)),
                  pl.BlockSpec((1, d, ff), lambda i, be, na: (be[i], 0, 0)),
                  pl.BlockSpec((1, d, ff), lambda i, be, na: (be[i], 0, 0)),
                  pl.BlockSpec((1, ff, d), lambda i, be, na: (be[i], 0, 0))],
        out_specs=pl.BlockSpec((blk * ROW_SUB, LANES), lambda i, be, na: (i, 0)),
        scratch_shapes=[pltpu.VMEM((d, ff), BF16),
                        pltpu.VMEM((d, ff), BF16),
                        pltpu.VMEM((ff, d), BF16)],
    )
    return pl.pallas_call(
        kern,
        grid_spec=grid_spec,
        out_shape=jax.ShapeDtypeStruct((nblk * blk * ROW_SUB, LANES), F32),
        compiler_params=pltpu.CompilerParams(
            dimension_semantics=("arbitrary",), vmem_limit_bytes=VMEM_LIMIT),
        name="experts",
    )(block_expert, n_active, xs_tiles, w_gate, w_up, w_down)


def _combine_kernel(d0_ref, d0n_ref, d1_ref, d1n_ref, rows_hbm, x1_ref, rt_ref, mod_ref, lnw_ref, lnb_ref,
                    o_ref, buf, sem, *, tm, alpha):
    i = pl.program_id(0)
    n = pl.num_programs(0)
    slot = i % 2

    def start_gather(i0_ref, i1_ref, s):
        def body(r, carry):
            _tile_copy(rows_hbm, buf.at[s, 0], sem.at[s], i0_ref[0, 0, r], r).start(priority=0)
            _tile_copy(rows_hbm, buf.at[s, 1], sem.at[s], i1_ref[0, 0, r], r).start(priority=1)
            return carry
        lax.fori_loop(0, tm, body, 0, unroll=4)

    def wait_gather(s):
        _wait_tiles(rows_hbm, buf.at[s, 0], sem.at[s])
        _wait_tiles(rows_hbm, buf.at[s, 1], sem.at[s])

    @pl.when(i == 0)
    def _():
        start_gather(d0_ref, d1_ref, 0)

    start_gather(d0n_ref, d1n_ref, 1 - slot)
    wait_gather(slot)
    rt = rt_ref[...]
    y = rt[:, 2:3] * _load_row_tiles(buf.at[slot, 0], tm) + rt[:, 3:4] * _load_row_tiles(buf.at[slot, 1], tm)
    gate2 = mod_ref[0, 5:6, :]
    z = alpha * x1_ref[...] + gate2 * y
    o_ref[...] = _layer_norm(z) * lnw_ref[...] + lnb_ref[...]

    @pl.when(i == n - 1)
    def _():
        wait_gather(1 - slot)


def _combine(dest0, dest1, rows_tiles, x1, route, mod, ln_w, ln_b, *, seq, tm, alpha):
    t, d = x1.shape
    nt = t // tm
    kern = functools.partial(_combine_kernel, tm=tm, alpha=alpha)
    cur = lambda i: (i, 0, 0)
    nxt = lambda i: (jnp.minimum(i + 1, nt - 1), 0, 0)
    idx_spec = lambda m: pl.BlockSpec((1, 1, tm), m, memory_space=pltpu.SMEM)
    row_spec = pl.BlockSpec((tm, d), lambda i: (i, 0))
    vec_spec = pl.BlockSpec((1, d), lambda i: (0, 0))
    return pl.pallas_call(
        kern,
        grid=(nt,),
        in_specs=[idx_spec(cur), idx_spec(nxt), idx_spec(cur), idx_spec(nxt),
                  pl.BlockSpec(memory_space=pl.ANY),
                  row_spec,
                  pl.BlockSpec((tm, LANES), lambda i: (i, 0)),
                  pl.BlockSpec((1,) + mod.shape[1:], lambda i: ((i * tm) // seq, 0, 0)),
                  vec_spec, vec_spec],
        out_specs=row_spec,
        out_shape=jax.ShapeDtypeStruct((t, d), F32),
        scratch_shapes=[pltpu.VMEM((2, 2, tm * ROW_SUB, LANES), F32),
                        pltpu.SemaphoreType.DMA((2,))],
        compiler_params=pltpu.CompilerParams(
            dimension_semantics=("arbitrary",), vmem_limit_bytes=VMEM_LIMIT),
        name="combine",
    )(dest0, dest0, dest1, dest1, rows_tiles, x1, route, mod, ln_w, ln_b)


def _dispatch_plan(route, counts, blk):
    t = route.shape[0]
    eid = route[:, :2].astype(jnp.int32)
    pos = route[:, 4:6].astype(jnp.int32)
    counts = counts[:N_EXPERTS].astype(jnp.int32)
    pcounts = ((counts + blk - 1) // blk) * blk
    pend = jnp.cumsum(pcounts)
    pstart = pend - pcounts
    experts = jnp.arange(N_EXPERTS, dtype=jnp.int32)
    dest = jnp.sum(jnp.where(eid[:, :, None] == experts, pstart, 0), axis=-1) + pos
    p_rows = ((2 * t + N_EXPERTS * (blk - 1) + blk - 1) // blk) * blk
    nblk = p_rows // blk
    block_start = jnp.arange(nblk, dtype=jnp.int32) * blk
    block_expert = jnp.minimum(jnp.sum((pend[None, :] <= block_start[:, None]).astype(jnp.int32), axis=1),
                               N_EXPERTS - 1)
    n_active = (pend[N_EXPERTS - 1:] // blk).astype(jnp.int32)
    tail = pend[N_EXPERTS - 1] + jnp.arange(N_EXPERTS, dtype=jnp.int32) * blk
    fill_start = jnp.concatenate([jnp.where(pcounts > 0, pend - blk, -1), jnp.where(tail < p_rows, tail, -1)])
    return dest[:, 0], dest[:, 1], fill_start, block_expert, n_active, p_rows


def _layer(x, c, positions, w_ada, b_ada, w_in, w_g2, b_g2, gla_nw, lq1, lk1, lq2, lk2, diff_nw, w_out,
           ln1_w, ln1_b, w_rg, b_rg, w_re, b_re, w_eg, w_eu, w_ed, ln2_w, ln2_b, *, lam_init,
           tm_in, tn_in, gla_rows, bq, bk, tm_out, moe_blk, tm_dsp, tm_cmb):
    bsz, seq, d = x.shape
    t = bsz * seq
    alpha = (2.0 * DEPTH) ** 0.25
    gla_dk = d // (2 * GLA_HEADS)
    gla_dv = d // GLA_HEADS
    dh = d // (2 * DIFF_HEADS)
    hk = GLA_HEADS * gla_dk
    hv = GLA_HEADS * gla_dv
    dq = DIFF_HEADS * 2 * dh
    gr_col = 2 * hk + 2 * hv
    q_col, k_col = 0, hk
    v_col, g_col = 2 * hk, 2 * hk + hv
    dq_col = gr_col
    dk_col = dq_col + dq
    dv_col = dk_col + dq
    mg_col = dv_col + dq
    w_main = jnp.concatenate([w_in[:, :gr_col], w_in[:, gr_col + GLA_GATE_RANK:]], axis=1).astype(BF16)
    half = dh // 2
    lane = np.arange(2 * dh)
    in_head = (lane // half % 2) * dh + (lane // dh) * half + lane % half
    cols = (np.arange(DIFF_HEADS)[:, None] * 2 * dh + in_head[None, :]).reshape(-1)
    w_main = jnp.concatenate([w_main[:, :dq_col], w_main[:, dq_col + cols], w_main[:, dk_col + cols],
                              w_main[:, dv_col:]], axis=1)
    w_gr = jnp.pad(w_in[:, gr_col:gr_col + GLA_GATE_RANK], ((0, 0), (0, LANES - GLA_GATE_RANK))).astype(BF16)
    w_g2p = jnp.pad(w_g2, ((0, LANES - GLA_GATE_RANK), (0, 0))).astype(BF16)
    x2 = x.reshape(t, d)

    ada = _ada(c, w_ada, b_ada.reshape(1, -1))
    mod = ada.reshape(bsz, 6, d)

    inv = ROPE_THETA ** (-jnp.arange(half, dtype=F32) / half)
    inv_row = jnp.tile(inv, LANES // half).reshape(1, LANES)
    cos, sin = _rope_tables(positions.reshape(t // ROPE_PACK, ROPE_PACK), inv_row, tm_in)

    proj, log_g = _inproj(x2, mod, cos, sin, w_main, w_gr, w_g2p, b_g2.reshape(1, -1), seq=seq, tm=tm_in, tn=tn_in,
                          q_col=dq_col, k_col=dk_col, rope_cols=dq, q_scale=dh ** -0.5 * LOG2_E)

    o_gla = _gla(proj, log_g, gla_nw.reshape(1, -1), bsz=bsz, seq=seq, rows=gla_rows, dk=gla_dk, dv=gla_dv,
                 q_col=q_col, k_col=k_col, v_col=v_col, g_col=g_col)

    lam_vecs = jnp.pad(jnp.stack([lq1, lk1, lq2, lk2]), ((0, 4), (0, LANES - dh)))
    o_diff = _diff(lam_vecs, proj, diff_nw.reshape(1, -1), bsz=bsz, seq=seq, bq=bq, bk=bk, dh=dh,
                   q_col=dq_col, k_col=dk_col, v_col=dv_col, lam_init=lam_init)

    w_r = jnp.pad(jnp.concatenate([w_rg, w_re], axis=1), ((0, 0), (0, LANES - N_GROUPS - N_EXPERTS))).astype(BF16)
    b_r = jnp.pad(jnp.concatenate([b_rg, b_re]), (0, LANES - N_GROUPS - N_EXPERTS)).reshape(1, LANES)
    x1, u2_tiles, route, counts = _outproj(o_gla, o_diff, proj, x2, mod, w_out.astype(BF16), ln1_w.reshape(1, -1),
                                           ln1_b.reshape(1, -1), w_r, b_r, seq=seq, tm=tm_out, ga_col=mg_col,
                                           alpha=alpha)

    dest0, dest1, fill_start, block_expert, n_active, p_rows = _dispatch_plan(route, counts[0], moe_blk)
    xs_tiles = _dispatch(fill_start, dest0.reshape(-1, 1, tm_dsp), dest1.reshape(-1, 1, tm_dsp), u2_tiles,
                         p_rows=p_rows, tm=tm_dsp, blk=moe_blk)
    rows = _experts(block_expert, n_active, xs_tiles, w_eg, w_eu, w_ed, blk=moe_blk)
    out = _combine(dest0.reshape(-1, 1, tm_cmb), dest1.reshape(-1, 1, tm_cmb), rows, x1, route, mod,
                   ln2_w.reshape(1, -1), ln2_b.reshape(1, -1), seq=seq, tm=tm_cmb, alpha=alpha)
    return out.reshape(bsz, seq, d)


def kernel(x, c, positions, w_ada, b_ada, w_in, w_gla_gate2, b_gla_gate2, gla_norm_w, diff_lambda_q1,
           diff_lambda_k1, diff_lambda_q2, diff_lambda_k2, diff_norm_w, w_out, ln1_w, ln1_b, w_router_group,
           b_router_group, w_router_expert, b_router_expert, w_exp_gate, w_exp_up, w_exp_down, ln2_w, ln2_b):
    assert w_ada.shape[0] == DEPTH
    for l in range(DEPTH):
        lam_init = 0.8 - 0.6 * math.exp(-0.3 * l)
        x = _layer(x, c, positions, w_ada[l], b_ada[l], w_in[l], w_gla_gate2[l], b_gla_gate2[l], gla_norm_w[l],
                   diff_lambda_q1[l], diff_lambda_k1[l], diff_lambda_q2[l], diff_lambda_k2[l], diff_norm_w[l],
                   w_out[l], ln1_w[l], ln1_b[l], w_router_group[l], b_router_group[l], w_router_expert[l],
                   b_router_expert[l], w_exp_gate[l], w_exp_up[l], w_exp_down[l], ln2_w[l], ln2_b[l],
                   lam_init=lam_init, tm_in=1024, tn_in=2048, gla_rows=2048, bq=512, bk=512, tm_out=512,
                   moe_blk=256, tm_dsp=1024, tm_cmb=512)
    return x
```
